```python
import jax
import jax.numpy as jnp
from jax import lax
import numpy as np

D_MODEL = 1024
BATCH = 16
SEQ = 2048
DEPTH = 1

MIX_WIDTH = D_MODEL
GMLP_CHUNK = 128
GMLP_GROUPS = 8
GMLP_GROUP_DIM = MIX_WIDTH // GMLP_GROUPS
HGRN_HEADS = 8
HGRN_HEAD_DIM = MIX_WIDTH // HGRN_HEADS
HGRN_CHUNK = 64
N_IN_SLICES = 8
N_GROUPS = 4
EXPERTS_PER_GROUP = 4
N_EXPERTS = N_GROUPS * EXPERTS_PER_GROUP
EXPERT_FF = D_MODEL // 2
TOP_K_IN_GROUP = 2
RMS_EPS = 1e-6
LN_EPS = 1e-5

kernel_name = "hybrid_gmlp_hgrn2_hmoe_block"


def _rmsnorm(x, g):
    xf = x.astype(jnp.float32)
    y = xf * lax.rsqrt(jnp.mean(xf * xf, axis=-1, keepdims=True) + RMS_EPS)
    return (y * g.astype(jnp.float32)).astype(x.dtype)


def _layernorm(x, g, b):
    xf = x.astype(jnp.float32)
    mu = jnp.mean(xf, axis=-1, keepdims=True)
    var = jnp.mean(jnp.square(xf - mu), axis=-1, keepdims=True)
    y = (xf - mu) * lax.rsqrt(var + LN_EPS)
    return (y * g.astype(jnp.float32) + b.astype(jnp.float32)).astype(x.dtype)


def _hgrn2_chunk_scan(q, k, v, log_f):
    b_, s_, h_, kd = q.shape
    vd = v.shape[-1]
    n = s_ // HGRN_CHUNK

    def to_chunks(t):
        return t.astype(jnp.float32).reshape(b_, n, HGRN_CHUNK, h_, t.shape[-1]).transpose(1, 0, 3, 2, 4)

    qc, kc, vc, fc = to_chunks(q), to_chunks(k), to_chunks(v), to_chunks(log_f)
    causal = jnp.tril(jnp.ones((HGRN_CHUNK, HGRN_CHUNK), dtype=bool))[:, :, None]

    def step(state, inp):
        q_, k_, v_, lf = inp
        bcum = jnp.cumsum(lf, axis=2)
        diff = bcum[:, :, :, None, :] - bcum[:, :, None, :, :]
        decay = jnp.exp(jnp.where(causal, diff, -jnp.inf))
        scores = jnp.einsum('bhtsk,bhsk->bhts', q_[:, :, :, None, :] * decay, k_)
        o = (jnp.einsum('bhts,bhsv->bhtv', scores, v_)
             + jnp.einsum('bhtk,bhkv->bhtv', q_ * jnp.exp(bcum), state))
        b_last = bcum[:, :, -1:, :]
        state = (jnp.exp(b_last[:, :, 0, :])[..., None] * state
                 + jnp.einsum('bhsk,bhsv->bhkv', k_ * jnp.exp(b_last - bcum), v_))
        return state, o

    s0 = jnp.zeros((b_, h_, kd, vd), jnp.float32)
    _, o = lax.scan(step, s0, (qc, kc, vc, fc))
    return o.transpose(1, 0, 3, 2, 4).reshape(b_, s_, h_, vd)


def _token_mixer(h, w_in, ln_v_g, ln_v_b, w_spatial, b_spatial, lb, g_hgrn, w_out):
    b_, s_, _ = h.shape
    proj = h @ w_in
    u, v, q, f_raw, i, og, gate_a, gate_b = jnp.split(proj, N_IN_SLICES, axis=-1)

    u = jax.nn.gelu(u)
    v = _layernorm(jax.nn.gelu(v), ln_v_g, ln_v_b)
    n = s_ // GMLP_CHUNK
    vc = v.reshape(b_, n, GMLP_CHUNK, GMLP_GROUPS, GMLP_GROUP_DIM)
    w_causal = w_spatial * jnp.tril(jnp.ones((GMLP_CHUNK, GMLP_CHUNK), w_spatial.dtype))
    zv = jnp.einsum('gts,bnsgc->bntgc', w_causal, vc) + b_spatial.T[:, :, None]
    y_a = u * zv.reshape(b_, s_, MIX_WIDTH)

    hs = (b_, s_, HGRN_HEADS, HGRN_HEAD_DIM)
    qh = jax.nn.silu(q).reshape(hs)
    f = lb + (1.0 - lb) * jax.nn.sigmoid(f_raw.astype(jnp.float32))
    log_f = jnp.log(f).reshape(hs)
    kh = (1.0 - f).reshape(hs)
    o = _hgrn2_chunk_scan(qh, kh, i.reshape(hs), log_f)
    o = _rmsnorm(o, g_hgrn.reshape(HGRN_HEADS, HGRN_HEAD_DIM))
    y_b = o.reshape(b_, s_, MIX_WIDTH).astype(h.dtype) * jax.nn.silu(og)

    y = jax.nn.sigmoid(gate_a) * y_a + jax.nn.sigmoid(gate_b) * y_b
    return y @ w_out


def _hier_moe(h, w_rg, b_rg, w_re, b_re, w1, w3, w2):
    b_, s_, _ = h.shape
    p_group = jax.nn.softmax((h @ w_rg + b_rg).astype(jnp.float32), axis=-1)
    p_top, g_idx = lax.top_k(p_group, 1)
    e_logits = (h @ w_re + b_re).astype(jnp.float32).reshape(b_, s_, N_GROUPS, EXPERTS_PER_GROUP)
    sel = jax.nn.one_hot(g_idx[..., 0], N_GROUPS, dtype=jnp.float32)
    e_logits = jnp.einsum('bsge,bsg->bse', e_logits, sel)
    p_exp = jax.nn.softmax(e_logits, axis=-1)
    top_p, top_i = lax.top_k(p_exp, TOP_K_IN_GROUP)
    w = p_top * top_p / jnp.sum(top_p, axis=-1, keepdims=True)
    expert_id = g_idx * EXPERTS_PER_GROUP + top_i
    combine = jnp.sum(w[..., None] * jax.nn.one_hot(expert_id, N_EXPERTS, dtype=jnp.float32), axis=-2)
    combine = combine.astype(h.dtype)
    y = jnp.zeros_like(h)
    for e in range(N_EXPERTS):
        a = jax.nn.silu(h @ w1[e]) * (h @ w3[e])
        y = y + combine[..., e:e + 1] * (a @ w2[e])
    return y


def setup_inputs(seed: int = 0) -> dict:
    key = jax.random.key(seed)
    ks = jax.random.split(key, 23)

    def nrm(k, shape, scale):
        return jax.random.normal(k, shape, jnp.float32) * scale

    d, wd, c_ = D_MODEL, MIX_WIDTH, GMLP_CHUNK
    return {
        "x": nrm(ks[0], (BATCH, SEQ, d), 1.0),
        "c": nrm(ks[1], (BATCH, d), 1.0),
        "w_ada": nrm(ks[2], (DEPTH, d, 6 * d), 0.5 * d ** -0.5),
        "b_ada": nrm(ks[3], (DEPTH, 6 * d), 0.01),
        "g_pre_mix": 1.0 + nrm(ks[4], (DEPTH, d), 0.05),
        "g_post_mix": 1.0 + nrm(ks[5], (DEPTH, d), 0.05),
        "w_in": nrm(ks[6], (DEPTH, d, N_IN_SLICES * wd), d ** -0.5),
        "ln_v_g": 1.0 + nrm(ks[7], (DEPTH, wd), 0.05),
        "ln_v_b": nrm(ks[8], (DEPTH, wd), 0.02),
        "w_spatial": nrm(ks[9], (DEPTH, GMLP_GROUPS, c_, c_), 0.5 * c_ ** -0.5),
        "b_spatial": 1.0 + nrm(ks[10], (DEPTH, GMLP_GROUPS, c_), 0.05),
        "lb_logits": nrm(ks[11], (DEPTH + 1, wd), 0.1),
        "g_hgrn_norm": 1.0 + nrm(ks[12], (DEPTH, wd), 0.05),
        "w_out": nrm(ks[13], (DEPTH, wd, d), wd ** -0.5),
        "g_pre_ffn": 1.0 + nrm(ks[14], (DEPTH, d), 0.05),
        "g_post_ffn": 1.0 + nrm(ks[15], (DEPTH, d), 0.05),
        "w_router_group": nrm(ks[16], (DEPTH, d, N_GROUPS), d ** -0.5),
        "b_router_group": nrm(ks[17], (DEPTH, N_GROUPS), 0.01),
        "w_router_expert": nrm(ks[18], (DEPTH, d, N_EXPERTS), d ** -0.5),
        "b_router_expert": nrm(ks[19], (DEPTH, N_EXPERTS), 0.01),
        "w1": nrm(ks[20], (DEPTH, N_EXPERTS, d, EXPERT_FF), d ** -0.5),
        "w3": nrm(ks[21], (DEPTH, N_EXPERTS, d, EXPERT_FF), d ** -0.5),
        "w2": nrm(ks[22], (DEPTH, N_EXPERTS, EXPERT_FF, d), EXPERT_FF ** -0.5),
    }


def reference(x, c, w_ada, b_ada, g_pre_mix, g_post_mix, w_in, ln_v_g, ln_v_b,
              w_spatial, b_spatial, lb_logits, g_hgrn_norm, w_out, g_pre_ffn,
              g_post_ffn, w_router_group, b_router_group, w_router_expert,
              b_router_expert, w1, w3, w2):
    lb_all = jnp.cumsum(jax.nn.softmax(lb_logits.astype(jnp.float32), axis=0), axis=0)
    for layer in range(DEPTH):
        ada = jax.nn.silu(c) @ w_ada[layer] + b_ada[layer]
        sh1, sc1, gt1, sh2, sc2, gt2 = jnp.split(ada[:, None, :], 6, axis=-1)

        h = _rmsnorm(x, g_pre_mix[layer]) * (1.0 + sc1) + sh1
        y = _token_mixer(h, w_in[layer], ln_v_g[layer], ln_v_b[layer], w_spatial[layer],
                         b_spatial[layer], lb_all[layer], g_hgrn_norm[layer], w_out[layer])
        x = x + gt1 * _rmsnorm(y, g_post_mix[layer])

        h = _rmsnorm(x, g_pre_ffn[layer]) * (1.0 + sc2) + sh2
        y = _hier_moe(h, w_router_group[layer], b_router_group[layer], w_router_expert[layer],
                      b_router_expert[layer], w1[layer], w3[layer], w2[layer])
        x = x + gt2 * _rmsnorm(y, g_post_ffn[layer])
    return x
```

```python
import functools

import jax
import jax.numpy as jnp
from jax import lax
from jax.experimental import pallas as pl
from jax.experimental.pallas import tpu as pltpu

RMS_EPS = 1e-6
LN_EPS = 1e-5
HGRN_HEAD_DIM = 128
HGRN_CHUNK = 64
N_IN_SLICES = 8
TOP_K_IN_GROUP = 2
LANES = 128
MIXER_SEQ_TILE = 256
MOE_TOKEN_TILE = 1024
VMEM_LIMIT_BYTES = 56 * 1024 * 1024

_F32 = jnp.float32
_BF16 = jnp.bfloat16


def _sigmoid(x):
    return 0.5 * (jnp.tanh(0.5 * x) + 1.0)


def _silu(x):
    return x * _sigmoid(x)


def _gelu_tanh(x):
    c = 0.7978845608028654
    return 0.5 * x * (1.0 + jnp.tanh(c * (x + 0.044715 * (x * x * x))))


def _rms(x):
    return x * lax.rsqrt(jnp.mean(x * x, axis=-1, keepdims=True) + RMS_EPS)


def _dot(a, b):
    return jnp.dot(a, b, preferred_element_type=_F32)


def _dot_nt(a, b):
    return lax.dot_general(a, b, (((1,), (1,)), ((), ())), preferred_element_type=_F32)


def _dot_tn(a, b):
    return lax.dot_general(a, b, (((0,), (0,)), ((), ())), preferred_element_type=_F32)


def _split3(x):
    hi = x.astype(_BF16)
    r = x - hi.astype(_F32)
    mid = r.astype(_BF16)
    lo = (r - mid.astype(_F32)).astype(_BF16)
    return hi, mid, lo


def _ada_kernel(c_ref, w_ref, b_ref, o_ref):
    s = _silu(c_ref[...])
    w = w_ref[...]
    acc = jnp.zeros(o_ref.shape, _F32)
    for sp in _split3(s):
        for wp in _split3(w)[:2]:
            acc = acc + _dot(sp, wp)
    o_ref[...] = acc + b_ref[...]


def _ada(c, w_ada, b_ada):
    b, d = c.shape
    n = w_ada.shape[1]
    tn = d
    return pl.pallas_call(
        _ada_kernel,
        grid=(n // tn,),
        in_specs=[
            pl.BlockSpec((b, d), lambda j: (0, 0)),
            pl.BlockSpec((d, tn), lambda j: (0, j)),
            pl.BlockSpec((1, tn), lambda j: (0, j)),
        ],
        out_specs=pl.BlockSpec((b, tn), lambda j: (0, j)),
        out_shape=jax.ShapeDtypeStruct((b, n), _F32),
        name="ada",
    )(c, w_ada, b_ada.reshape(1, n))


def _lb_kernel(l_ref, o_ref):
    l = l_ref[...]
    e = jnp.exp(l - jnp.max(l, axis=0, keepdims=True))
    p = e / jnp.sum(e, axis=0, keepdims=True)
    rows = [p[0:1]]
    for i in range(1, p.shape[0]):
        rows.append(rows[-1] + p[i:i + 1])
    for i, r in enumerate(rows):
        o_ref[i:i + 1, :] = r


def _lower_bounds(lb_logits):
    return pl.pallas_call(
        _lb_kernel,
        out_shape=jax.ShapeDtypeStruct(lb_logits.shape, _F32),
        name="lower_bounds",
    )(lb_logits.astype(_F32))


def _mixer_kernel(x_ref, ada_ref, gpre_ref, gpost_ref, win_ref, lng_ref, lnb_ref, wsp_ref,
                  bspt_ref, lb_ref, ghg_ref, wout_ref, o_ref, st_ref, ya_ref, ob_ref):
    ts, d = x_ref.shape[1], x_ref.shape[2]
    w = wout_ref.shape[0]
    n_groups, gchunk = wsp_ref.shape[0], wsp_ref.shape[1]
    gdim = w // n_groups
    n_heads = w // HGRN_HEAD_DIM
    c = HGRN_CHUNK

    @pl.when(pl.program_id(1) == 0)
    def _():
        st_ref[...] = jnp.zeros(st_ref.shape, _F32)

    x = x_ref[0]
    ada = ada_ref[0]
    sh1, sc1, gt1 = ada[0:1], ada[1:2], ada[2:3]
    h = (_rms(x) * gpre_ref[...]) * (1.0 + sc1) + sh1
    hb = h.astype(_BF16)

    def proj(j):
        return _dot(hb, win_ref[:, j * w:(j + 1) * w])

    u = _gelu_tanh(proj(0))
    v = _gelu_tanh(proj(1))
    mu = jnp.mean(v, axis=-1, keepdims=True)
    vc = v - mu
    var = jnp.mean(vc * vc, axis=-1, keepdims=True)
    vn = ((vc * lax.rsqrt(var + LN_EPS)) * lng_ref[...] + lnb_ref[...]).astype(_BF16)
    tri_g = (lax.broadcasted_iota(jnp.int32, (gchunk, gchunk), 0)
             >= lax.broadcasted_iota(jnp.int32, (gchunk, gchunk), 1))
    bspt = bspt_ref[...]
    for g in range(n_groups):
        wc = jnp.where(tri_g, wsp_ref[g], 0.0).astype(_BF16)
        cols = slice(g * gdim, (g + 1) * gdim)
        for n in range(ts // gchunk):
            rows = slice(n * gchunk, (n + 1) * gchunk)
            zv = _dot(wc, vn[rows, cols]) + bspt[:, g:g + 1]
            ya_ref[rows, cols] = u[rows, cols] * zv

    q = _silu(proj(2))
    lb = lb_ref[...]
    f = lb + (1.0 - lb) * _sigmoid(proj(3))
    lf = jnp.log(f)
    k = 1.0 - f
    iv = proj(4).astype(_BF16)
    r_i = lax.broadcasted_iota(jnp.int32, (ts, ts), 0)
    c_i = lax.broadcasted_iota(jnp.int32, (ts, ts), 1)
    ltri = jnp.where((r_i // c == c_i // c) & (c_i <= r_i), 1.0, 0.0).astype(_BF16)
    bcum = jnp.zeros((ts, w), _F32)
    for part in _split3(lf):
        bcum = bcum + _dot(ltri, part)
    tri_c = (lax.broadcasted_iota(jnp.int32, (c, c), 0)
             >= lax.broadcasted_iota(jnp.int32, (c, c), 1))
    for n in range(ts // c):
        rows = slice(n * c, (n + 1) * c)
        bc = bcum[rows]
        mid = bc[c // 2 - 1:c // 2]
        last = bc[c - 1:c]
        qx = q[rows] * jnp.exp(bc - mid)
        kx = k[rows] * jnp.exp(mid - bc)
        qd = (qx * jnp.exp(mid)).astype(_BF16)
        kd = (kx * jnp.exp(last - mid)).astype(_BF16)
        qx = qx.astype(_BF16)
        kx = kx.astype(_BF16)
        dec = jnp.exp(last)
        iv_c = iv[rows]
        for hd in range(n_heads):
            cols = slice(hd * HGRN_HEAD_DIM, (hd + 1) * HGRN_HEAD_DIM)
            st = st_ref[hd]
            a = jnp.where(tri_c, _dot_nt(qx[:, cols], kx[:, cols]), 0.0).astype(_BF16)
            o = _dot(a, iv_c[:, cols]) + _dot_nt(qd[:, cols], st.astype(_BF16))
            st_ref[hd] = st * dec[:, cols] + _dot_tn(iv_c[:, cols], kd[:, cols])
            ob_ref[rows, cols] = _rms(o)
    yb = (ob_ref[...] * ghg_ref[...]) * _silu(proj(5))

    y = _sigmoid(proj(6)) * ya_ref[...] + _sigmoid(proj(7)) * yb
    out = _dot(y.astype(_BF16), wout_ref[...])
    o_ref[0] = x + gt1 * (_rms(out) * gpost_ref[...])


def _mixer(x, ada, g_pre, g_post, w_in, ln_g, ln_b, w_spatial, b_spatial, lb, g_hgrn, w_out):
    b, s, d = x.shape
    w = w_out.shape[0]
    n_groups, gchunk, _ = w_spatial.shape
    ts = min(MIXER_SEQ_TILE, s)
    assert s % ts == 0 and ts % gchunk == 0 and ts % HGRN_CHUNK == 0
    assert w % HGRN_HEAD_DIM == 0 and w // n_groups == LANES and w_in.shape == (d, N_IN_SLICES * w)
    n_heads = w // HGRN_HEAD_DIM
    row = lambda a: a.reshape(1, -1).astype(_F32)
    const2 = lambda i, j: (0, 0)
    const3 = lambda i, j: (0, 0, 0)
    single = dict(pipeline_mode=pl.Buffered(1))
    return pl.pallas_call(
        _mixer_kernel,
        grid=(b, s // ts),
        in_specs=[
            pl.BlockSpec((1, ts, d), lambda i, j: (i, j, 0)),
            pl.BlockSpec((1, ada.shape[1], d), lambda i, j: (i, 0, 0)),
            pl.BlockSpec((1, d), const2),
            pl.BlockSpec((1, d), const2),
            pl.BlockSpec((d, N_IN_SLICES * w), const2, **single),
            pl.BlockSpec((1, w), const2),
            pl.BlockSpec((1, w), const2),
            pl.BlockSpec((n_groups, gchunk, gchunk), const3, **single),
            pl.BlockSpec((gchunk, n_groups), const2),
            pl.BlockSpec((1, w), const2),
            pl.BlockSpec((1, w), const2),
            pl.BlockSpec((w, d), const2, **single),
        ],
        out_specs=pl.BlockSpec((1, ts, d), lambda i, j: (i, j, 0)),
        out_shape=jax.ShapeDtypeStruct((b, s, d), _F32),
        scratch_shapes=[
            pltpu.VMEM((n_heads, HGRN_HEAD_DIM, HGRN_HEAD_DIM), _F32),
            pltpu.VMEM((ts, w), _F32),
            pltpu.VMEM((ts, w), _F32),
        ],
        compiler_params=pltpu.CompilerParams(
            dimension_semantics=("parallel", "arbitrary"),
            vmem_limit_bytes=VMEM_LIMIT_BYTES),
        name="mixer",
    )(x, ada, row(g_pre), row(g_post), w_in.astype(_BF16), row(ln_g), row(ln_b),
      w_spatial.astype(_F32), b_spatial.T.astype(_F32), row(lb), row(g_hgrn), w_out.astype(_BF16))


def _route(logits, n_groups, per_group):
    lane = lax.broadcasted_iota(jnp.int32, logits.shape, 1)
    neg = -jnp.inf
    lg = jnp.where(lane < n_groups, logits, neg)
    mg = jnp.max(lg, axis=-1, keepdims=True)
    p_top = 1.0 / jnp.sum(jnp.exp(lg - mg), axis=-1, keepdims=True)
    g_idx = jnp.min(jnp.where(lg == mg, lane, LANES), axis=-1, keepdims=True)
    first = n_groups + g_idx * per_group
    le = jnp.where((lane >= first) & (lane < first + per_group), logits, neg)
    m1 = jnp.max(le, axis=-1, keepdims=True)
    i1 = jnp.min(jnp.where(le == m1, lane, LANES), axis=-1, keepdims=True)
    le2 = jnp.where(lane == i1, neg, le)
    m2 = jnp.max(le2, axis=-1, keepdims=True)
    i2 = jnp.min(jnp.where(le2 == m2, lane, LANES), axis=-1, keepdims=True)
    r = jnp.exp(m2 - m1)
    w1 = p_top / (1.0 + r)
    w2 = w1 * r
    return jnp.where(lane == i1, w1, 0.0) + jnp.where(lane == i2, w2, 0.0)


def _moe_kernel(x_ref, ada_ref, gpre_ref, gpost_ref, wr_ref, br_ref, w1_ref, w3_ref, w2_ref,
                o_ref, h_ref, comb_ref, acc_ref, *, n_groups, per_group):
    e = pl.program_id(2)
    n_exp = pl.num_programs(2)

    @pl.when(e == 0)
    def _():
        ada = ada_ref[0]
        sh2, sc2 = ada[3:4], ada[4:5]
        h = (_rms(x_ref[0]) * gpre_ref[...]) * (1.0 + sc2) + sh2
        hb = h.astype(_BF16)
        h_ref[...] = hb
        logits = _dot(hb, wr_ref[...]) + br_ref[...]
        comb_ref[...] = _route(logits, n_groups, per_group)
        acc_ref[...] = jnp.zeros(acc_ref.shape, _F32)

    hb = h_ref[...]
    a = _silu(_dot(hb, w1_ref[0].astype(_BF16))) * _dot(hb, w3_ref[0].astype(_BF16))
    lane = lax.broadcasted_iota(jnp.int32, comb_ref.shape, 1)
    ce = jnp.sum(jnp.where(lane == n_groups + e, comb_ref[...], 0.0), axis=-1, keepdims=True)
    acc_ref[...] += ce * _dot(a.astype(_BF16), w2_ref[0].astype(_BF16))

    @pl.when(e == n_exp - 1)
    def _():
        gt2 = ada_ref[0][5:6]
        o_ref[0] = x_ref[0] + gt2 * (_rms(acc_ref[...]) * gpost_ref[...])


def _moe(x, ada, g_pre, g_post, w_rg, b_rg, w_re, b_re, w1, w3, w2):
    b, s, d = x.shape
    n_groups = w_rg.shape[1]
    n_exp = w_re.shape[1]
    ff = w1.shape[2]
    assert n_groups + n_exp <= LANES
    tm = min(MOE_TOKEN_TILE, s)
    assert s % tm == 0
    pad = LANES - n_groups - n_exp
    wr = jnp.concatenate([w_rg, w_re, jnp.zeros((d, pad), _F32)], axis=1).astype(_BF16)
    br = jnp.concatenate([b_rg, b_re, jnp.zeros((pad,), _F32)]).reshape(1, LANES).astype(_F32)
    row = lambda a: a.reshape(1, -1).astype(_F32)
    const2 = lambda i, j, e: (0, 0)
    kern = functools.partial(_moe_kernel, n_groups=n_groups, per_group=n_exp // n_groups)
    return pl.pallas_call(
        kern,
        grid=(b, s // tm, n_exp),
        in_specs=[
            pl.BlockSpec((1, tm, d), lambda i, j, e: (i, j, 0)),
            pl.BlockSpec((1, ada.shape[1], d), lambda i, j, e: (i, 0, 0)),
            pl.BlockSpec((1, d), const2),
            pl.BlockSpec((1, d), const2),
            pl.BlockSpec((d, LANES), const2),
            pl.BlockSpec((1, LANES), const2),
            pl.BlockSpec((1, d, ff), lambda i, j, e: (e, 0, 0)),
            pl.BlockSpec((1, d, ff), lambda i, j, e: (e, 0, 0)),
            pl.BlockSpec((1, ff, d), lambda i, j, e: (e, 0, 0)),
        ],
        out_specs=pl.BlockSpec((1, tm, d), lambda i, j, e: (i, j, 0)),
        out_shape=jax.ShapeDtypeStruct((b, s, d), _F32),
        scratch_shapes=[
            pltpu.VMEM((tm, d), _BF16),
            pltpu.VMEM((tm, LANES), _F32),
            pltpu.VMEM((tm, d), _F32),
        ],
        compiler_params=pltpu.CompilerParams(
            dimension_semantics=("parallel", "parallel", "arbitrary"),
            vmem_limit_bytes=VMEM_LIMIT_BYTES),
        name="moe",
    )(x, ada, row(g_pre), row(g_post), wr, br, w1, w3, w2)


def kernel(x, c, w_ada, b_ada, g_pre_mix, g_post_mix, w_in, ln_v_g, ln_v_b, w_spatial, b_spatial,
           lb_logits, g_hgrn_norm, w_out, g_pre_ffn, g_post_ffn, w_router_group, b_router_group,
           w_router_expert, b_router_expert, w1, w3, w2):
    depth = w_in.shape[0]
    b, s, d = x.shape
    lb_all = _lower_bounds(lb_logits)
    for layer in range(depth):
        ada = _ada(c, w_ada[layer], b_ada[layer]).reshape(b, 6, d)
        x = _mixer(x, ada, g_pre_mix[layer], g_post_mix[layer], w_in[layer], ln_v_g[layer],
                   ln_v_b[layer], w_spatial[layer], b_spatial[layer], lb_all[layer],
                   g_hgrn_norm[layer], w_out[layer])
        x = _moe(x, ada, g_pre_ffn[layer], g_post_ffn[layer], w_router_group[layer],
                 b_router_group[layer], w_router_expert[layer], b_router_expert[layer],
                 w1[layer], w3[layer], w2[layer])
    return x
```

```python
import functools

import jax
import jax.numpy as jnp
from jax import lax
from jax.experimental import pallas as pl
from jax.experimental.pallas import tpu as pltpu

RMS_EPS = 1e-6
LN_EPS = 1e-5
HGRN_HEAD_DIM = 128
HGRN_CHUNK = 64
N_IN_SLICES = 8
TOP_K_IN_GROUP = 2
LANES = 128
MIXER_SEQ_TILE = 256
MOE_TOKEN_TILE = 256
CHUNK_ROWS = 16
GMM_ROW_TILE = 512
VMEM_LIMIT_BYTES = 56 * 1024 * 1024

_F32 = jnp.float32
_BF16 = jnp.bfloat16


def _sigmoid(x):
    return 0.5 * (jnp.tanh(0.5 * x) + 1.0)


def _silu(x):
    return x * _sigmoid(x)


def _gelu_tanh(x):
    c = 0.7978845608028654
    return 0.5 * x * (1.0 + jnp.tanh(c * (x + 0.044715 * (x * x * x))))


def _rms(x):
    return x * lax.rsqrt(jnp.mean(x * x, axis=-1, keepdims=True) + RMS_EPS)


def _dot(a, b):
    return jnp.dot(a, b, preferred_element_type=_F32)


def _dot_nt(a, b):
    return lax.dot_general(a, b, (((1,), (1,)), ((), ())), preferred_element_type=_F32)


def _dot_tn(a, b):
    return lax.dot_general(a, b, (((0,), (0,)), ((), ())), preferred_element_type=_F32)


def _split3(x):
    hi = x.astype(_BF16)
    r = x - hi.astype(_F32)
    mid = r.astype(_BF16)
    lo = (r - mid.astype(_F32)).astype(_BF16)
    return hi, mid, lo


def _ada_kernel(c_ref, w_ref, b_ref, o_ref):
    s = _silu(c_ref[...])
    w = w_ref[...]
    acc = jnp.zeros(o_ref.shape, _F32)
    for sp in _split3(s):
        for wp in _split3(w)[:2]:
            acc = acc + _dot(sp, wp)
    o_ref[...] = acc + b_ref[...]


def _ada(c, w_ada, b_ada):
    b, d = c.shape
    n = w_ada.shape[1]
    tn = d
    return pl.pallas_call(
        _ada_kernel,
        grid=(n // tn,),
        in_specs=[
            pl.BlockSpec((b, d), lambda j: (0, 0)),
            pl.BlockSpec((d, tn), lambda j: (0, j)),
            pl.BlockSpec((1, tn), lambda j: (0, j)),
        ],
        out_specs=pl.BlockSpec((b, tn), lambda j: (0, j)),
        out_shape=jax.ShapeDtypeStruct((b, n), _F32),
        name="ada",
    )(c, w_ada, b_ada.reshape(1, n))


def _lb_kernel(l_ref, o_ref):
    l = l_ref[...]
    e = jnp.exp(l - jnp.max(l, axis=0, keepdims=True))
    p = e / jnp.sum(e, axis=0, keepdims=True)
    rows = [p[0:1]]
    for i in range(1, p.shape[0]):
        rows.append(rows[-1] + p[i:i + 1])
    for i, r in enumerate(rows):
        o_ref[i:i + 1, :] = r


def _lower_bounds(lb_logits):
    return pl.pallas_call(
        _lb_kernel,
        out_shape=jax.ShapeDtypeStruct(lb_logits.shape, _F32),
        name="lower_bounds",
    )(lb_logits.astype(_F32))


def _mixer_kernel(x_ref, ada_ref, gpre_ref, gpost_ref, win_ref, lng_ref, lnb_ref, wsp_ref,
                  bspt_ref, lb_ref, ghg_ref, wout_ref, o_ref, st_ref, ya_ref, ob_ref):
    ts, d = x_ref.shape[1], x_ref.shape[2]
    w = wout_ref.shape[0]
    n_groups, gchunk = wsp_ref.shape[0], wsp_ref.shape[1]
    gdim = w // n_groups
    n_heads = w // HGRN_HEAD_DIM
    c = HGRN_CHUNK

    @pl.when(pl.program_id(1) == 0)
    def _():
        st_ref[...] = jnp.zeros(st_ref.shape, _F32)

    x = x_ref[0]
    ada = ada_ref[0]
    sh1, sc1, gt1 = ada[0:1], ada[1:2], ada[2:3]
    h = (_rms(x) * gpre_ref[...]) * (1.0 + sc1) + sh1
    hb = h.astype(_BF16)

    def proj(j):
        return _dot(hb, win_ref[:, j * w:(j + 1) * w])

    u = _gelu_tanh(proj(0))
    v = _gelu_tanh(proj(1))
    mu = jnp.mean(v, axis=-1, keepdims=True)
    vc = v - mu
    var = jnp.mean(vc * vc, axis=-1, keepdims=True)
    vn = ((vc * lax.rsqrt(var + LN_EPS)) * lng_ref[...] + lnb_ref[...]).astype(_BF16)
    tri_g = (lax.broadcasted_iota(jnp.int32, (gchunk, gchunk), 0)
             >= lax.broadcasted_iota(jnp.int32, (gchunk, gchunk), 1))
    bspt = bspt_ref[...]
    for g in range(n_groups):
        wc = jnp.where(tri_g, wsp_ref[g], 0.0).astype(_BF16)
        cols = slice(g * gdim, (g + 1) * gdim)
        for n in range(ts // gchunk):
            rows = slice(n * gchunk, (n + 1) * gchunk)
            zv = _dot(wc, vn[rows, cols]) + bspt[:, g:g + 1]
            ya_ref[rows, cols] = u[rows, cols] * zv

    q = _silu(proj(2))
    lb = lb_ref[...]
    f = lb + (1.0 - lb) * _sigmoid(proj(3))
    lf = jnp.log(f)
    k = 1.0 - f
    iv = proj(4).astype(_BF16)
    r_i = lax.broadcasted_iota(jnp.int32, (ts, ts), 0)
    c_i = lax.broadcasted_iota(jnp.int32, (ts, ts), 1)
    ltri = jnp.where((r_i // c == c_i // c) & (c_i <= r_i), 1.0, 0.0).astype(_BF16)
    bcum = jnp.zeros((ts, w), _F32)
    for part in _split3(lf):
        bcum = bcum + _dot(ltri, part)
    tri_c = (lax.broadcasted_iota(jnp.int32, (c, c), 0)
             >= lax.broadcasted_iota(jnp.int32, (c, c), 1))
    for n in range(ts // c):
        rows = slice(n * c, (n + 1) * c)
        bc = bcum[rows]
        mid = bc[c // 2 - 1:c // 2]
        last = bc[c - 1:c]
        qx = q[rows] * jnp.exp(bc - mid)
        kx = k[rows] * jnp.exp(mid - bc)
        qd = (qx * jnp.exp(mid)).astype(_BF16)
        kd = (kx * jnp.exp(last - mid)).astype(_BF16)
        qx = qx.astype(_BF16)
        kx = kx.astype(_BF16)
        dec = jnp.exp(last)
        iv_c = iv[rows]
        for hd in range(n_heads):
            cols = slice(hd * HGRN_HEAD_DIM, (hd + 1) * HGRN_HEAD_DIM)
            st = st_ref[hd]
            a = jnp.where(tri_c, _dot_nt(qx[:, cols], kx[:, cols]), 0.0).astype(_BF16)
            o = _dot(a, iv_c[:, cols]) + _dot_nt(qd[:, cols], st.astype(_BF16))
            st_ref[hd] = st * dec[:, cols] + _dot_tn(iv_c[:, cols], kd[:, cols])
            ob_ref[rows, cols] = _rms(o)
    yb = (ob_ref[...] * ghg_ref[...]) * _silu(proj(5))

    y = _sigmoid(proj(6)) * ya_ref[...] + _sigmoid(proj(7)) * yb
    out = _dot(y.astype(_BF16), wout_ref[...])
    o_ref[0] = x + gt1 * (_rms(out) * gpost_ref[...])


def _mixer(x, ada, g_pre, g_post, w_in, ln_g, ln_b, w_spatial, b_spatial, lb, g_hgrn, w_out):
    b, s, d = x.shape
    w = w_out.shape[0]
    n_groups, gchunk, _ = w_spatial.shape
    ts = min(MIXER_SEQ_TILE, s)
    assert s % ts == 0 and ts % gchunk == 0 and ts % HGRN_CHUNK == 0
    assert w % HGRN_HEAD_DIM == 0 and w // n_groups == LANES and w_in.shape == (d, N_IN_SLICES * w)
    n_heads = w // HGRN_HEAD_DIM
    row = lambda a: a.reshape(1, -1).astype(_F32)
    const2 = lambda i, j: (0, 0)
    const3 = lambda i, j: (0, 0, 0)
    single = dict(pipeline_mode=pl.Buffered(1))
    return pl.pallas_call(
        _mixer_kernel,
        grid=(b, s // ts),
        in_specs=[
            pl.BlockSpec((1, ts, d), lambda i, j: (i, j, 0)),
            pl.BlockSpec((1, ada.shape[1], d), lambda i, j: (i, 0, 0)),
            pl.BlockSpec((1, d), const2),
            pl.BlockSpec((1, d), const2),
            pl.BlockSpec((d, N_IN_SLICES * w), const2, **single),
            pl.BlockSpec((1, w), const2),
            pl.BlockSpec((1, w), const2),
            pl.BlockSpec((n_groups, gchunk, gchunk), const3, **single),
            pl.BlockSpec((gchunk, n_groups), const2),
            pl.BlockSpec((1, w), const2),
            pl.BlockSpec((1, w), const2),
            pl.BlockSpec((w, d), const2, **single),
        ],
        out_specs=pl.BlockSpec((1, ts, d), lambda i, j: (i, j, 0)),
        out_shape=jax.ShapeDtypeStruct((b, s, d), _F32),
        scratch_shapes=[
            pltpu.VMEM((n_heads, HGRN_HEAD_DIM, HGRN_HEAD_DIM), _F32),
            pltpu.VMEM((ts, w), _F32),
            pltpu.VMEM((ts, w), _F32),
        ],
        compiler_params=pltpu.CompilerParams(
            dimension_semantics=("parallel", "arbitrary"),
            vmem_limit_bytes=VMEM_LIMIT_BYTES),
        name="mixer",
    )(x, ada, row(g_pre), row(g_post), w_in.astype(_BF16), row(ln_g), row(ln_b),
      w_spatial.astype(_F32), b_spatial.T.astype(_F32), row(lb), row(g_hgrn), w_out.astype(_BF16))


def _route(logits, n_groups, per_group):
    lane = lax.broadcasted_iota(jnp.int32, logits.shape, 1)
    neg = -jnp.inf
    lg = jnp.where(lane < n_groups, logits, neg)
    mg = jnp.max(lg, axis=-1, keepdims=True)
    p_top = 1.0 / jnp.sum(jnp.exp(lg - mg), axis=-1, keepdims=True)
    g_idx = jnp.min(jnp.where(lg == mg, lane, LANES), axis=-1, keepdims=True)
    first = n_groups + g_idx * per_group
    le = jnp.where((lane >= first) & (lane < first + per_group), logits, neg)
    m1 = jnp.max(le, axis=-1, keepdims=True)
    i1 = jnp.min(jnp.where(le == m1, lane, LANES), axis=-1, keepdims=True)
    le2 = jnp.where(lane == i1, neg, le)
    m2 = jnp.max(le2, axis=-1, keepdims=True)
    i2 = jnp.min(jnp.where(le2 == m2, lane, LANES), axis=-1, keepdims=True)
    r = jnp.exp(m2 - m1)
    w1 = p_top / (1.0 + r)
    w2 = w1 * r
    return i1, i2, w1, w2


def _router_kernel(x_ref, ada_ref, gpre_ref, wr_ref, br_ref, h_ref, meta_ref, metat_ref, cnt_ref,
                   *, n_groups, per_group):
    tm = x_ref.shape[1]
    ada = ada_ref[0]
    sh2, sc2 = ada[3:4], ada[4:5]
    h = (_rms(x_ref[0]) * gpre_ref[...]) * (1.0 + sc2) + sh2
    hb = h.astype(_BF16)
    h_ref[0] = hb
    logits = _dot(hb, wr_ref[...]) + br_ref[...]
    i1, i2, w1, w2 = _route(logits, n_groups, per_group)

    lane = lax.broadcasted_iota(jnp.int32, (tm, LANES), 1)
    ind = jnp.where((lane == i1) | (lane == i2), 1.0, 0.0)
    earlier = (lax.broadcasted_iota(jnp.int32, (tm, tm), 1)
               < lax.broadcasted_iota(jnp.int32, (tm, tm), 0))
    rank = _dot(jnp.where(earlier, 1.0, 0.0).astype(_BF16), ind.astype(_BF16))
    cnt = jnp.sum(ind, axis=0, keepdims=True)
    chunks = jnp.floor((cnt + (CHUNK_ROWS - 1)) * (1.0 / CHUNK_ROWS))
    before = (lax.broadcasted_iota(jnp.int32, (LANES, LANES), 0)
              < lax.broadcasted_iota(jnp.int32, (LANES, LANES), 1))
    first_chunk = _dot(jnp.broadcast_to(chunks, (8, LANES)).astype(_BF16),
                       jnp.where(before, 1.0, 0.0).astype(_BF16))[0:1]
    dest = first_chunk * CHUNK_ROWS + rank
    d1 = jnp.sum(jnp.where(lane == i1, dest, 0.0), axis=-1, keepdims=True)
    d2 = jnp.sum(jnp.where(lane == i2, dest, 0.0), axis=-1, keepdims=True)
    meta = jnp.where(lane == 0, d1, jnp.where(lane == 1, d2,
                     jnp.where(lane == 2, w1, jnp.where(lane == 3, w2, 0.0))))
    meta_ref[0] = meta
    metat_ref[0] = meta.T[0:8]
    cnt_ref[0] = jnp.broadcast_to(cnt, (8, LANES)).astype(jnp.int32)


def _chunk_loop(n, fn):
    def body(c, carry):
        fn(c)
        return carry
    lax.fori_loop(0, n, body, 0)


def _dispatch_kernel(nch_ref, cmap_ref, tbase_ref, ntail_ref, nu_ref, metat_ref, h_ref, xs_ref,
                     loc_ref, zero_ref, sem, zsem, *, max_chunks):
    i = pl.program_id(0)
    nt = pl.num_programs(0)
    slot = i % 2
    lrows, tm = loc_ref.shape[1], h_ref.shape[1]

    def chunk_copy(slot_, c, dst_chunk):
        return pltpu.make_async_copy(
            loc_ref.at[slot_, pl.ds(pl.multiple_of(c * CHUNK_ROWS, CHUNK_ROWS), CHUNK_ROWS)],
            xs_ref.at[pl.ds(pl.multiple_of(dst_chunk * CHUNK_ROWS, CHUNK_ROWS), CHUNK_ROWS)],
            sem.at[slot_])

    def zero_copy(dst_chunk):
        return pltpu.make_async_copy(
            zero_ref.at[pl.ds(0, CHUNK_ROWS)],
            xs_ref.at[pl.ds(pl.multiple_of(dst_chunk * CHUNK_ROWS, CHUNK_ROWS), CHUNK_ROWS)],
            zsem.at[0])

    def zero_tile_copy(tile):
        return pltpu.make_async_copy(
            zero_ref,
            xs_ref.at[pl.ds(pl.multiple_of(tile * GMM_ROW_TILE, GMM_ROW_TILE), GMM_ROW_TILE)],
            zsem.at[1])

    @pl.when(i == 0)
    def _():
        zero_ref[...] = jnp.zeros(zero_ref.shape, zero_ref.dtype)
        n_unused = xs_ref.shape[0] // GMM_ROW_TILE - nu_ref[0]
        for e in range(tbase_ref.shape[0]):
            _chunk_loop(ntail_ref[e], lambda j, e=e: zero_copy(tbase_ref[e] + j).start())
        _chunk_loop(n_unused, lambda j: zero_tile_copy(nu_ref[0] + j).start())
        for e in range(tbase_ref.shape[0]):
            _chunk_loop(ntail_ref[e], lambda j: zero_copy(0).wait())
        _chunk_loop(n_unused, lambda j: zero_tile_copy(0).wait())

    @pl.when(i >= 2)
    def _():
        _chunk_loop(nch_ref[i - 2], lambda c: chunk_copy(slot, 0, 0).wait())

    mt = metat_ref[0]
    d1 = mt[0:1].astype(jnp.int32)
    d2 = mt[1:2].astype(jnp.int32)
    r = lax.broadcasted_iota(jnp.int32, (lrows, tm), 0)
    sel = jnp.where((r == d1) | (r == d2), 1.0, 0.0).astype(_BF16)
    loc_ref[slot] = _dot(sel, h_ref[0]).astype(_BF16)
    n = nch_ref[i]
    _chunk_loop(n, lambda c: chunk_copy(slot, c, cmap_ref[i * max_chunks + c]).start())

    @pl.when(i == nt - 1)
    def _():
        _chunk_loop(n, lambda c: chunk_copy(slot, 0, 0).wait())

        @pl.when(i >= 1)
        def _():
            _chunk_loop(nch_ref[i - 1], lambda c: chunk_copy(1 - slot, 0, 0).wait())


def _experts_kernel(te_ref, tf_ref, nu_ref, xs_ref, w1_ref, w3_ref, w2_ref, ys_ref,
                    wb1_ref, wb3_ref, wb2_ref):
    i = pl.program_id(0)

    @pl.when(i < nu_ref[0])
    def _():
        @pl.when(tf_ref[i] == 1)
        def _():
            wb1_ref[...] = w1_ref[0].astype(_BF16)
            wb3_ref[...] = w3_ref[0].astype(_BF16)
            wb2_ref[...] = w2_ref[0].astype(_BF16)

        xs = xs_ref[...]
        a = _silu(_dot(xs, wb1_ref[...])) * _dot(xs, wb3_ref[...])
        ys_ref[...] = _dot(a.astype(_BF16), wb2_ref[...]).astype(_BF16)

    @pl.when(i >= nu_ref[0])
    def _():
        ys_ref[...] = jnp.zeros(ys_ref.shape, ys_ref.dtype)


def _combine_kernel(nch_ref, cmap_ref, x_ref, ada_ref, gpost_ref, meta_ref, ys_ref, o_ref,
                    loc_ref, sem, *, max_chunks):
    i = pl.program_id(0)
    nt = pl.num_programs(0)
    slot = i % 2
    lrows, tm = loc_ref.shape[1], x_ref.shape[1]

    def chunk_copy(tile, slot_, c):
        src_chunk = cmap_ref[tile * max_chunks + c]
        return pltpu.make_async_copy(
            ys_ref.at[pl.ds(pl.multiple_of(src_chunk * CHUNK_ROWS, CHUNK_ROWS), CHUNK_ROWS)],
            loc_ref.at[slot_, pl.ds(pl.multiple_of(c * CHUNK_ROWS, CHUNK_ROWS), CHUNK_ROWS)],
            sem.at[slot_])

    def fetch(tile, slot_):
        _chunk_loop(nch_ref[tile], lambda c: chunk_copy(tile, slot_, c).start())

    @pl.when(i == 0)
    def _():
        loc_ref[...] = jnp.zeros(loc_ref.shape, loc_ref.dtype)
        fetch(0, 0)

    @pl.when(i + 1 < nt)
    def _():
        fetch(i + 1, 1 - slot)

    _chunk_loop(nch_ref[i], lambda c: chunk_copy(i, slot, 0).wait())

    meta = meta_ref[0]
    d1 = meta[:, 0:1].astype(jnp.int32)
    d2 = meta[:, 1:2].astype(jnp.int32)
    w1, w2 = meta[:, 2:3], meta[:, 3:4]
    r = lax.broadcasted_iota(jnp.int32, (tm, lrows), 1)
    ys = loc_ref[slot]
    y = (w1 * _dot(jnp.where(r == d1, 1.0, 0.0).astype(_BF16), ys)
         + w2 * _dot(jnp.where(r == d2, 1.0, 0.0).astype(_BF16), ys))
    gt2 = ada_ref[0][5:6]
    o_ref[0] = x_ref[0] + gt2 * (_rms(y) * gpost_ref[...])


def _moe_plan(cnt, max_chunks, n_gmm_tiles):
    per_tile = GMM_ROW_TILE // CHUNK_ROWS
    pc = (cnt + (CHUNK_ROWS - 1)) // CHUNK_ROWS
    local_first = jnp.cumsum(pc, axis=1) - pc
    nch = jnp.sum(pc, axis=1)
    e_chunks = jnp.sum(pc, axis=0)
    e_region = ((e_chunks + per_tile - 1) // per_tile) * per_tile
    e_first = jnp.cumsum(e_region) - e_region
    seg_first = e_first[None, :] + jnp.cumsum(pc, axis=0) - pc
    c = jnp.arange(max_chunks, dtype=jnp.int32)[None, :, None]
    inside = (c >= local_first[:, None, :]) & (c < (local_first + pc)[:, None, :])
    cmap = jnp.sum(jnp.where(inside, seg_first[:, None, :] + c - local_first[:, None, :], 0), axis=-1)
    n_used = jnp.sum(e_region) // per_tile
    t = jnp.arange(n_gmm_tiles, dtype=jnp.int32)
    t_used = jnp.minimum(t, n_used - 1)
    tile_e = jnp.sum(t_used[:, None] * per_tile >= (e_first + e_region)[None, :], axis=1)
    tile_first = jnp.concatenate([jnp.ones((1,), jnp.int32),
                                  (tile_e[1:] != tile_e[:-1]).astype(jnp.int32)])
    i32 = lambda a: a.astype(jnp.int32)
    return dict(nch=i32(nch), cmap=i32(cmap.reshape(-1)), tail_first=i32(e_first + e_chunks),
                n_tail=i32(e_region - e_chunks), tile_e=i32(tile_e), tile_first=tile_first,
                n_used=i32(n_used.reshape(1)))


def _moe(x, ada, g_pre, g_post, w_rg, b_rg, w_re, b_re, w1, w3, w2):
    b, s, d = x.shape
    n_groups = w_rg.shape[1]
    n_exp = w_re.shape[1]
    ff = w1.shape[2]
    assert n_groups + n_exp <= LANES
    tm = min(MOE_TOKEN_TILE, s)
    assert s % tm == 0 and tm % LANES == 0 and GMM_ROW_TILE % CHUNK_ROWS == 0
    tiles_per_batch = s // tm
    nt = b * tiles_per_batch
    max_chunks = (TOP_K_IN_GROUP * tm + n_exp * (CHUNK_ROWS - 1)) // CHUNK_ROWS
    lrows = -(-max_chunks * CHUNK_ROWS // LANES) * LANES
    per_tile = GMM_ROW_TILE // CHUNK_ROWS
    n_gmm_tiles = -(-(nt * max_chunks + n_exp * (per_tile - 1)) // per_tile)
    n_rows = n_gmm_tiles * GMM_ROW_TILE

    pad = LANES - n_groups - n_exp
    wr = jnp.concatenate([w_rg, w_re, jnp.zeros((d, pad), _F32)], axis=1).astype(_BF16)
    br = jnp.concatenate([b_rg, b_re, jnp.zeros((pad,), _F32)]).reshape(1, LANES).astype(_F32)
    row = lambda a: a.reshape(1, -1).astype(_F32)
    n_ada = ada.shape[1]
    xt = x.reshape(nt, tm, d)

    h2, meta, metat, cnt = pl.pallas_call(
        functools.partial(_router_kernel, n_groups=n_groups, per_group=n_exp // n_groups),
        grid=(nt,),
        in_specs=[
            pl.BlockSpec((1, tm, d), lambda i: (i, 0, 0)),
            pl.BlockSpec((1, n_ada, d), lambda i: (i // tiles_per_batch, 0, 0)),
            pl.BlockSpec((1, d), lambda i: (0, 0)),
            pl.BlockSpec((d, LANES), lambda i: (0, 0)),
            pl.BlockSpec((1, LANES), lambda i: (0, 0)),
        ],
        out_specs=[
            pl.BlockSpec((1, tm, d), lambda i: (i, 0, 0)),
            pl.BlockSpec((1, tm, LANES), lambda i: (i, 0, 0)),
            pl.BlockSpec((1, 8, tm), lambda i: (i, 0, 0)),
            pl.BlockSpec((1, 8, LANES), lambda i: (i, 0, 0)),
        ],
        out_shape=[
            jax.ShapeDtypeStruct((nt, tm, d), _BF16),
            jax.ShapeDtypeStruct((nt, tm, LANES), _F32),
            jax.ShapeDtypeStruct((nt, 8, tm), _F32),
            jax.ShapeDtypeStruct((nt, 8, LANES), jnp.int32),
        ],
        compiler_params=pltpu.CompilerParams(dimension_semantics=("parallel",)),
        name="router",
    )(xt, ada, row(g_pre), wr, br)

    plan = _moe_plan(cnt[:, 0, n_groups:n_groups + n_exp], max_chunks, n_gmm_tiles)

    xs = pl.pallas_call(
        functools.partial(_dispatch_kernel, max_chunks=max_chunks),
        grid_spec=pltpu.PrefetchScalarGridSpec(
            num_scalar_prefetch=5,
            grid=(nt,),
            in_specs=[
                pl.BlockSpec((1, 8, tm), lambda i, *_: (i, 0, 0)),
                pl.BlockSpec((1, tm, d), lambda i, *_: (i, 0, 0)),
            ],
            out_specs=pl.BlockSpec(memory_space=pl.ANY),
            scratch_shapes=[
                pltpu.VMEM((2, lrows, d), _BF16),
                pltpu.VMEM((GMM_ROW_TILE, d), _BF16),
                pltpu.SemaphoreType.DMA((2,)),
                pltpu.SemaphoreType.DMA((2,)),
            ],
        ),
        out_shape=jax.ShapeDtypeStruct((n_rows, d), _BF16),
        compiler_params=pltpu.CompilerParams(dimension_semantics=("arbitrary",)),
        name="dispatch",
    )(plan["nch"], plan["cmap"], plan["tail_first"], plan["n_tail"], plan["n_used"], metat, h2)

    ys = pl.pallas_call(
        _experts_kernel,
        grid_spec=pltpu.PrefetchScalarGridSpec(
            num_scalar_prefetch=3,
            grid=(n_gmm_tiles,),
            in_specs=[
                pl.BlockSpec((GMM_ROW_TILE, d), lambda i, te, tf, nu: (jnp.minimum(i, nu[0] - 1), 0)),
                pl.BlockSpec((1, d, ff), lambda i, te, tf, nu: (te[i], 0, 0)),
                pl.BlockSpec((1, d, ff), lambda i, te, tf, nu: (te[i], 0, 0)),
                pl.BlockSpec((1, ff, d), lambda i, te, tf, nu: (te[i], 0, 0)),
            ],
            out_specs=pl.BlockSpec((GMM_ROW_TILE, d), lambda i, te, tf, nu: (i, 0)),
            scratch_shapes=[
                pltpu.VMEM((d, ff), _BF16),
                pltpu.VMEM((d, ff), _BF16),
                pltpu.VMEM((ff, d), _BF16),
            ],
        ),
        out_shape=jax.ShapeDtypeStruct((n_rows, d), _BF16),
        compiler_params=pltpu.CompilerParams(dimension_semantics=("arbitrary",),
                                             vmem_limit_bytes=VMEM_LIMIT_BYTES),
        name="experts",
    )(plan["tile_e"], plan["tile_first"], plan["n_used"], xs, w1, w3, w2)

    out = pl.pallas_call(
        functools.partial(_combine_kernel, max_chunks=max_chunks),
        grid_spec=pltpu.PrefetchScalarGridSpec(
            num_scalar_prefetch=2,
            grid=(nt,),
            in_specs=[
                pl.BlockSpec((1, tm, d), lambda i, *_: (i, 0, 0)),
                pl.BlockSpec((1, n_ada, d), lambda i, *_: (i // tiles_per_batch, 0, 0)),
                pl.BlockSpec((1, d), lambda i, *_: (0, 0)),
                pl.BlockSpec((1, tm, LANES), lambda i, *_: (i, 0, 0)),
                pl.BlockSpec(memory_space=pl.ANY),
            ],
            out_specs=pl.BlockSpec((1, tm, d), lambda i, *_: (i, 0, 0)),
            scratch_shapes=[
                pltpu.VMEM((2, lrows, d), _BF16),
                pltpu.SemaphoreType.DMA((2,)),
            ],
        ),
        out_shape=jax.ShapeDtypeStruct((nt, tm, d), _F32),
        compiler_params=pltpu.CompilerParams(dimension_semantics=("arbitrary",)),
        name="combine",
    )(plan["nch"], plan["cmap"], xt, ada, row(g_post), meta, ys)
    return out.reshape(b, s, d)


def kernel(x, c, w_ada, b_ada, g_pre_mix, g_post_mix, w_in, ln_v_g, ln_v_b, w_spatial, b_spatial,
           lb_logits, g_hgrn_norm, w_out, g_pre_ffn, g_post_ffn, w_router_group, b_router_group,
           w_router_expert, b_router_expert, w1, w3, w2):
    depth = w_in.shape[0]
    b, s, d = x.shape
    lb_all = _lower_bounds(lb_logits)
    for layer in range(depth):
        ada = _ada(c, w_ada[layer], b_ada[layer]).reshape(b, 6, d)
        x = _mixer(x, ada, g_pre_mix[layer], g_post_mix[layer], w_in[layer], ln_v_g[layer],
                   ln_v_b[layer], w_spatial[layer], b_spatial[layer], lb_all[layer],
                   g_hgrn_norm[layer], w_out[layer])
        x = _moe(x, ada, g_pre_ffn[layer], g_post_ffn[layer], w_router_group[layer],
                 b_router_group[layer], w_router_expert[layer], b_router_expert[layer],
                 w1[layer], w3[layer], w2[layer])
    return x
```

```python
import functools

import jax
import jax.numpy as jnp
from jax import lax
from jax.experimental import pallas as pl
from jax.experimental.pallas import tpu as pltpu

RMS_EPS = 1e-6
LN_EPS = 1e-5
HGRN_HEAD_DIM = 128
HGRN_CHUNK = 128
N_IN_SLICES = 8
TOP_K_IN_GROUP = 2
LANES = 128
MIXER_SEQ_TILE = 256
CHUNK_ROWS = 16
GMM_ROW_TILE = 512
VMEM_LIMIT_BYTES = 56 * 1024 * 1024

_F32 = jnp.float32
_BF16 = jnp.bfloat16


def _sigmoid(x):
    return 0.5 * (jnp.tanh(0.5 * x) + 1.0)


def _silu(x):
    return x * _sigmoid(x)


def _gelu_tanh(x):
    c = 0.7978845608028654
    return 0.5 * x * (1.0 + jnp.tanh(c * (x + 0.044715 * (x * x * x))))


def _rms(x):
    return x * lax.rsqrt(jnp.mean(x * x, axis=-1, keepdims=True) + RMS_EPS)


def _dot(a, b):
    return jnp.dot(a, b, preferred_element_type=_F32)


def _dot_nt(a, b):
    return lax.dot_general(a, b, (((1,), (1,)), ((), ())), preferred_element_type=_F32)


def _dot_tn(a, b):
    return lax.dot_general(a, b, (((0,), (0,)), ((), ())), preferred_element_type=_F32)


def _split3(x):
    hi = x.astype(_BF16)
    r = x - hi.astype(_F32)
    mid = r.astype(_BF16)
    lo = (r - mid.astype(_F32)).astype(_BF16)
    return hi, mid, lo


def _ada_kernel(c_ref, w_ref, b_ref, o_ref):
    s = _silu(c_ref[...])
    w = w_ref[...]
    acc = jnp.zeros(o_ref.shape, _F32)
    for sp in _split3(s):
        for wp in _split3(w)[:2]:
            acc = acc + _dot(sp, wp)
    o_ref[...] = acc + b_ref[...]


def _ada(c, w_ada, b_ada):
    b, d = c.shape
    n = w_ada.shape[1]
    tn = d
    return pl.pallas_call(
        _ada_kernel,
        grid=(n // tn,),
        in_specs=[
            pl.BlockSpec((b, d), lambda j: (0, 0)),
            pl.BlockSpec((d, tn), lambda j: (0, j)),
            pl.BlockSpec((1, tn), lambda j: (0, j)),
        ],
        out_specs=pl.BlockSpec((b, tn), lambda j: (0, j)),
        out_shape=jax.ShapeDtypeStruct((b, n), _F32),
        name="ada",
    )(c, w_ada, b_ada.reshape(1, n))


def _lb_kernel(l_ref, o_ref):
    l = l_ref[...]
    e = jnp.exp(l - jnp.max(l, axis=0, keepdims=True))
    p = e / jnp.sum(e, axis=0, keepdims=True)
    rows = [p[0:1]]
    for i in range(1, p.shape[0]):
        rows.append(rows[-1] + p[i:i + 1])
    for i, r in enumerate(rows):
        o_ref[i:i + 1, :] = r


def _lower_bounds(lb_logits):
    return pl.pallas_call(
        _lb_kernel,
        out_shape=jax.ShapeDtypeStruct(lb_logits.shape, _F32),
        name="lower_bounds",
    )(lb_logits.astype(_F32))


def _mixer_kernel(x_ref, ada_ref, gpre_ref, gpost_ref, win_ref, lng_ref, lnb_ref, wsp_ref,
                  bspt_ref, lb_ref, ghg_ref, wout_ref, gpre2_ref, wr_ref, br_ref,
                  o_ref, h2_ref, meta_ref, metat_ref, cnt_ref, st_ref, ya_ref, ob_ref,
                  *, n_route_groups, per_group):
    ts, d = x_ref.shape[1], x_ref.shape[2]
    w = wout_ref.shape[0]
    n_groups, gchunk = wsp_ref.shape[0], wsp_ref.shape[1]
    gdim = w // n_groups
    n_heads = w // HGRN_HEAD_DIM
    c = HGRN_CHUNK

    @pl.when(pl.program_id(1) == 0)
    def _():
        st_ref[...] = jnp.zeros(st_ref.shape, _F32)

    x = x_ref[0]
    ada = ada_ref[0]
    sh1, sc1, gt1 = ada[0:1], ada[1:2], ada[2:3]
    h = (_rms(x) * gpre_ref[...]) * (1.0 + sc1) + sh1
    h_bf = h.astype(_BF16)

    def proj(j):
        return _dot(h_bf, win_ref[:, j * w:(j + 1) * w])

    u = _gelu_tanh(proj(0))
    v = _gelu_tanh(proj(1))
    mu = jnp.mean(v, axis=-1, keepdims=True)
    vc = v - mu
    var = jnp.mean(vc * vc, axis=-1, keepdims=True)
    vn = ((vc * lax.rsqrt(var + LN_EPS)) * lng_ref[...] + lnb_ref[...]).astype(_BF16)
    tri_g = (lax.broadcasted_iota(jnp.int32, (gchunk, gchunk), 0)
             >= lax.broadcasted_iota(jnp.int32, (gchunk, gchunk), 1))
    bspt = bspt_ref[...]
    n_gchunks = ts // gchunk
    for g in range(n_groups):
        wc = jnp.where(tri_g, wsp_ref[g], 0.0).astype(_BF16)
        cols = slice(g * gdim, (g + 1) * gdim)
        vg = jnp.concatenate([vn[n * gchunk:(n + 1) * gchunk, cols] for n in range(n_gchunks)], axis=1)
        zv = _dot(wc, vg)
        for n in range(n_gchunks):
            rows = slice(n * gchunk, (n + 1) * gchunk)
            ya_ref[rows, cols] = u[rows, cols] * (zv[:, n * gdim:(n + 1) * gdim] + bspt[:, g:g + 1])

    q = _silu(proj(2))
    lb = lb_ref[...]
    f = lb + (1.0 - lb) * _sigmoid(proj(3))
    lf = jnp.log(f)
    k = 1.0 - f
    iv = proj(4).astype(_BF16)
    r_i = lax.broadcasted_iota(jnp.int32, (ts, ts), 0)
    c_i = lax.broadcasted_iota(jnp.int32, (ts, ts), 1)
    ltri = jnp.where((r_i // c == c_i // c) & (c_i <= r_i), 1.0, 0.0).astype(_BF16)
    bcum = jnp.zeros((ts, w), _F32)
    for part in _split3(lf):
        bcum = bcum + _dot(ltri, part)
    hb = c // 2
    row_h = lax.broadcasted_iota(jnp.int32, (hb, c), 0)
    lane_h = lax.broadcasted_iota(jnp.int32, (hb, c), 1)
    top_mask = lane_h <= row_h
    left = lane_h < hb
    bot_mask = lane_h - hb <= row_h
    pair = 2 * HGRN_HEAD_DIM
    pr = lax.broadcasted_iota(jnp.int32, (pair, pair), 0)
    pc_i = lax.broadcasted_iota(jnp.int32, (pair, pair), 1)
    same_head = (pr < HGRN_HEAD_DIM) == (pc_i < HGRN_HEAD_DIM)
    up_rows = lax.broadcasted_iota(jnp.int32, (c, pair), 1) < HGRN_HEAD_DIM
    for n in range(ts // c):
        rows = slice(n * c, (n + 1) * c)
        bc = bcum[rows]
        b_a, b_m, b_b, b_l = (bc[hb // 2 - 1:hb // 2], bc[hb - 1:hb],
                              bc[hb + hb // 2 - 1:hb + hb // 2], bc[c - 1:c])
        stack = lambda ra, rb: jnp.concatenate(
            [jnp.broadcast_to(ra, (hb, w)), jnp.broadcast_to(rb, (hb, w))], axis=0)
        ref = stack(b_a, b_b)
        qx = q[rows] * jnp.exp(bc - ref)
        kx = k[rows] * jnp.exp(ref - bc)
        qd = (qx * stack(jnp.exp(b_a), jnp.exp(b_b))).astype(_BF16)
        kd = (kx * stack(jnp.exp(b_l - b_a), jnp.exp(b_l - b_b))).astype(_BF16)
        q_off = qx[hb:] * jnp.exp(b_b - b_m)
        k_off = kx[:hb] * jnp.exp(b_m - b_a)
        q3 = jnp.concatenate([qx, q_off], axis=0).astype(_BF16)
        k3 = jnp.concatenate([kx, k_off, jnp.zeros((hb, w), _F32)], axis=0).astype(_BF16)
        dec = jnp.exp(b_l)
        iv_c = iv[rows]
        for p in range(n_heads // 2):
            pcols = slice(p * pair, (p + 1) * pair)
            a_heads = []
            for hd in (2 * p, 2 * p + 1):
                cols = slice(hd * HGRN_HEAD_DIM, (hd + 1) * HGRN_HEAD_DIM)
                sc = _dot_nt(q3[:, cols], k3[:, cols])
                top = jnp.where(top_mask, sc[0:hb, 0:c], 0.0)
                bot = jnp.where(left, sc[2 * hb:3 * hb, c:2 * c],
                                jnp.where(bot_mask, sc[hb:2 * hb, 0:c], 0.0))
                a_heads.append(jnp.concatenate([top, bot], axis=0))
            a_pair = jnp.concatenate(a_heads, axis=1).astype(_BF16)
            iv_p = iv_c[:, pcols]
            iv_blk = jnp.concatenate([jnp.where(up_rows, iv_p, 0.0).astype(_BF16),
                                      jnp.where(up_rows, 0.0, iv_p).astype(_BF16)], axis=0)
            st = st_ref[p]
            o = _dot(a_pair, iv_blk) + _dot_nt(qd[:, pcols], st.astype(_BF16))
            upd = _dot_tn(iv_p, kd[:, pcols])
            st_ref[p] = st * dec[:, pcols] + jnp.where(same_head, upd, 0.0)
            for j in range(2):
                hcols = slice(j * HGRN_HEAD_DIM, (j + 1) * HGRN_HEAD_DIM)
                ob_ref[rows, p * pair + j * HGRN_HEAD_DIM:p * pair + (j + 1) * HGRN_HEAD_DIM] = _rms(o[:, hcols])
    yb = (ob_ref[...] * ghg_ref[...]) * _silu(proj(5))

    y = _sigmoid(proj(6)) * ya_ref[...] + _sigmoid(proj(7)) * yb
    out = _dot(y.astype(_BF16), wout_ref[...])
    x1 = x + gt1 * (_rms(out) * gpost_ref[...])
    o_ref[0] = x1

    _route_tile(x1, ada[3:4], ada[4:5], gpre2_ref[...], wr_ref[...], br_ref[...],
                h2_ref, meta_ref, metat_ref, cnt_ref, n_route_groups, per_group)


def _mixer(x, ada, g_pre, g_post, w_in, ln_g, ln_b, w_spatial, b_spatial, lb, g_hgrn, w_out,
           g_pre_ffn, w_rg, b_rg, w_re, b_re):
    b, s, d = x.shape
    w = w_out.shape[0]
    n_groups, gchunk, _ = w_spatial.shape
    ts = min(MIXER_SEQ_TILE, s)
    assert s % ts == 0 and ts % gchunk == 0 and ts % HGRN_CHUNK == 0 and ts % LANES == 0
    n_rg, n_exp = w_rg.shape[1], w_re.shape[1]
    assert n_rg + n_exp <= LANES
    pad = LANES - n_rg - n_exp
    wr = jnp.concatenate([w_rg, w_re, jnp.zeros((d, pad), _F32)], axis=1).astype(_BF16)
    br = jnp.concatenate([b_rg, b_re, jnp.zeros((pad,), _F32)]).reshape(1, LANES).astype(_F32)
    tiles = s // ts
    nt = b * tiles
    tile3 = lambda i, j: (i * tiles + j, 0, 0)
    assert w % HGRN_HEAD_DIM == 0 and w // n_groups == LANES and w_in.shape == (d, N_IN_SLICES * w)
    n_heads = w // HGRN_HEAD_DIM
    assert n_heads % 2 == 0 and HGRN_HEAD_DIM == LANES
    row = lambda a: a.reshape(1, -1).astype(_F32)
    const2 = lambda i, j: (0, 0)
    const3 = lambda i, j: (0, 0, 0)
    single = dict(pipeline_mode=pl.Buffered(1))
    return pl.pallas_call(
        functools.partial(_mixer_kernel, n_route_groups=n_rg, per_group=n_exp // n_rg),
        grid=(b, s // ts),
        in_specs=[
            pl.BlockSpec((1, ts, d), lambda i, j: (i, j, 0)),
            pl.BlockSpec((1, ada.shape[1], d), lambda i, j: (i, 0, 0)),
            pl.BlockSpec((1, d), const2),
            pl.BlockSpec((1, d), const2),
            pl.BlockSpec((d, N_IN_SLICES * w), const2, **single),
            pl.BlockSpec((1, w), const2),
            pl.BlockSpec((1, w), const2),
            pl.BlockSpec((n_groups, gchunk, gchunk), const3, **single),
            pl.BlockSpec((gchunk, n_groups), const2),
            pl.BlockSpec((1, w), const2),
            pl.BlockSpec((1, w), const2),
            pl.BlockSpec((w, d), const2, **single),
            pl.BlockSpec((1, d), const2),
            pl.BlockSpec((d, LANES), const2),
            pl.BlockSpec((1, LANES), const2),
        ],
        out_specs=[
            pl.BlockSpec((1, ts, d), lambda i, j: (i, j, 0)),
            pl.BlockSpec((1, ts, d), tile3),
            pl.BlockSpec((1, ts, LANES), tile3),
            pl.BlockSpec((1, 8, ts), tile3),
            pl.BlockSpec((1, 8, LANES), tile3),
        ],
        out_shape=[
            jax.ShapeDtypeStruct((b, s, d), _F32),
            jax.ShapeDtypeStruct((nt, ts, d), _BF16),
            jax.ShapeDtypeStruct((nt, ts, LANES), _F32),
            jax.ShapeDtypeStruct((nt, 8, ts), _F32),
            jax.ShapeDtypeStruct((nt, 8, LANES), jnp.int32),
        ],
        scratch_shapes=[
            pltpu.VMEM((n_heads // 2, 2 * HGRN_HEAD_DIM, 2 * HGRN_HEAD_DIM), _F32),
            pltpu.VMEM((ts, w), _F32),
            pltpu.VMEM((ts, w), _F32),
        ],
        compiler_params=pltpu.CompilerParams(
            dimension_semantics=("parallel", "arbitrary"),
            vmem_limit_bytes=VMEM_LIMIT_BYTES),
        name="mixer",
    )(x, ada, row(g_pre), row(g_post), w_in.astype(_BF16), row(ln_g), row(ln_b),
      w_spatial.astype(_F32), b_spatial.T.astype(_F32), row(lb), row(g_hgrn), w_out.astype(_BF16),
      row(g_pre_ffn), wr, br)


def _route(logits, n_groups, per_group):
    lane = lax.broadcasted_iota(jnp.int32, logits.shape, 1)
    neg = -jnp.inf
    lg = jnp.where(lane < n_groups, logits, neg)
    mg = jnp.max(lg, axis=-1, keepdims=True)
    p_top = 1.0 / jnp.sum(jnp.exp(lg - mg), axis=-1, keepdims=True)
    g_idx = jnp.min(jnp.where(lg == mg, lane, LANES), axis=-1, keepdims=True)
    first = n_groups + g_idx * per_group
    le = jnp.where((lane >= first) & (lane < first + per_group), logits, neg)
    m1 = jnp.max(le, axis=-1, keepdims=True)
    i1 = jnp.min(jnp.where(le == m1, lane, LANES), axis=-1, keepdims=True)
    le2 = jnp.where(lane == i1, neg, le)
    m2 = jnp.max(le2, axis=-1, keepdims=True)
    i2 = jnp.min(jnp.where(le2 == m2, lane, LANES), axis=-1, keepdims=True)
    r = jnp.exp(m2 - m1)
    w1 = p_top / (1.0 + r)
    w2 = w1 * r
    return i1, i2, w1, w2


def _route_tile(x, sh2, sc2, gpre, wr, br, h_ref, meta_ref, metat_ref, cnt_ref, n_groups, per_group):
    tm = x.shape[0]
    h = (_rms(x) * gpre) * (1.0 + sc2) + sh2
    hb = h.astype(_BF16)
    h_ref[0] = hb
    logits = _dot(hb, wr) + br
    i1, i2, w1, w2 = _route(logits, n_groups, per_group)

    lane = lax.broadcasted_iota(jnp.int32, (tm, LANES), 1)
    ind = jnp.where((lane == i1) | (lane == i2), 1.0, 0.0)
    earlier = (lax.broadcasted_iota(jnp.int32, (tm, tm), 1)
               < lax.broadcasted_iota(jnp.int32, (tm, tm), 0))
    rank = _dot(jnp.where(earlier, 1.0, 0.0).astype(_BF16), ind.astype(_BF16))
    cnt = jnp.sum(ind, axis=0, keepdims=True)
    chunks = jnp.floor((cnt + (CHUNK_ROWS - 1)) * (1.0 / CHUNK_ROWS))
    before = (lax.broadcasted_iota(jnp.int32, (LANES, LANES), 0)
              < lax.broadcasted_iota(jnp.int32, (LANES, LANES), 1))
    first_chunk = _dot(jnp.broadcast_to(chunks, (8, LANES)).astype(_BF16),
                       jnp.where(before, 1.0, 0.0).astype(_BF16))[0:1]
    dest = first_chunk * CHUNK_ROWS + rank
    d1 = jnp.sum(jnp.where(lane == i1, dest, 0.0), axis=-1, keepdims=True)
    d2 = jnp.sum(jnp.where(lane == i2, dest, 0.0), axis=-1, keepdims=True)
    meta = jnp.where(lane == 0, d1, jnp.where(lane == 1, d2,
                     jnp.where(lane == 2, w1, jnp.where(lane == 3, w2, 0.0))))
    meta_ref[0] = meta
    metat_ref[0] = meta.T[0:8]
    cnt_ref[0] = jnp.broadcast_to(cnt, (8, LANES)).astype(jnp.int32)


def _chunk_loop(n, fn):
    def body(c, carry):
        fn(c)
        return carry
    lax.fori_loop(0, n, body, 0)


def _wait_chunks(n, max_chunks, copy_of_chunks):
    bit = 1
    while bit * 2 <= max_chunks:
        bit *= 2
    while bit >= 1:
        @pl.when((n & bit) != 0)
        def _(bit=bit):
            copy_of_chunks(bit).wait()
        bit //= 2


def _dispatch_kernel(nch_ref, cmap_ref, tbase_ref, ntail_ref, nu_ref, metat_ref, h_ref, xs_ref,
                     loc_ref, zero_ref, sem, zsem, *, max_chunks):
    i = pl.program_id(0)
    nt = pl.num_programs(0)
    slot = i % 2
    lrows, tm = loc_ref.shape[1], h_ref.shape[1]

    def chunk_copy(slot_, c, dst_chunk):
        return pltpu.make_async_copy(
            loc_ref.at[slot_, pl.ds(pl.multiple_of(c * CHUNK_ROWS, CHUNK_ROWS), CHUNK_ROWS)],
            xs_ref.at[pl.ds(pl.multiple_of(dst_chunk * CHUNK_ROWS, CHUNK_ROWS), CHUNK_ROWS)],
            sem.at[slot_])

    def zero_copy(dst_chunk):
        return pltpu.make_async_copy(
            zero_ref.at[pl.ds(0, CHUNK_ROWS)],
            xs_ref.at[pl.ds(pl.multiple_of(dst_chunk * CHUNK_ROWS, CHUNK_ROWS), CHUNK_ROWS)],
            zsem.at[0])

    def zero_tile_copy(tile):
        return pltpu.make_async_copy(
            zero_ref,
            xs_ref.at[pl.ds(pl.multiple_of(tile * GMM_ROW_TILE, GMM_ROW_TILE), GMM_ROW_TILE)],
            zsem.at[1])

    @pl.when(i == 0)
    def _():
        zero_ref[...] = jnp.zeros(zero_ref.shape, zero_ref.dtype)
        n_unused = xs_ref.shape[0] // GMM_ROW_TILE - nu_ref[0]
        for e in range(tbase_ref.shape[0]):
            _chunk_loop(ntail_ref[e], lambda j, e=e: zero_copy(tbase_ref[e] + j).start())
        _chunk_loop(n_unused, lambda j: zero_tile_copy(nu_ref[0] + j).start())
        for e in range(tbase_ref.shape[0]):
            _chunk_loop(ntail_ref[e], lambda j: zero_copy(0).wait())
        _chunk_loop(n_unused, lambda j: zero_tile_copy(0).wait())

    def wait_slot(slot_, n_):
        _wait_chunks(n_, max_chunks, lambda k: pltpu.make_async_copy(
            loc_ref.at[slot_, pl.ds(0, k * CHUNK_ROWS)], xs_ref.at[pl.ds(0, k * CHUNK_ROWS)],
            sem.at[slot_]))

    @pl.when(i >= 2)
    def _():
        wait_slot(slot, nch_ref[i - 2])

    mt = metat_ref[0]
    d1 = mt[0:1].astype(jnp.int32)
    d2 = mt[1:2].astype(jnp.int32)
    r = lax.broadcasted_iota(jnp.int32, (lrows, tm), 0)
    sel = jnp.where((r == d1) | (r == d2), 1.0, 0.0).astype(_BF16)
    loc_ref[slot] = _dot(sel, h_ref[0]).astype(_BF16)
    n = nch_ref[i]
    _chunk_loop(n, lambda c: chunk_copy(slot, c, cmap_ref[i * max_chunks + c]).start())

    @pl.when(i == nt - 1)
    def _():
        wait_slot(slot, n)

        @pl.when(i >= 1)
        def _():
            wait_slot(1 - slot, nch_ref[i - 1])


def _experts_kernel(te_ref, tf_ref, nu_ref, xs_ref, w1_ref, w3_ref, w2_ref, ys_ref,
                    wb1_ref, wb3_ref, wb2_ref):
    i = pl.program_id(0)

    @pl.when(i < nu_ref[0])
    def _():
        @pl.when(tf_ref[i] == 1)
        def _():
            wb1_ref[...] = w1_ref[0].astype(_BF16)
            wb3_ref[...] = w3_ref[0].astype(_BF16)
            wb2_ref[...] = w2_ref[0].astype(_BF16)

        half = xs_ref.shape[0] // 2
        for r in range(2):
            rows = slice(r * half, (r + 1) * half)
            xs = xs_ref[rows, :]
            a = _silu(_dot(xs, wb1_ref[...])) * _dot(xs, wb3_ref[...])
            ys_ref[rows, :] = _dot(a.astype(_BF16), wb2_ref[...]).astype(_BF16)

    @pl.when(i >= nu_ref[0])
    def _():
        ys_ref[...] = jnp.zeros(ys_ref.shape, ys_ref.dtype)


def _combine_kernel(nch_ref, cmap_ref, x_ref, ada_ref, gpost_ref, meta_ref, ys_ref, o_ref,
                    loc_ref, sem, *, max_chunks):
    i = pl.program_id(0)
    nt = pl.num_programs(0)
    slot = i % 2
    lrows, tm = loc_ref.shape[1], x_ref.shape[1]

    def chunk_copy(tile, slot_, c):
        src_chunk = cmap_ref[tile * max_chunks + c]
        return pltpu.make_async_copy(
            ys_ref.at[pl.ds(pl.multiple_of(src_chunk * CHUNK_ROWS, CHUNK_ROWS), CHUNK_ROWS)],
            loc_ref.at[slot_, pl.ds(pl.multiple_of(c * CHUNK_ROWS, CHUNK_ROWS), CHUNK_ROWS)],
            sem.at[slot_])

    def fetch(tile, slot_):
        _chunk_loop(nch_ref[tile], lambda c: chunk_copy(tile, slot_, c).start())

    @pl.when(i == 0)
    def _():
        loc_ref[...] = jnp.zeros(loc_ref.shape, loc_ref.dtype)
        fetch(0, 0)

    @pl.when(i + 1 < nt)
    def _():
        fetch(i + 1, 1 - slot)

    _wait_chunks(nch_ref[i], max_chunks, lambda k: pltpu.make_async_copy(
        ys_ref.at[pl.ds(0, k * CHUNK_ROWS)], loc_ref.at[slot, pl.ds(0, k * CHUNK_ROWS)],
        sem.at[slot]))

    meta = meta_ref[0]
    d1 = meta[:, 0:1].astype(jnp.int32)
    d2 = meta[:, 1:2].astype(jnp.int32)
    w1, w2 = meta[:, 2:3], meta[:, 3:4]
    r = lax.broadcasted_iota(jnp.int32, (tm, lrows), 1)
    ys = loc_ref[slot]
    y = (w1 * _dot(jnp.where(r == d1, 1.0, 0.0).astype(_BF16), ys)
         + w2 * _dot(jnp.where(r == d2, 1.0, 0.0).astype(_BF16), ys))
    gt2 = ada_ref[0][5:6]
    o_ref[0] = x_ref[0] + gt2 * (_rms(y) * gpost_ref[...])


def _moe_plan(cnt, max_chunks, n_gmm_tiles):
    per_tile = GMM_ROW_TILE // CHUNK_ROWS
    pc = (cnt + (CHUNK_ROWS - 1)) // CHUNK_ROWS
    local_first = jnp.cumsum(pc, axis=1) - pc
    nch = jnp.sum(pc, axis=1)
    e_chunks = jnp.sum(pc, axis=0)
    e_region = ((e_chunks + per_tile - 1) // per_tile) * per_tile
    e_first = jnp.cumsum(e_region) - e_region
    seg_first = e_first[None, :] + jnp.cumsum(pc, axis=0) - pc
    c = jnp.arange(max_chunks, dtype=jnp.int32)[None, :, None]
    inside = (c >= local_first[:, None, :]) & (c < (local_first + pc)[:, None, :])
    cmap = jnp.sum(jnp.where(inside, seg_first[:, None, :] + c - local_first[:, None, :], 0), axis=-1)
    n_used = jnp.sum(e_region) // per_tile
    t = jnp.arange(n_gmm_tiles, dtype=jnp.int32)
    t_used = jnp.minimum(t, n_used - 1)
    tile_e = jnp.sum(t_used[:, None] * per_tile >= (e_first + e_region)[None, :], axis=1)
    tile_first = jnp.concatenate([jnp.ones((1,), jnp.int32),
                                  (tile_e[1:] != tile_e[:-1]).astype(jnp.int32)])
    i32 = lambda a: a.astype(jnp.int32)
    return dict(nch=i32(nch), cmap=i32(cmap.reshape(-1)), tail_first=i32(e_first + e_chunks),
                n_tail=i32(e_region - e_chunks), tile_e=i32(tile_e), tile_first=tile_first,
                n_used=i32(n_used.reshape(1)))


def _moe(x, h2, meta, metat, cnt, ada, g_post, n_groups, w1, w3, w2):
    b, s, d = x.shape
    n_exp, _, ff = w1.shape
    tm = h2.shape[1]
    assert s % tm == 0 and tm % LANES == 0 and GMM_ROW_TILE % CHUNK_ROWS == 0
    tiles_per_batch = s // tm
    nt = b * tiles_per_batch
    max_chunks = (TOP_K_IN_GROUP * tm + n_exp * (CHUNK_ROWS - 1)) // CHUNK_ROWS
    lrows = -(-max_chunks * CHUNK_ROWS // LANES) * LANES
    per_tile = GMM_ROW_TILE // CHUNK_ROWS
    n_gmm_tiles = -(-(nt * max_chunks + n_exp * (per_tile - 1)) // per_tile)
    n_rows = n_gmm_tiles * GMM_ROW_TILE

    row = lambda a: a.reshape(1, -1).astype(_F32)
    n_ada = ada.shape[1]
    xt = x.reshape(nt, tm, d)

    plan = _moe_plan(cnt[:, 0, n_groups:n_groups + n_exp], max_chunks, n_gmm_tiles)

    xs = pl.pallas_call(
        functools.partial(_dispatch_kernel, max_chunks=max_chunks),
        grid_spec=pltpu.PrefetchScalarGridSpec(
            num_scalar_prefetch=5,
            grid=(nt,),
            in_specs=[
                pl.BlockSpec((1, 8, tm), lambda i, *_: (i, 0, 0)),
                pl.BlockSpec((1, tm, d), lambda i, *_: (i, 0, 0)),
            ],
            out_specs=pl.BlockSpec(memory_space=pl.ANY),
            scratch_shapes=[
                pltpu.VMEM((2, lrows, d), _BF16),
                pltpu.VMEM((GMM_ROW_TILE, d), _BF16),
                pltpu.SemaphoreType.DMA((2,)),
                pltpu.SemaphoreType.DMA((2,)),
            ],
        ),
        out_shape=jax.ShapeDtypeStruct((n_rows, d), _BF16),
        compiler_params=pltpu.CompilerParams(dimension_semantics=("arbitrary",)),
        name="dispatch",
    )(plan["nch"], plan["cmap"], plan["tail_first"], plan["n_tail"], plan["n_used"], metat, h2)

    ys = pl.pallas_call(
        _experts_kernel,
        grid_spec=pltpu.PrefetchScalarGridSpec(
            num_scalar_prefetch=3,
            grid=(n_gmm_tiles,),
            in_specs=[
                pl.BlockSpec((GMM_ROW_TILE, d), lambda i, te, tf, nu: (jnp.minimum(i, nu[0] - 1), 0)),
                pl.BlockSpec((1, d, ff), lambda i, te, tf, nu: (te[i], 0, 0)),
                pl.BlockSpec((1, d, ff), lambda i, te, tf, nu: (te[i], 0, 0)),
                pl.BlockSpec((1, ff, d), lambda i, te, tf, nu: (te[i], 0, 0)),
            ],
            out_specs=pl.BlockSpec((GMM_ROW_TILE, d), lambda i, te, tf, nu: (i, 0)),
            scratch_shapes=[
                pltpu.VMEM((d, ff), _BF16),
                pltpu.VMEM((d, ff), _BF16),
                pltpu.VMEM((ff, d), _BF16),
            ],
        ),
        out_shape=jax.ShapeDtypeStruct((n_rows, d), _BF16),
        compiler_params=pltpu.CompilerParams(dimension_semantics=("arbitrary",),
                                             vmem_limit_bytes=VMEM_LIMIT_BYTES),
        name="experts",
    )(plan["tile_e"], plan["tile_first"], plan["n_used"], xs, w1, w3, w2)

    out = pl.pallas_call(
        functools.partial(_combine_kernel, max_chunks=max_chunks),
        grid_spec=pltpu.PrefetchScalarGridSpec(
            num_scalar_prefetch=2,
            grid=(nt,),
            in_specs=[
                pl.BlockSpec((1, tm, d), lambda i, *_: (i, 0, 0)),
                pl.BlockSpec((1, n_ada, d), lambda i, *_: (i // tiles_per_batch, 0, 0)),
                pl.BlockSpec((1, d), lambda i, *_: (0, 0)),
                pl.BlockSpec((1, tm, LANES), lambda i, *_: (i, 0, 0)),
                pl.BlockSpec(memory_space=pl.ANY),
            ],
            out_specs=pl.BlockSpec((1, tm, d), lambda i, *_: (i, 0, 0)),
            scratch_shapes=[
                pltpu.VMEM((2, lrows, d), _BF16),
                pltpu.SemaphoreType.DMA((2,)),
            ],
        ),
        out_shape=jax.ShapeDtypeStruct((nt, tm, d), _F32),
        compiler_params=pltpu.CompilerParams(dimension_semantics=("arbitrary",)),
        name="combine",
    )(plan["nch"], plan["cmap"], xt, ada, row(g_post), meta, ys)
    return out.reshape(b, s, d)


def kernel(x, c, w_ada, b_ada, g_pre_mix, g_post_mix, w_in, ln_v_g, ln_v_b, w_spatial, b_spatial,
           lb_logits, g_hgrn_norm, w_out, g_pre_ffn, g_post_ffn, w_router_group, b_router_group,
           w_router_expert, b_router_expert, w1, w3, w2):
    depth = w_in.shape[0]
    b, s, d = x.shape
    lb_all = _lower_bounds(lb_logits)
    for layer in range(depth):
        ada = _ada(c, w_ada[layer], b_ada[layer]).reshape(b, 6, d)
        x, h2, meta, metat, cnt = _mixer(
            x, ada, g_pre_mix[layer], g_post_mix[layer], w_in[layer], ln_v_g[layer], ln_v_b[layer],
            w_spatial[layer], b_spatial[layer], lb_all[layer], g_hgrn_norm[layer], w_out[layer],
            g_pre_ffn[layer], w_router_group[layer], b_router_group[layer],
            w_router_expert[layer], b_router_expert[layer])
        x = _moe(x, h2, meta, metat, cnt, ada, g_post_ffn[layer], w_router_group.shape[-1],
                 w1[layer], w3[layer], w2[layer])
    return x
```

```python
import functools

import jax
import jax.numpy as jnp
from jax import lax
from jax.experimental import pallas as pl
from jax.experimental.pallas import tpu as pltpu

RMS_EPS = 1e-6
LN_EPS = 1e-5
HGRN_HEAD_DIM = 128
HGRN_CHUNK = 128
N_IN_SLICES = 8
TOP_K_IN_GROUP = 2
LANES = 128
MIXER_SEQ_TILE = 256
CHUNK_ROWS = 16
GMM_ROW_TILE = 512
MOE_TILES_PER_STEP = 2
VMEM_LIMIT_BYTES = 56 * 1024 * 1024

_F32 = jnp.float32
_BF16 = jnp.bfloat16


def _sigmoid(x):
    return 0.5 * (jnp.tanh(0.5 * x) + 1.0)


def _silu(x):
    return x * _sigmoid(x)


def _gelu_tanh(x):
    c = 0.7978845608028654
    return 0.5 * x * (1.0 + jnp.tanh(c * (x + 0.044715 * (x * x * x))))


def _rms(x):
    return x * lax.rsqrt(jnp.mean(x * x, axis=-1, keepdims=True) + RMS_EPS)


def _dot(a, b):
    return jnp.dot(a, b, preferred_element_type=_F32)


def _dot_nt(a, b):
    return lax.dot_general(a, b, (((1,), (1,)), ((), ())), preferred_element_type=_F32)


def _dot_tn(a, b):
    return lax.dot_general(a, b, (((0,), (0,)), ((), ())), preferred_element_type=_F32)


def _split3(x):
    hi = x.astype(_BF16)
    r = x - hi.astype(_F32)
    mid = r.astype(_BF16)
    lo = (r - mid.astype(_F32)).astype(_BF16)
    return hi, mid, lo


def _ada_kernel(c_ref, w_ref, b_ref, o_ref):
    s = _silu(c_ref[...])
    w = w_ref[...]
    acc = jnp.zeros(o_ref.shape, _F32)
    for sp in _split3(s):
        for wp in _split3(w)[:2]:
            acc = acc + _dot(sp, wp)
    o_ref[...] = acc + b_ref[...]


def _ada(c, w_ada, b_ada):
    b, d = c.shape
    n = w_ada.shape[1]
    tn = d
    return pl.pallas_call(
        _ada_kernel,
        grid=(n // tn,),
        in_specs=[
            pl.BlockSpec((b, d), lambda j: (0, 0)),
            pl.BlockSpec((d, tn), lambda j: (0, j)),
            pl.BlockSpec((1, tn), lambda j: (0, j)),
        ],
        out_specs=pl.BlockSpec((b, tn), lambda j: (0, j)),
        out_shape=jax.ShapeDtypeStruct((b, n), _F32),
        name="ada",
    )(c, w_ada, b_ada.reshape(1, n))


def _lb_kernel(l_ref, o_ref):
    l = l_ref[...]
    e = jnp.exp(l - jnp.max(l, axis=0, keepdims=True))
    p = e / jnp.sum(e, axis=0, keepdims=True)
    rows = [p[0:1]]
    for i in range(1, p.shape[0]):
        rows.append(rows[-1] + p[i:i + 1])
    for i, r in enumerate(rows):
        o_ref[i:i + 1, :] = r


def _lower_bounds(lb_logits):
    return pl.pallas_call(
        _lb_kernel,
        out_shape=jax.ShapeDtypeStruct(lb_logits.shape, _F32),
        name="lower_bounds",
    )(lb_logits.astype(_F32))


def _mixer_kernel(x_ref, ada_ref, gpre_ref, gpost_ref, win_ref, lng_ref, lnb_ref, wsp_ref,
                  bspt_ref, lb_ref, ghg_ref, wout_ref, gpre2_ref, wrt_ref, brt_ref,
                  o_ref, h2_ref, meta_ref, metat_ref, cnt_ref, st_ref, ya_ref, ob_ref,
                  *, n_route_groups, per_group):
    ts, d = x_ref.shape[1], x_ref.shape[2]
    w = wout_ref.shape[0]
    n_groups, gchunk = wsp_ref.shape[0], wsp_ref.shape[1]
    gdim = w // n_groups
    n_heads = w // HGRN_HEAD_DIM
    c = HGRN_CHUNK

    @pl.when(pl.program_id(1) == 0)
    def _():
        st_ref[...] = jnp.zeros(st_ref.shape, _F32)

    x = x_ref[0]
    ada = ada_ref[0]
    sh1, sc1, gt1 = ada[0:1], ada[1:2], ada[2:3]
    h = (_rms(x) * gpre_ref[...]) * (1.0 + sc1) + sh1
    h_bf = h.astype(_BF16)

    def proj(j):
        return _dot(h_bf, win_ref[:, j * w:(j + 1) * w])

    u = _gelu_tanh(proj(0))
    v = _gelu_tanh(proj(1))
    mu = jnp.mean(v, axis=-1, keepdims=True)
    vc = v - mu
    var = jnp.mean(vc * vc, axis=-1, keepdims=True)
    vn = ((vc * lax.rsqrt(var + LN_EPS)) * lng_ref[...] + lnb_ref[...]).astype(_BF16)
    tri_g = (lax.broadcasted_iota(jnp.int32, (gchunk, gchunk), 0)
             >= lax.broadcasted_iota(jnp.int32, (gchunk, gchunk), 1))
    bspt = bspt_ref[...]
    n_gchunks = ts // gchunk
    for g in range(n_groups):
        wc = jnp.where(tri_g, wsp_ref[g], 0.0).astype(_BF16)
        cols = slice(g * gdim, (g + 1) * gdim)
        vg = jnp.concatenate([vn[n * gchunk:(n + 1) * gchunk, cols] for n in range(n_gchunks)], axis=1)
        zv = _dot(wc, vg)
        for n in range(n_gchunks):
            rows = slice(n * gchunk, (n + 1) * gchunk)
            ya_ref[rows, cols] = u[rows, cols] * (zv[:, n * gdim:(n + 1) * gdim] + bspt[:, g:g + 1])

    q = _silu(proj(2))
    lb = lb_ref[...]
    f = lb + (1.0 - lb) * _sigmoid(proj(3))
    lf = jnp.log(f)
    k = 1.0 - f
    iv = proj(4).astype(_BF16)
    r_i = lax.broadcasted_iota(jnp.int32, (ts, ts), 0)
    c_i = lax.broadcasted_iota(jnp.int32, (ts, ts), 1)
    ltri = jnp.where((r_i // c == c_i // c) & (c_i <= r_i), 1.0, 0.0).astype(_BF16)
    bcum = jnp.zeros((ts, w), _F32)
    for part in _split3(lf):
        bcum = bcum + _dot(ltri, part)
    hb = c // 2
    row_h = lax.broadcasted_iota(jnp.int32, (hb, c), 0)
    lane_h = lax.broadcasted_iota(jnp.int32, (hb, c), 1)
    top_mask = lane_h <= row_h
    left = lane_h < hb
    bot_mask = lane_h - hb <= row_h
    pair = 2 * HGRN_HEAD_DIM
    pr = lax.broadcasted_iota(jnp.int32, (pair, pair), 0)
    pc_i = lax.broadcasted_iota(jnp.int32, (pair, pair), 1)
    same_head = (pr < HGRN_HEAD_DIM) == (pc_i < HGRN_HEAD_DIM)
    up_rows = lax.broadcasted_iota(jnp.int32, (c, pair), 1) < HGRN_HEAD_DIM
    for n in range(ts // c):
        rows = slice(n * c, (n + 1) * c)
        bc = bcum[rows]
        b_a, b_m, b_b, b_l = (bc[hb // 2 - 1:hb // 2], bc[hb - 1:hb],
                              bc[hb + hb // 2 - 1:hb + hb // 2], bc[c - 1:c])
        stack = lambda ra, rb: jnp.concatenate(
            [jnp.broadcast_to(ra, (hb, w)), jnp.broadcast_to(rb, (hb, w))], axis=0)
        ref = stack(b_a, b_b)
        qx = q[rows] * jnp.exp(bc - ref)
        kx = k[rows] * jnp.exp(ref - bc)
        qd = (qx * stack(jnp.exp(b_a), jnp.exp(b_b))).astype(_BF16)
        kd = (kx * stack(jnp.exp(b_l - b_a), jnp.exp(b_l - b_b))).astype(_BF16)
        q_off = qx[hb:] * jnp.exp(b_b - b_m)
        k_off = kx[:hb] * jnp.exp(b_m - b_a)
        q3 = jnp.concatenate([qx, q_off], axis=0).astype(_BF16)
        k3 = jnp.concatenate([kx, k_off, jnp.zeros((hb, w), _F32)], axis=0).astype(_BF16)
        dec = jnp.exp(b_l)
        iv_c = iv[rows]
        for p in range(n_heads // 2):
            pcols = slice(p * pair, (p + 1) * pair)
            a_heads = []
            for hd in (2 * p, 2 * p + 1):
                cols = slice(hd * HGRN_HEAD_DIM, (hd + 1) * HGRN_HEAD_DIM)
                sc = _dot_nt(q3[:, cols], k3[:, cols])
                top = jnp.where(top_mask, sc[0:hb, 0:c], 0.0)
                bot = jnp.where(left, sc[2 * hb:3 * hb, c:2 * c],
                                jnp.where(bot_mask, sc[hb:2 * hb, 0:c], 0.0))
                a_heads.append(jnp.concatenate([top, bot], axis=0))
            a_pair = jnp.concatenate(a_heads, axis=1).astype(_BF16)
            iv_p = iv_c[:, pcols]
            iv_blk = jnp.concatenate([jnp.where(up_rows, iv_p, 0.0).astype(_BF16),
                                      jnp.where(up_rows, 0.0, iv_p).astype(_BF16)], axis=0)
            st = st_ref[p]
            o = _dot(a_pair, iv_blk) + _dot_nt(qd[:, pcols], st.astype(_BF16))
            upd = _dot_tn(iv_p, kd[:, pcols])
            st_ref[p] = st * dec[:, pcols] + jnp.where(same_head, upd, 0.0)
            for j in range(2):
                hcols = slice(j * HGRN_HEAD_DIM, (j + 1) * HGRN_HEAD_DIM)
                ob_ref[rows, p * pair + j * HGRN_HEAD_DIM:p * pair + (j + 1) * HGRN_HEAD_DIM] = _rms(o[:, hcols])
    yb = (ob_ref[...] * ghg_ref[...]) * _silu(proj(5))

    y = _sigmoid(proj(6)) * ya_ref[...] + _sigmoid(proj(7)) * yb
    out = _dot(y.astype(_BF16), wout_ref[...])
    x1 = x + gt1 * (_rms(out) * gpost_ref[...])
    o_ref[0] = x1

    _route_tile(x1, ada[3:4], ada[4:5], gpre2_ref[...], wrt_ref[...], brt_ref[...],
                h2_ref, meta_ref, metat_ref, cnt_ref, n_route_groups, per_group)


def _mixer(x, ada, g_pre, g_post, w_in, ln_g, ln_b, w_spatial, b_spatial, lb, g_hgrn, w_out,
           g_pre_ffn, w_rg, b_rg, w_re, b_re):
    b, s, d = x.shape
    w = w_out.shape[0]
    n_groups, gchunk, _ = w_spatial.shape
    ts = min(MIXER_SEQ_TILE, s)
    assert s % ts == 0 and ts % gchunk == 0 and ts % HGRN_CHUNK == 0 and ts % LANES == 0
    n_rg, n_exp = w_rg.shape[1], w_re.shape[1]
    assert n_rg + n_exp <= LANES
    pad = LANES - n_rg - n_exp
    wrt = jnp.concatenate([w_rg, w_re, jnp.zeros((d, pad), _F32)], axis=1).T.astype(_BF16)
    brt = jnp.concatenate([b_rg, b_re, jnp.zeros((pad,), _F32)]).reshape(LANES, 1).astype(_F32)
    tiles = s // ts
    nt = b * tiles
    tile3 = lambda i, j: (i * tiles + j, 0, 0)
    assert w % HGRN_HEAD_DIM == 0 and w // n_groups == LANES and w_in.shape == (d, N_IN_SLICES * w)
    n_heads = w // HGRN_HEAD_DIM
    assert n_heads % 2 == 0 and HGRN_HEAD_DIM == LANES
    row = lambda a: a.reshape(1, -1).astype(_F32)
    const2 = lambda i, j: (0, 0)
    const3 = lambda i, j: (0, 0, 0)
    single = dict(pipeline_mode=pl.Buffered(1))
    return pl.pallas_call(
        functools.partial(_mixer_kernel, n_route_groups=n_rg, per_group=n_exp // n_rg),
        grid=(b, s // ts),
        in_specs=[
            pl.BlockSpec((1, ts, d), lambda i, j: (i, j, 0)),
            pl.BlockSpec((1, ada.shape[1], d), lambda i, j: (i, 0, 0)),
            pl.BlockSpec((1, d), const2),
            pl.BlockSpec((1, d), const2),
            pl.BlockSpec((d, N_IN_SLICES * w), const2, **single),
            pl.BlockSpec((1, w), const2),
            pl.BlockSpec((1, w), const2),
            pl.BlockSpec((n_groups, gchunk, gchunk), const3, **single),
            pl.BlockSpec((gchunk, n_groups), const2),
            pl.BlockSpec((1, w), const2),
            pl.BlockSpec((1, w), const2),
            pl.BlockSpec((w, d), const2, **single),
            pl.BlockSpec((1, d), const2),
            pl.BlockSpec((LANES, d), const2),
            pl.BlockSpec((LANES, 1), const2),
        ],
        out_specs=[
            pl.BlockSpec((1, ts, d), lambda i, j: (i, j, 0)),
            pl.BlockSpec((1, ts, d), tile3),
            pl.BlockSpec((1, ts, LANES), tile3),
            pl.BlockSpec((1, 8, ts), tile3),
            pl.BlockSpec((1, 8, LANES), tile3),
        ],
        out_shape=[
            jax.ShapeDtypeStruct((b, s, d), _F32),
            jax.ShapeDtypeStruct((nt, ts, d), _BF16),
            jax.ShapeDtypeStruct((nt, ts, LANES), _F32),
            jax.ShapeDtypeStruct((nt, 8, ts), _F32),
            jax.ShapeDtypeStruct((nt, 8, LANES), jnp.int32),
        ],
        scratch_shapes=[
            pltpu.VMEM((n_heads // 2, 2 * HGRN_HEAD_DIM, 2 * HGRN_HEAD_DIM), _F32),
            pltpu.VMEM((ts, w), _F32),
            pltpu.VMEM((ts, w), _F32),
        ],
        compiler_params=pltpu.CompilerParams(
            dimension_semantics=("parallel", "arbitrary"),
            vmem_limit_bytes=VMEM_LIMIT_BYTES),
        name="mixer",
    )(x, ada, row(g_pre), row(g_post), w_in.astype(_BF16), row(ln_g), row(ln_b),
      w_spatial.astype(_F32), b_spatial.T.astype(_F32), row(lb), row(g_hgrn), w_out.astype(_BF16),
      row(g_pre_ffn), wrt, brt)


def _first_max(vals):
    m = functools.reduce(jnp.maximum, vals)
    idx = jnp.full(m.shape, len(vals) - 1, jnp.int32)
    for j in range(len(vals) - 2, -1, -1):
        idx = jnp.where(vals[j] == m, j, idx)
    return m, idx


def _route(lt, n_groups, per_group):
    lg = [lt[g:g + 1] for g in range(n_groups)]
    mg, g_idx = _first_max(lg)
    p_top = 1.0 / functools.reduce(lambda a, b: a + b, [jnp.exp(l - mg) for l in lg])
    le = []
    for j in range(per_group):
        v = lt[n_groups + j:n_groups + j + 1]
        for g in range(1, n_groups):
            r0 = n_groups + g * per_group + j
            v = jnp.where(g_idx == g, lt[r0:r0 + 1], v)
        le.append(v)
    m1, j1 = _first_max(le)
    m2, j2 = _first_max([jnp.where(j1 == j, -jnp.inf, v) for j, v in enumerate(le)])
    r = jnp.exp(m2 - m1)
    w1 = p_top / (1.0 + r)
    w2 = w1 * r
    return g_idx * per_group + j1, g_idx * per_group + j2, w1, w2


def _route_tile(x, sh2, sc2, gpre, wrt, brt, h_ref, meta_ref, metat_ref, cnt_ref, n_groups, per_group):
    tm = x.shape[0]
    h = (_rms(x) * gpre) * (1.0 + sc2) + sh2
    hb = h.astype(_BF16)
    h_ref[0] = hb
    lt = _dot_nt(wrt, hb) + brt
    e1, e2, w1, w2 = _route(lt, n_groups, per_group)

    sub = lax.broadcasted_iota(jnp.int32, (LANES, tm), 0)
    ind = jnp.where((sub == e1) | (sub == e2), 1.0, 0.0)
    earlier = (lax.broadcasted_iota(jnp.int32, (tm, tm), 0)
               < lax.broadcasted_iota(jnp.int32, (tm, tm), 1))
    rank = _dot(ind.astype(_BF16), jnp.where(earlier, 1.0, 0.0).astype(_BF16))
    cnt = jnp.sum(ind, axis=1, keepdims=True)
    chunks = jnp.floor((cnt + (CHUNK_ROWS - 1)) * (1.0 / CHUNK_ROWS))
    below = (lax.broadcasted_iota(jnp.int32, (LANES, LANES), 1)
             < lax.broadcasted_iota(jnp.int32, (LANES, LANES), 0))
    first_chunk = _dot(jnp.where(below, 1.0, 0.0).astype(_BF16),
                       jnp.broadcast_to(chunks, (LANES, LANES)).astype(_BF16))
    dest = first_chunk[:, 0:1] * CHUNK_ROWS + rank
    n_rows = -(-n_groups * per_group // 8) * 8
    sub_e = lax.broadcasted_iota(jnp.int32, (n_rows, tm), 0)
    d1 = jnp.sum(jnp.where(sub_e == e1, dest[:n_rows], 0.0), axis=0, keepdims=True)
    d2 = jnp.sum(jnp.where(sub_e == e2, dest[:n_rows], 0.0), axis=0, keepdims=True)
    row8 = lax.broadcasted_iota(jnp.int32, (8, tm), 0)
    metat = jnp.where(row8 == 0, d1, jnp.where(row8 == 1, d2,
                      jnp.where(row8 == 2, w1, jnp.where(row8 == 3, w2, 0.0))))
    metat_ref[0] = metat
    meta_ref[0] = jnp.concatenate([metat, jnp.zeros((LANES - 8, tm), _F32)], axis=0).T
    cnt_ref[0] = jnp.broadcast_to(cnt, (LANES, LANES)).T[0:8].astype(jnp.int32)


def _chunk_loop(n, fn):
    def body(c, carry):
        fn(c)
        return carry
    lax.fori_loop(0, n, body, 0)


def _wait_chunks(n, max_chunks, copy_of_chunks):
    bit = 1
    while bit * 2 <= max_chunks:
        bit *= 2
    while bit >= 1:
        @pl.when((n & bit) != 0)
        def _(bit=bit):
            copy_of_chunks(bit).wait()
        bit //= 2


def _dispatch_kernel(nch_ref, cmap_ref, tbase_ref, ntail_ref, nu_ref, metat_ref, h_ref, xs_ref,
                     loc_ref, zero_ref, sem, zsem, *, max_chunks):
    i = pl.program_id(0)
    n_steps = pl.num_programs(0)
    g = h_ref.shape[0]
    lrows, tm = loc_ref.shape[1], h_ref.shape[1]
    buf = lambda step, j: (step % 2) * g + j

    def chunk_copy(slot_, c, dst_chunk):
        return pltpu.make_async_copy(
            loc_ref.at[slot_, pl.ds(pl.multiple_of(c * CHUNK_ROWS, CHUNK_ROWS), CHUNK_ROWS)],
            xs_ref.at[pl.ds(pl.multiple_of(dst_chunk * CHUNK_ROWS, CHUNK_ROWS), CHUNK_ROWS)],
            sem.at[slot_])

    def zero_copy(dst_chunk):
        return pltpu.make_async_copy(
            zero_ref.at[pl.ds(0, CHUNK_ROWS)],
            xs_ref.at[pl.ds(pl.multiple_of(dst_chunk * CHUNK_ROWS, CHUNK_ROWS), CHUNK_ROWS)],
            zsem.at[0])

    def zero_tile_copy(tile):
        return pltpu.make_async_copy(
            zero_ref,
            xs_ref.at[pl.ds(pl.multiple_of(tile * GMM_ROW_TILE, GMM_ROW_TILE), GMM_ROW_TILE)],
            zsem.at[1])

    @pl.when(i == 0)
    def _():
        zero_ref[...] = jnp.zeros(zero_ref.shape, zero_ref.dtype)
        n_unused = xs_ref.shape[0] // GMM_ROW_TILE - nu_ref[0]
        for e in range(tbase_ref.shape[0]):
            _chunk_loop(ntail_ref[e], lambda j, e=e: zero_copy(tbase_ref[e] + j).start())
        _chunk_loop(n_unused, lambda j: zero_tile_copy(nu_ref[0] + j).start())
        for e in range(tbase_ref.shape[0]):
            _chunk_loop(ntail_ref[e], lambda j: zero_copy(0).wait())
        _chunk_loop(n_unused, lambda j: zero_tile_copy(0).wait())

    def wait_slot(slot_, n_):
        _wait_chunks(n_, max_chunks, lambda k: pltpu.make_async_copy(
            loc_ref.at[slot_, pl.ds(0, k * CHUNK_ROWS)], xs_ref.at[pl.ds(0, k * CHUNK_ROWS)],
            sem.at[slot_]))

    @pl.when(i >= 2)
    def _():
        for j in range(g):
            wait_slot(buf(i, j), nch_ref[(i - 2) * g + j])

    r = lax.broadcasted_iota(jnp.int32, (lrows, tm), 0)
    for j in range(g):
        mt = metat_ref[j]
        d1 = mt[0:1].astype(jnp.int32)
        d2 = mt[1:2].astype(jnp.int32)
        sel = jnp.where((r == d1) | (r == d2), 1.0, 0.0).astype(_BF16)
        loc_ref[buf(i, j)] = _dot(sel, h_ref[j]).astype(_BF16)
    for j in range(g):
        tile = i * g + j
        _chunk_loop(nch_ref[tile], lambda c, j=j, tile=tile: chunk_copy(
            buf(i, j), c, cmap_ref[tile * max_chunks + c]).start())

    @pl.when(i == n_steps - 1)
    def _():
        for j in range(g):
            wait_slot(buf(i, j), nch_ref[i * g + j])

        @pl.when(i >= 1)
        def _():
            for j in range(g):
                wait_slot(buf(i - 1, j), nch_ref[(i - 1) * g + j])


def _experts_kernel(te_ref, tf_ref, nu_ref, xs_ref, w1_ref, w3_ref, w2_ref, ys_ref,
                    wb1_ref, wb3_ref, wb2_ref):
    i = pl.program_id(0)

    @pl.when(i < nu_ref[0])
    def _():
        @pl.when(tf_ref[i] == 1)
        def _():
            wb1_ref[...] = w1_ref[0].astype(_BF16)
            wb3_ref[...] = w3_ref[0].astype(_BF16)
            wb2_ref[...] = w2_ref[0].astype(_BF16)

        xs = xs_ref[...]
        a = _silu(_dot(xs, wb1_ref[...])) * _dot(xs, wb3_ref[...])
        ys_ref[...] = _dot(a.astype(_BF16), wb2_ref[...]).astype(_BF16)

    @pl.when(i >= nu_ref[0])
    def _():
        ys_ref[...] = jnp.zeros(ys_ref.shape, ys_ref.dtype)


def _combine_kernel(nch_ref, cmap_ref, x_ref, ada_ref, gpost_ref, meta_ref, ys_ref, o_ref,
                    loc_ref, sem, *, max_chunks):
    i = pl.program_id(0)
    n_steps = pl.num_programs(0)
    g = x_ref.shape[0]
    lrows, tm = loc_ref.shape[1], x_ref.shape[1]
    buf = lambda step, j: (step % 2) * g + j

    def chunk_copy(tile, slot_, c):
        src_chunk = cmap_ref[tile * max_chunks + c]
        return pltpu.make_async_copy(
            ys_ref.at[pl.ds(pl.multiple_of(src_chunk * CHUNK_ROWS, CHUNK_ROWS), CHUNK_ROWS)],
            loc_ref.at[slot_, pl.ds(pl.multiple_of(c * CHUNK_ROWS, CHUNK_ROWS), CHUNK_ROWS)],
            sem.at[slot_])

    def fetch(step):
        for j in range(g):
            tile = step * g + j
            _chunk_loop(nch_ref[tile], lambda c, j=j, tile=tile: chunk_copy(tile, buf(step, j), c).start())

    @pl.when(i == 0)
    def _():
        loc_ref[...] = jnp.zeros(loc_ref.shape, loc_ref.dtype)
        fetch(0)

    @pl.when(i + 1 < n_steps)
    def _():
        fetch(i + 1)

    for j in range(g):
        _wait_chunks(nch_ref[i * g + j], max_chunks, lambda k, j=j: pltpu.make_async_copy(
            ys_ref.at[pl.ds(0, k * CHUNK_ROWS)], loc_ref.at[buf(i, j), pl.ds(0, k * CHUNK_ROWS)],
            sem.at[buf(i, j)]))

    r = lax.broadcasted_iota(jnp.int32, (tm, lrows), 1)
    gt2 = ada_ref[0][5:6]
    for j in range(g):
        meta = meta_ref[j]
        d1 = meta[:, 0:1].astype(jnp.int32)
        d2 = meta[:, 1:2].astype(jnp.int32)
        w1, w2 = meta[:, 2:3], meta[:, 3:4]
        ys = loc_ref[buf(i, j)]
        y = (w1 * _dot(jnp.where(r == d1, 1.0, 0.0).astype(_BF16), ys)
             + w2 * _dot(jnp.where(r == d2, 1.0, 0.0).astype(_BF16), ys))
        o_ref[j] = x_ref[j] + gt2 * (_rms(y) * gpost_ref[...])


def _moe_plan(cnt, max_chunks, n_gmm_tiles):
    per_tile = GMM_ROW_TILE // CHUNK_ROWS
    pc = (cnt + (CHUNK_ROWS - 1)) // CHUNK_ROWS
    local_first = jnp.cumsum(pc, axis=1) - pc
    nch = jnp.sum(pc, axis=1)
    e_chunks = jnp.sum(pc, axis=0)
    e_region = ((e_chunks + per_tile - 1) // per_tile) * per_tile
    e_first = jnp.cumsum(e_region) - e_region
    seg_first = e_first[None, :] + jnp.cumsum(pc, axis=0) - pc
    c = jnp.arange(max_chunks, dtype=jnp.int32)[None, :, None]
    inside = (c >= local_first[:, None, :]) & (c < (local_first + pc)[:, None, :])
    cmap = jnp.sum(jnp.where(inside, seg_first[:, None, :] + c - local_first[:, None, :], 0), axis=-1)
    n_used = jnp.sum(e_region) // per_tile
    t = jnp.arange(n_gmm_tiles, dtype=jnp.int32)
    t_used = jnp.minimum(t, n_used - 1)
    tile_e = jnp.sum(t_used[:, None] * per_tile >= (e_first + e_region)[None, :], axis=1)
    tile_first = jnp.concatenate([jnp.ones((1,), jnp.int32),
                                  (tile_e[1:] != tile_e[:-1]).astype(jnp.int32)])
    i32 = lambda a: a.astype(jnp.int32)
    return dict(nch=i32(nch), cmap=i32(cmap.reshape(-1)), tail_first=i32(e_first + e_chunks),
                n_tail=i32(e_region - e_chunks), tile_e=i32(tile_e), tile_first=tile_first,
                n_used=i32(n_used.reshape(1)))


def _moe(x, h2, meta, metat, cnt, ada, g_post, n_groups, w1, w3, w2):
    b, s, d = x.shape
    n_exp, _, ff = w1.shape
    tm = h2.shape[1]
    assert s % tm == 0 and tm % LANES == 0 and GMM_ROW_TILE % CHUNK_ROWS == 0
    tiles_per_batch = s // tm
    nt = b * tiles_per_batch
    max_chunks = (TOP_K_IN_GROUP * tm + n_exp * (CHUNK_ROWS - 1)) // CHUNK_ROWS
    lrows = -(-max_chunks * CHUNK_ROWS // LANES) * LANES
    per_tile = GMM_ROW_TILE // CHUNK_ROWS
    n_gmm_tiles = -(-(nt * max_chunks + n_exp * (per_tile - 1)) // per_tile)
    n_rows = n_gmm_tiles * GMM_ROW_TILE

    row = lambda a: a.reshape(1, -1).astype(_F32)
    n_ada = ada.shape[1]
    xt = x.reshape(nt, tm, d)
    g = MOE_TILES_PER_STEP if tiles_per_batch % MOE_TILES_PER_STEP == 0 else 1
    steps_per_batch = tiles_per_batch // g

    plan = _moe_plan(cnt[:, 0, :n_exp], max_chunks, n_gmm_tiles)

    xs = pl.pallas_call(
        functools.partial(_dispatch_kernel, max_chunks=max_chunks),
        grid_spec=pltpu.PrefetchScalarGridSpec(
            num_scalar_prefetch=5,
            grid=(nt // g,),
            in_specs=[
                pl.BlockSpec((g, 8, tm), lambda i, *_: (i, 0, 0)),
                pl.BlockSpec((g, tm, d), lambda i, *_: (i, 0, 0)),
            ],
            out_specs=pl.BlockSpec(memory_space=pl.ANY),
            scratch_shapes=[
                pltpu.VMEM((2 * g, lrows, d), _BF16),
                pltpu.VMEM((GMM_ROW_TILE, d), _BF16),
                pltpu.SemaphoreType.DMA((2 * g,)),
                pltpu.SemaphoreType.DMA((2,)),
            ],
        ),
        out_shape=jax.ShapeDtypeStruct((n_rows, d), _BF16),
        compiler_params=pltpu.CompilerParams(dimension_semantics=("arbitrary",)),
        name="dispatch",
    )(plan["nch"], plan["cmap"], plan["tail_first"], plan["n_tail"], plan["n_used"], metat, h2)

    ys = pl.pallas_call(
        _experts_kernel,
        grid_spec=pltpu.PrefetchScalarGridSpec(
            num_scalar_prefetch=3,
            grid=(n_gmm_tiles,),
            in_specs=[
                pl.BlockSpec((GMM_ROW_TILE, d), lambda i, te, tf, nu: (jnp.minimum(i, nu[0] - 1), 0)),
                pl.BlockSpec((1, d, ff), lambda i, te, tf, nu: (te[i], 0, 0)),
                pl.BlockSpec((1, d, ff), lambda i, te, tf, nu: (te[i], 0, 0)),
                pl.BlockSpec((1, ff, d), lambda i, te, tf, nu: (te[i], 0, 0)),
            ],
            out_specs=pl.BlockSpec((GMM_ROW_TILE, d), lambda i, te, tf, nu: (i, 0)),
            scratch_shapes=[
                pltpu.VMEM((d, ff), _BF16),
                pltpu.VMEM((d, ff), _BF16),
                pltpu.VMEM((ff, d), _BF16),
            ],
        ),
        out_shape=jax.ShapeDtypeStruct((n_rows, d), _BF16),
        compiler_params=pltpu.CompilerParams(dimension_semantics=("arbitrary",),
                                             vmem_limit_bytes=VMEM_LIMIT_BYTES),
        name="experts",
    )(plan["tile_e"], plan["tile_first"], plan["n_used"], xs, w1, w3, w2)

    out = pl.pallas_call(
        functools.partial(_combine_kernel, max_chunks=max_chunks),
        grid_spec=pltpu.PrefetchScalarGridSpec(
            num_scalar_prefetch=2,
            grid=(nt // g,),
            in_specs=[
                pl.BlockSpec((g, tm, d), lambda i, *_: (i, 0, 0)),
                pl.BlockSpec((1, n_ada, d), lambda i, *_: (i // steps_per_batch, 0, 0)),
                pl.BlockSpec((1, d), lambda i, *_: (0, 0)),
                pl.BlockSpec((g, tm, LANES), lambda i, *_: (i, 0, 0)),
                pl.BlockSpec(memory_space=pl.ANY),
            ],
            out_specs=pl.BlockSpec((g, tm, d), lambda i, *_: (i, 0, 0)),
            scratch_shapes=[
                pltpu.VMEM((2 * g, lrows, d), _BF16),
                pltpu.SemaphoreType.DMA((2 * g,)),
            ],
        ),
        out_shape=jax.ShapeDtypeStruct((nt, tm, d), _F32),
        compiler_params=pltpu.CompilerParams(dimension_semantics=("arbitrary",)),
        name="combine",
    )(plan["nch"], plan["cmap"], xt, ada, row(g_post), meta, ys)
    return out.reshape(b, s, d)


def kernel(x, c, w_ada, b_ada, g_pre_mix, g_post_mix, w_in, ln_v_g, ln_v_b, w_spatial, b_spatial,
           lb_logits, g_hgrn_norm, w_out, g_pre_ffn, g_post_ffn, w_router_group, b_router_group,
           w_router_expert, b_router_expert, w1, w3, w2):
    depth = w_in.shape[0]
    b, s, d = x.shape
    lb_all = _lower_bounds(lb_logits)
    for layer in range(depth):
        ada = _ada(c, w_ada[layer], b_ada[layer]).reshape(b, 6, d)
        x, h2, meta, metat, cnt = _mixer(
            x, ada, g_pre_mix[layer], g_post_mix[layer], w_in[layer], ln_v_g[layer], ln_v_b[layer],
            w_spatial[layer], b_spatial[layer], lb_all[layer], g_hgrn_norm[layer], w_out[layer],
            g_pre_ffn[layer], w_router_group[layer], b_router_group[layer],
            w_router_expert[layer], b_router_expert[layer])
        x = _moe(x, h2, meta, metat, cnt, ada, g_post_ffn[layer], w_router_group.shape[-1],
                 w1[layer], w3[layer], w2[layer])
    return x
```

```python
import functools

import jax
import jax.numpy as jnp
from jax import lax
from jax.experimental import pallas as pl
from jax.experimental.pallas import tpu as pltpu

RMS_EPS = 1e-6
LN_EPS = 1e-5
HGRN_HEAD_DIM = 128
HGRN_CHUNK = 128
N_IN_SLICES = 8
TOP_K_IN_GROUP = 2
LANES = 128
MIXER_SEQ_TILE = 256
PROJ_COLS = 256
CHUNK_ROWS = 16
GMM_ROW_TILE = 512
MOE_TILES_PER_STEP = 2
VMEM_LIMIT_BYTES = 56 * 1024 * 1024

_F32 = jnp.float32
_BF16 = jnp.bfloat16


def _sigmoid(x):
    return 0.5 * (jnp.tanh(0.5 * x) + 1.0)


def _silu(x):
    return x * _sigmoid(x)


def _gelu_tanh(x):
    c = 0.7978845608028654
    return 0.5 * x * (1.0 + jnp.tanh(c * (x + 0.044715 * (x * x * x))))


def _rms(x):
    return x * lax.rsqrt(jnp.mean(x * x, axis=-1, keepdims=True) + RMS_EPS)


def _dot(a, b):
    return jnp.dot(a, b, preferred_element_type=_F32)


def _dot_nt(a, b):
    return lax.dot_general(a, b, (((1,), (1,)), ((), ())), preferred_element_type=_F32)


def _dot_tn(a, b):
    return lax.dot_general(a, b, (((0,), (0,)), ((), ())), preferred_element_type=_F32)


def _split3(x):
    hi = x.astype(_BF16)
    r = x - hi.astype(_F32)
    mid = r.astype(_BF16)
    lo = (r - mid.astype(_F32)).astype(_BF16)
    return hi, mid, lo


def _ada_kernel(c_ref, w_ref, b_ref, o_ref):
    s = _silu(c_ref[...])
    w = w_ref[...]
    acc = jnp.zeros(o_ref.shape, _F32)
    for sp in _split3(s):
        for wp in _split3(w)[:2]:
            acc = acc + _dot(sp, wp)
    o_ref[...] = acc + b_ref[...]


def _ada(c, w_ada, b_ada):
    b, d = c.shape
    n = w_ada.shape[1]
    tn = d
    return pl.pallas_call(
        _ada_kernel,
        grid=(n // tn,),
        in_specs=[
            pl.BlockSpec((b, d), lambda j: (0, 0)),
            pl.BlockSpec((d, tn), lambda j: (0, j)),
            pl.BlockSpec((1, tn), lambda j: (0, j)),
        ],
        out_specs=pl.BlockSpec((b, tn), lambda j: (0, j)),
        out_shape=jax.ShapeDtypeStruct((b, n), _F32),
        name="ada",
    )(c, w_ada, b_ada.reshape(1, n))


def _lb_kernel(l_ref, o_ref):
    l = l_ref[...]
    e = jnp.exp(l - jnp.max(l, axis=0, keepdims=True))
    p = e / jnp.sum(e, axis=0, keepdims=True)
    rows = [p[0:1]]
    for i in range(1, p.shape[0]):
        rows.append(rows[-1] + p[i:i + 1])
    for i, r in enumerate(rows):
        o_ref[i:i + 1, :] = r


def _lower_bounds(lb_logits):
    return pl.pallas_call(
        _lb_kernel,
        out_shape=jax.ShapeDtypeStruct(lb_logits.shape, _F32),
        name="lower_bounds",
    )(lb_logits.astype(_F32))


def _mixer_kernel(x_ref, ada_ref, adap_ref, gpre_ref, gpost_ref, win_ref, lng_ref, lnb_ref, wsp_ref,
                  bspt_ref, lb_ref, ghg_ref, wout_ref, gpre2_ref, wrt_ref, brt_ref,
                  o_ref, h2_ref, meta_ref, metat_ref, cnt_ref, st_ref, ya_ref, ob_ref, x1p_ref,
                  act_ref, iv_ref, vn_ref,
                  *, n_route_groups, per_group, tiles_per_batch):
    ts, d = x_ref.shape[1], x_ref.shape[2]
    w = wout_ref.shape[0]
    n_groups, gchunk = wsp_ref.shape[0], wsp_ref.shape[1]
    gdim = w // n_groups
    n_heads = w // HGRN_HEAD_DIM
    c = HGRN_CHUNK
    t = pl.program_id(0)
    is_tile = t < pl.num_programs(0) - 1

    @pl.when(t % tiles_per_batch == 0)
    def _():
        st_ref[...] = jnp.zeros(st_ref.shape, _F32)

    @pl.when(t == 0)
    def _():
        x1p_ref[...] = jnp.zeros(x1p_ref.shape, _F32)

    x1_prev = x1p_ref[...]
    adap = adap_ref[0]
    _route_tile(x1_prev, adap[3:4], adap[4:5], gpre2_ref[...], wrt_ref[...], brt_ref[...],
                h2_ref, meta_ref, metat_ref, cnt_ref, n_route_groups, per_group)

    x = x_ref[0]
    ada = ada_ref[0]
    sh1, sc1, gt1 = ada[0:1], ada[1:2], ada[2:3]
    h = (_rms(x) * gpre_ref[...]) * (1.0 + sc1) + sh1
    h_bf = h.astype(_BF16)

    lb = lb_ref[...]
    (s_v, s_u, s_q, s_f, s_i, s_og, s_ga, s_gb) = (1, 0, 2, 3, 4, 5, 6, 7)
    (a_u, a_v, a_q, a_lf, a_k, a_og, a_ga, a_gb) = range(8)

    def activate(j, cols, r):
        if j == s_u:
            act_ref[a_u, :, cols] = _gelu_tanh(r)
        elif j == s_v:
            act_ref[a_v, :, cols] = _gelu_tanh(r)
        elif j == s_q:
            act_ref[a_q, :, cols] = _silu(r)
        elif j == s_f:
            f = lb[:, cols] + (1.0 - lb[:, cols]) * _sigmoid(r)
            act_ref[a_lf, :, cols] = jnp.log(f)
            act_ref[a_k, :, cols] = 1.0 - f
        elif j == s_i:
            iv_ref[:, cols] = r.astype(_BF16)
        elif j == s_og:
            act_ref[a_og, :, cols] = _silu(r)
        elif j == s_ga:
            act_ref[a_ga, :, cols] = _sigmoid(r)
        else:
            act_ref[a_gb, :, cols] = _sigmoid(r)

    def layer_norm_v():
        v = act_ref[a_v]
        mu = jnp.mean(v, axis=-1, keepdims=True)
        vc = v - mu
        var = jnp.mean(vc * vc, axis=-1, keepdims=True)
        vn_ref[...] = ((vc * lax.rsqrt(var + LN_EPS)) * lng_ref[...] + lnb_ref[...]).astype(_BF16)

    order = (s_v, s_u, s_q, s_f, s_i, s_og, s_ga, s_gb)
    pending = None
    for pos, j in enumerate(order):
        for cb in range(w // PROJ_COLS):
            cols = slice(cb * PROJ_COLS, (cb + 1) * PROJ_COLS)
            r = _dot(h_bf, win_ref[:, j * w + cb * PROJ_COLS:j * w + (cb + 1) * PROJ_COLS])
            if pending is not None:
                pending()
            pending = functools.partial(activate, j, cols, r)
            if pos == 1 and cb == 0:
                layer_norm_v()
    pending()

    u = act_ref[a_u]
    vn = vn_ref[...]
    tri_g = (lax.broadcasted_iota(jnp.int32, (gchunk, gchunk), 0)
             >= lax.broadcasted_iota(jnp.int32, (gchunk, gchunk), 1))
    bspt = bspt_ref[...]
    n_gchunks = ts // gchunk
    for g in range(n_groups):
        wc = jnp.where(tri_g, wsp_ref[g], 0.0).astype(_BF16)
        cols = slice(g * gdim, (g + 1) * gdim)
        vg = jnp.concatenate([vn[n * gchunk:(n + 1) * gchunk, cols] for n in range(n_gchunks)], axis=1)
        zv = _dot(wc, vg)
        for n in range(n_gchunks):
            rows = slice(n * gchunk, (n + 1) * gchunk)
            ya_ref[rows, cols] = u[rows, cols] * (zv[:, n * gdim:(n + 1) * gdim] + bspt[:, g:g + 1])

    q = act_ref[a_q]
    lf = act_ref[a_lf]
    k = act_ref[a_k]
    iv = iv_ref[...]
    r_i = lax.broadcasted_iota(jnp.int32, (ts, ts), 0)
    c_i = lax.broadcasted_iota(jnp.int32, (ts, ts), 1)
    ltri = jnp.where((r_i // c == c_i // c) & (c_i <= r_i), 1.0, 0.0).astype(_BF16)
    bcum = jnp.zeros((ts, w), _F32)
    for part in _split3(lf):
        bcum = bcum + _dot(ltri, part)
    hb = c // 2
    row_h = lax.broadcasted_iota(jnp.int32, (hb, c), 0)
    lane_h = lax.broadcasted_iota(jnp.int32, (hb, c), 1)
    top_mask = lane_h <= row_h
    left = lane_h < hb
    bot_mask = lane_h - hb <= row_h
    pair = 2 * HGRN_HEAD_DIM
    pr = lax.broadcasted_iota(jnp.int32, (pair, pair), 0)
    pc_i = lax.broadcasted_iota(jnp.int32, (pair, pair), 1)
    same_head = (pr < HGRN_HEAD_DIM) == (pc_i < HGRN_HEAD_DIM)
    up_rows = lax.broadcasted_iota(jnp.int32, (c, pair), 1) < HGRN_HEAD_DIM
    for n in range(ts // c):
        rows = slice(n * c, (n + 1) * c)
        bc = bcum[rows]
        b_a, b_m, b_b, b_l = (bc[hb // 2 - 1:hb // 2], bc[hb - 1:hb],
                              bc[hb + hb // 2 - 1:hb + hb // 2], bc[c - 1:c])
        stack = lambda ra, rb: jnp.concatenate(
            [jnp.broadcast_to(ra, (hb, w)), jnp.broadcast_to(rb, (hb, w))], axis=0)
        ref = stack(b_a, b_b)
        qx = q[rows] * jnp.exp(bc - ref)
        kx = k[rows] * jnp.exp(ref - bc)
        qd = (qx * stack(jnp.exp(b_a), jnp.exp(b_b))).astype(_BF16)
        kd = (kx * stack(jnp.exp(b_l - b_a), jnp.exp(b_l - b_b))).astype(_BF16)
        q_off = qx[hb:] * jnp.exp(b_b - b_m)
        k_off = kx[:hb] * jnp.exp(b_m - b_a)
        q3 = jnp.concatenate([qx, q_off], axis=0).astype(_BF16)
        k3 = jnp.concatenate([kx, k_off, jnp.zeros((hb, w), _F32)], axis=0).astype(_BF16)
        dec = jnp.exp(b_l)
        iv_c = iv[rows]
        for p in range(n_heads // 2):
            pcols = slice(p * pair, (p + 1) * pair)
            a_heads = []
            for hd in (2 * p, 2 * p + 1):
                cols = slice(hd * HGRN_HEAD_DIM, (hd + 1) * HGRN_HEAD_DIM)
                sc = _dot_nt(q3[:, cols], k3[:, cols])
                top = jnp.where(top_mask, sc[0:hb, 0:c], 0.0)
                bot = jnp.where(left, sc[2 * hb:3 * hb, c:2 * c],
                                jnp.where(bot_mask, sc[hb:2 * hb, 0:c], 0.0))
                a_heads.append(jnp.concatenate([top, bot], axis=0))
            a_pair = jnp.concatenate(a_heads, axis=1).astype(_BF16)
            iv_p = iv_c[:, pcols]
            iv_blk = jnp.concatenate([jnp.where(up_rows, iv_p, 0.0).astype(_BF16),
                                      jnp.where(up_rows, 0.0, iv_p).astype(_BF16)], axis=0)
            st = st_ref[p]
            o = _dot(a_pair, iv_blk) + _dot_nt(qd[:, pcols], st.astype(_BF16))
            upd = _dot_tn(iv_p, kd[:, pcols])
            st_ref[p] = st * dec[:, pcols] + jnp.where(same_head, upd, 0.0)
            for j in range(2):
                hcols = slice(j * HGRN_HEAD_DIM, (j + 1) * HGRN_HEAD_DIM)
                ob_ref[rows, p * pair + j * HGRN_HEAD_DIM:p * pair + (j + 1) * HGRN_HEAD_DIM] = _rms(o[:, hcols])
    yb = (ob_ref[...] * ghg_ref[...]) * act_ref[a_og]

    y = act_ref[a_ga] * ya_ref[...] + act_ref[a_gb] * yb
    out = _dot(y.astype(_BF16), wout_ref[...])
    x1 = jnp.where(is_tile, x + gt1 * (_rms(out) * gpost_ref[...]), x1_prev)
    o_ref[0] = x1
    x1p_ref[...] = x1


def _mixer(x, ada, g_pre, g_post, w_in, ln_g, ln_b, w_spatial, b_spatial, lb, g_hgrn, w_out,
           g_pre_ffn, w_rg, b_rg, w_re, b_re):
    b, s, d = x.shape
    w = w_out.shape[0]
    n_groups, gchunk, _ = w_spatial.shape
    ts = min(MIXER_SEQ_TILE, s)
    assert s % ts == 0 and ts % gchunk == 0 and ts % HGRN_CHUNK == 0 and ts % LANES == 0
    n_rg, n_exp = w_rg.shape[1], w_re.shape[1]
    assert n_rg + n_exp <= LANES
    pad = LANES - n_rg - n_exp
    wrt = jnp.concatenate([w_rg, w_re, jnp.zeros((d, pad), _F32)], axis=1).T.astype(_BF16)
    brt = jnp.concatenate([b_rg, b_re, jnp.zeros((pad,), _F32)]).reshape(LANES, 1).astype(_F32)
    tiles = s // ts
    nt = b * tiles
    cur = lambda t: jnp.minimum(t, nt - 1)
    prev = lambda t: jnp.maximum(t - 1, 0)
    cur3 = lambda t: (cur(t), 0, 0)
    prev3 = lambda t: (prev(t), 0, 0)
    assert w % HGRN_HEAD_DIM == 0 and w // n_groups == LANES and w_in.shape == (d, N_IN_SLICES * w)
    assert w % PROJ_COLS == 0
    n_heads = w // HGRN_HEAD_DIM
    assert n_heads % 2 == 0 and HGRN_HEAD_DIM == LANES
    row = lambda a: a.reshape(1, -1).astype(_F32)
    const2 = lambda t: (0, 0)
    const3 = lambda t: (0, 0, 0)
    single = dict(pipeline_mode=pl.Buffered(1))
    x1, h2, meta, metat, cnt = pl.pallas_call(
        functools.partial(_mixer_kernel, n_route_groups=n_rg, per_group=n_exp // n_rg,
                          tiles_per_batch=tiles),
        grid=(nt + 1,),
        in_specs=[
            pl.BlockSpec((1, ts, d), cur3),
            pl.BlockSpec((1, ada.shape[1], d), lambda t: (cur(t) // tiles, 0, 0)),
            pl.BlockSpec((1, ada.shape[1], d), lambda t: (prev(t) // tiles, 0, 0)),
            pl.BlockSpec((1, d), const2),
            pl.BlockSpec((1, d), const2),
            pl.BlockSpec((d, N_IN_SLICES * w), const2, **single),
            pl.BlockSpec((1, w), const2),
            pl.BlockSpec((1, w), const2),
            pl.BlockSpec((n_groups, gchunk, gchunk), const3, **single),
            pl.BlockSpec((gchunk, n_groups), const2),
            pl.BlockSpec((1, w), const2),
            pl.BlockSpec((1, w), const2),
            pl.BlockSpec((w, d), const2, **single),
            pl.BlockSpec((1, d), const2),
            pl.BlockSpec((LANES, d), const2),
            pl.BlockSpec((LANES, 1), const2),
        ],
        out_specs=[
            pl.BlockSpec((1, ts, d), cur3),
            pl.BlockSpec((1, ts, d), prev3),
            pl.BlockSpec((1, ts, LANES), prev3),
            pl.BlockSpec((1, 8, ts), prev3),
            pl.BlockSpec((1, 8, LANES), prev3),
        ],
        out_shape=[
            jax.ShapeDtypeStruct((nt, ts, d), _F32),
            jax.ShapeDtypeStruct((nt, ts, d), _BF16),
            jax.ShapeDtypeStruct((nt, ts, LANES), _F32),
            jax.ShapeDtypeStruct((nt, 8, ts), _F32),
            jax.ShapeDtypeStruct((nt, 8, LANES), jnp.int32),
        ],
        scratch_shapes=[
            pltpu.VMEM((n_heads // 2, 2 * HGRN_HEAD_DIM, 2 * HGRN_HEAD_DIM), _F32),
            pltpu.VMEM((ts, w), _F32),
            pltpu.VMEM((ts, w), _F32),
            pltpu.VMEM((ts, d), _F32),
            pltpu.VMEM((N_IN_SLICES, ts, w), _F32),
            pltpu.VMEM((ts, w), _BF16),
            pltpu.VMEM((ts, w), _BF16),
        ],
        compiler_params=pltpu.CompilerParams(
            dimension_semantics=("arbitrary",),
            vmem_limit_bytes=VMEM_LIMIT_BYTES),
        name="mixer",
    )(x.reshape(nt, ts, d), ada, ada, row(g_pre), row(g_post), w_in.astype(_BF16), row(ln_g),
      row(ln_b), w_spatial.astype(_F32), b_spatial.T.astype(_F32), row(lb), row(g_hgrn),
      w_out.astype(_BF16), row(g_pre_ffn), wrt, brt)
    return x1.reshape(b, s, d), h2, meta, metat, cnt


def _first_max(vals):
    m = functools.reduce(jnp.maximum, vals)
    idx = jnp.full(m.shape, len(vals) - 1, jnp.int32)
    for j in range(len(vals) - 2, -1, -1):
        idx = jnp.where(vals[j] == m, j, idx)
    return m, idx


def _route(lt, n_groups, per_group):
    lg = [lt[g:g + 1] for g in range(n_groups)]
    mg, g_idx = _first_max(lg)
    p_top = 1.0 / functools.reduce(lambda a, b: a + b, [jnp.exp(l - mg) for l in lg])
    le = []
    for j in range(per_group):
        v = lt[n_groups + j:n_groups + j + 1]
        for g in range(1, n_groups):
            r0 = n_groups + g * per_group + j
            v = jnp.where(g_idx == g, lt[r0:r0 + 1], v)
        le.append(v)
    m1, j1 = _first_max(le)
    m2, j2 = _first_max([jnp.where(j1 == j, -jnp.inf, v) for j, v in enumerate(le)])
    r = jnp.exp(m2 - m1)
    w1 = p_top / (1.0 + r)
    w2 = w1 * r
    return g_idx * per_group + j1, g_idx * per_group + j2, w1, w2


def _route_tile(x, sh2, sc2, gpre, wrt, brt, h_ref, meta_ref, metat_ref, cnt_ref, n_groups, per_group):
    tm = x.shape[0]
    h = (_rms(x) * gpre) * (1.0 + sc2) + sh2
    hb = h.astype(_BF16)
    h_ref[0] = hb
    lt = _dot_nt(wrt, hb) + brt
    e1, e2, w1, w2 = _route(lt, n_groups, per_group)

    sub = lax.broadcasted_iota(jnp.int32, (LANES, tm), 0)
    ind = jnp.where((sub == e1) | (sub == e2), 1.0, 0.0)
    earlier = (lax.broadcasted_iota(jnp.int32, (tm, tm), 0)
               < lax.broadcasted_iota(jnp.int32, (tm, tm), 1))
    rank = _dot(ind.astype(_BF16), jnp.where(earlier, 1.0, 0.0).astype(_BF16))
    cnt = jnp.sum(ind, axis=1, keepdims=True)
    chunks = jnp.floor((cnt + (CHUNK_ROWS - 1)) * (1.0 / CHUNK_ROWS))
    below = (lax.broadcasted_iota(jnp.int32, (LANES, LANES), 1)
             < lax.broadcasted_iota(jnp.int32, (LANES, LANES), 0))
    first_chunk = _dot(jnp.where(below, 1.0, 0.0).astype(_BF16),
                       jnp.broadcast_to(chunks, (LANES, LANES)).astype(_BF16))
    dest = first_chunk[:, 0:1] * CHUNK_ROWS + rank
    n_rows = -(-n_groups * per_group // 8) * 8
    sub_e = lax.broadcasted_iota(jnp.int32, (n_rows, tm), 0)
    d1 = jnp.sum(jnp.where(sub_e == e1, dest[:n_rows], 0.0), axis=0, keepdims=True)
    d2 = jnp.sum(jnp.where(sub_e == e2, dest[:n_rows], 0.0), axis=0, keepdims=True)
    row8 = lax.broadcasted_iota(jnp.int32, (8, tm), 0)
    metat = jnp.where(row8 == 0, d1, jnp.where(row8 == 1, d2,
                      jnp.where(row8 == 2, w1, jnp.where(row8 == 3, w2, 0.0))))
    metat_ref[0] = metat
    meta_ref[0] = jnp.concatenate([metat, jnp.zeros((LANES - 8, tm), _F32)], axis=0).T
    cnt_ref[0] = jnp.broadcast_to(cnt, (LANES, LANES)).T[0:8].astype(jnp.int32)


def _chunk_loop(n, fn):
    def body(c, carry):
        fn(c)
        return carry
    lax.fori_loop(0, n, body, 0)


def _wait_chunks(n, max_chunks, copy_of_chunks):
    bit = 1
    while bit * 2 <= max_chunks:
        bit *= 2
    while bit >= 1:
        @pl.when((n & bit) != 0)
        def _(bit=bit):
            copy_of_chunks(bit).wait()
        bit //= 2


def _dispatch_kernel(nch_ref, cmap_ref, tbase_ref, ntail_ref, nu_ref, metat_ref, h_ref, xs_ref,
                     loc_ref, zero_ref, sem, zsem, *, max_chunks):
    i = pl.program_id(0)
    n_steps = pl.num_programs(0)
    g = h_ref.shape[0]
    lrows, tm = loc_ref.shape[1], h_ref.shape[1]
    buf = lambda step, j: (step % 2) * g + j

    def chunk_copy(slot_, c, dst_chunk):
        return pltpu.make_async_copy(
            loc_ref.at[slot_, pl.ds(pl.multiple_of(c * CHUNK_ROWS, CHUNK_ROWS), CHUNK_ROWS)],
            xs_ref.at[pl.ds(pl.multiple_of(dst_chunk * CHUNK_ROWS, CHUNK_ROWS), CHUNK_ROWS)],
            sem.at[slot_])

    def zero_copy(dst_chunk):
        return pltpu.make_async_copy(
            zero_ref.at[pl.ds(0, CHUNK_ROWS)],
            xs_ref.at[pl.ds(pl.multiple_of(dst_chunk * CHUNK_ROWS, CHUNK_ROWS), CHUNK_ROWS)],
            zsem.at[0])

    def zero_tile_copy(tile):
        return pltpu.make_async_copy(
            zero_ref,
            xs_ref.at[pl.ds(pl.multiple_of(tile * GMM_ROW_TILE, GMM_ROW_TILE), GMM_ROW_TILE)],
            zsem.at[1])

    @pl.when(i == 0)
    def _():
        zero_ref[...] = jnp.zeros(zero_ref.shape, zero_ref.dtype)
        n_unused = xs_ref.shape[0] // GMM_ROW_TILE - nu_ref[0]
        for e in range(tbase_ref.shape[0]):
            _chunk_loop(ntail_ref[e], lambda j, e=e: zero_copy(tbase_ref[e] + j).start())
        _chunk_loop(n_unused, lambda j: zero_tile_copy(nu_ref[0] + j).start())
        for e in range(tbase_ref.shape[0]):
            _chunk_loop(ntail_ref[e], lambda j: zero_copy(0).wait())
        _chunk_loop(n_unused, lambda j: zero_tile_copy(0).wait())

    def wait_slot(slot_, n_):
        _wait_chunks(n_, max_chunks, lambda k: pltpu.make_async_copy(
            loc_ref.at[slot_, pl.ds(0, k * CHUNK_ROWS)], xs_ref.at[pl.ds(0, k * CHUNK_ROWS)],
            sem.at[slot_]))

    @pl.when(i >= 2)
    def _():
        for j in range(g):
            wait_slot(buf(i, j), nch_ref[(i - 2) * g + j])

    r = lax.broadcasted_iota(jnp.int32, (lrows, tm), 0)
    for j in range(g):
        mt = metat_ref[j]
        d1 = mt[0:1].astype(jnp.int32)
        d2 = mt[1:2].astype(jnp.int32)
        sel = jnp.where((r == d1) | (r == d2), 1.0, 0.0).astype(_BF16)
        loc_ref[buf(i, j)] = _dot(sel, h_ref[j]).astype(_BF16)
    for j in range(g):
        tile = i * g + j
        _chunk_loop(nch_ref[tile], lambda c, j=j, tile=tile: chunk_copy(
            buf(i, j), c, cmap_ref[tile * max_chunks + c]).start())

    @pl.when(i == n_steps - 1)
    def _():
        for j in range(g):
            wait_slot(buf(i, j), nch_ref[i * g + j])

        @pl.when(i >= 1)
        def _():
            for j in range(g):
                wait_slot(buf(i - 1, j), nch_ref[(i - 1) * g + j])


def _experts_kernel(te_ref, tf_ref, nu_ref, xs_ref, w1_ref, w3_ref, w2_ref, ys_ref,
                    wb1_ref, wb3_ref, wb2_ref):
    i = pl.program_id(0)

    @pl.when(i < nu_ref[0])
    def _():
        @pl.when(tf_ref[i] == 1)
        def _():
            wb1_ref[...] = w1_ref[0].astype(_BF16)
            wb3_ref[...] = w3_ref[0].astype(_BF16)
            wb2_ref[...] = w2_ref[0].astype(_BF16)

        xs = xs_ref[...]
        a = _silu(_dot(xs, wb1_ref[...])) * _dot(xs, wb3_ref[...])
        ys_ref[...] = _dot(a.astype(_BF16), wb2_ref[...]).astype(_BF16)

    @pl.when(i >= nu_ref[0])
    def _():
        ys_ref[...] = jnp.zeros(ys_ref.shape, ys_ref.dtype)


def _combine_kernel(nch_ref, cmap_ref, x_ref, ada_ref, gpost_ref, meta_ref, ys_ref, o_ref,
                    loc_ref, sem, *, max_chunks):
    i = pl.program_id(0)
    n_steps = pl.num_programs(0)
    g = x_ref.shape[0]
    lrows, tm = loc_ref.shape[1], x_ref.shape[1]
    buf = lambda step, j: (step % 2) * g + j

    def chunk_copy(tile, slot_, c):
        src_chunk = cmap_ref[tile * max_chunks + c]
        return pltpu.make_async_copy(
            ys_ref.at[pl.ds(pl.multiple_of(src_chunk * CHUNK_ROWS, CHUNK_ROWS), CHUNK_ROWS)],
            loc_ref.at[slot_, pl.ds(pl.multiple_of(c * CHUNK_ROWS, CHUNK_ROWS), CHUNK_ROWS)],
            sem.at[slot_])

    def fetch(step):
        for j in range(g):
            tile = step * g + j
            _chunk_loop(nch_ref[tile], lambda c, j=j, tile=tile: chunk_copy(tile, buf(step, j), c).start())

    @pl.when(i == 0)
    def _():
        loc_ref[...] = jnp.zeros(loc_ref.shape, loc_ref.dtype)
        fetch(0)

    @pl.when(i + 1 < n_steps)
    def _():
        fetch(i + 1)

    for j in range(g):
        _wait_chunks(nch_ref[i * g + j], max_chunks, lambda k, j=j: pltpu.make_async_copy(
            ys_ref.at[pl.ds(0, k * CHUNK_ROWS)], loc_ref.at[buf(i, j), pl.ds(0, k * CHUNK_ROWS)],
            sem.at[buf(i, j)]))

    r = lax.broadcasted_iota(jnp.int32, (tm, lrows), 1)
    gt2 = ada_ref[0][5:6]
    for j in range(g):
        meta = meta_ref[j]
        d1 = meta[:, 0:1].astype(jnp.int32)
        d2 = meta[:, 1:2].astype(jnp.int32)
        w1, w2 = meta[:, 2:3], meta[:, 3:4]
        ys = loc_ref[buf(i, j)]
        y = (w1 * _dot(jnp.where(r == d1, 1.0, 0.0).astype(_BF16), ys)
             + w2 * _dot(jnp.where(r == d2, 1.0, 0.0).astype(_BF16), ys))
        o_ref[j] = x_ref[j] + gt2 * (_rms(y) * gpost_ref[...])


def _moe_plan(cnt, max_chunks, n_gmm_tiles):
    per_tile = GMM_ROW_TILE // CHUNK_ROWS
    pc = (cnt + (CHUNK_ROWS - 1)) // CHUNK_ROWS
    local_first = jnp.cumsum(pc, axis=1) - pc
    nch = jnp.sum(pc, axis=1)
    e_chunks = jnp.sum(pc, axis=0)
    e_region = ((e_chunks + per_tile - 1) // per_tile) * per_tile
    e_first = jnp.cumsum(e_region) - e_region
    seg_first = e_first[None, :] + jnp.cumsum(pc, axis=0) - pc
    c = jnp.arange(max_chunks, dtype=jnp.int32)[None, :, None]
    inside = (c >= local_first[:, None, :]) & (c < (local_first + pc)[:, None, :])
    cmap = jnp.sum(jnp.where(inside, seg_first[:, None, :] + c - local_first[:, None, :], 0), axis=-1)
    n_used = jnp.sum(e_region) // per_tile
    t = jnp.arange(n_gmm_tiles, dtype=jnp.int32)
    t_used = jnp.minimum(t, n_used - 1)
    tile_e = jnp.sum(t_used[:, None] * per_tile >= (e_first + e_region)[None, :], axis=1)
    tile_first = jnp.concatenate([jnp.ones((1,), jnp.int32),
                                  (tile_e[1:] != tile_e[:-1]).astype(jnp.int32)])
    i32 = lambda a: a.astype(jnp.int32)
    return dict(nch=i32(nch), cmap=i32(cmap.reshape(-1)), tail_first=i32(e_first + e_chunks),
                n_tail=i32(e_region - e_chunks), tile_e=i32(tile_e), tile_first=tile_first,
                n_used=i32(n_used.reshape(1)))


def _moe(x, h2, meta, metat, cnt, ada, g_post, n_groups, w1, w3, w2):
    b, s, d = x.shape
    n_exp, _, ff = w1.shape
    tm = h2.shape[1]
    assert s % tm == 0 and tm % LANES == 0 and GMM_ROW_TILE % CHUNK_ROWS == 0
    tiles_per_batch = s // tm
    nt = b * tiles_per_batch
    max_chunks = (TOP_K_IN_GROUP * tm + n_exp * (CHUNK_ROWS - 1)) // CHUNK_ROWS
    lrows = -(-max_chunks * CHUNK_ROWS // LANES) * LANES
    per_tile = GMM_ROW_TILE // CHUNK_ROWS
    n_gmm_tiles = -(-(nt * max_chunks + n_exp * (per_tile - 1)) // per_tile)
    n_rows = n_gmm_tiles * GMM_ROW_TILE

    row = lambda a: a.reshape(1, -1).astype(_F32)
    n_ada = ada.shape[1]
    xt = x.reshape(nt, tm, d)
    g = MOE_TILES_PER_STEP if tiles_per_batch % MOE_TILES_PER_STEP == 0 else 1
    steps_per_batch = tiles_per_batch // g

    plan = _moe_plan(cnt[:, 0, :n_exp], max_chunks, n_gmm_tiles)

    xs = pl.pallas_call(
        functools.partial(_dispatch_kernel, max_chunks=max_chunks),
        grid_spec=pltpu.PrefetchScalarGridSpec(
            num_scalar_prefetch=5,
            grid=(nt // g,),
            in_specs=[
                pl.BlockSpec((g, 8, tm), lambda i, *_: (i, 0, 0)),
                pl.BlockSpec((g, tm, d), lambda i, *_: (i, 0, 0)),
            ],
            out_specs=pl.BlockSpec(memory_space=pl.ANY),
            scratch_shapes=[
                pltpu.VMEM((2 * g, lrows, d), _BF16),
                pltpu.VMEM((GMM_ROW_TILE, d), _BF16),
                pltpu.SemaphoreType.DMA((2 * g,)),
                pltpu.SemaphoreType.DMA((2,)),
            ],
        ),
        out_shape=jax.ShapeDtypeStruct((n_rows, d), _BF16),
        compiler_params=pltpu.CompilerParams(dimension_semantics=("arbitrary",)),
        name="dispatch",
    )(plan["nch"], plan["cmap"], plan["tail_first"], plan["n_tail"], plan["n_used"], metat, h2)

    ys = pl.pallas_call(
        _experts_kernel,
        grid_spec=pltpu.PrefetchScalarGridSpec(
            num_scalar_prefetch=3,
            grid=(n_gmm_tiles,),
            in_specs=[
                pl.BlockSpec((GMM_ROW_TILE, d), lambda i, te, tf, nu: (jnp.minimum(i, nu[0] - 1), 0)),
                pl.BlockSpec((1, d, ff), lambda i, te, tf, nu: (te[i], 0, 0)),
                pl.BlockSpec((1, d, ff), lambda i, te, tf, nu: (te[i], 0, 0)),
                pl.BlockSpec((1, ff, d), lambda i, te, tf, nu: (te[i], 0, 0)),
            ],
            out_specs=pl.BlockSpec((GMM_ROW_TILE, d), lambda i, te, tf, nu: (i, 0)),
            scratch_shapes=[
                pltpu.VMEM((d, ff), _BF16),
                pltpu.VMEM((d, ff), _BF16),
                pltpu.VMEM((ff, d), _BF16),
            ],
        ),
        out_shape=jax.ShapeDtypeStruct((n_rows, d), _BF16),
        compiler_params=pltpu.CompilerParams(dimension_semantics=("arbitrary",),
                                             vmem_limit_bytes=VMEM_LIMIT_BYTES),
        name="experts",
    )(plan["tile_e"], plan["tile_first"], plan["n_used"], xs, w1, w3, w2)

    out = pl.pallas_call(
        functools.partial(_combine_kernel, max_chunks=max_chunks),
        grid_spec=pltpu.PrefetchScalarGridSpec(
            num_scalar_prefetch=2,
            grid=(nt // g,),
            in_specs=[
                pl.BlockSpec((g, tm, d), lambda i, *_: (i, 0, 0)),
                pl.BlockSpec((1, n_ada, d), lambda i, *_: (i // steps_per_batch, 0, 0)),
                pl.BlockSpec((1, d), lambda i, *_: (0, 0)),
                pl.BlockSpec((g, tm, LANES), lambda i, *_: (i, 0, 0)),
                pl.BlockSpec(memory_space=pl.ANY),
            ],
            out_specs=pl.BlockSpec((g, tm, d), lambda i, *_: (i, 0, 0)),
            scratch_shapes=[
                pltpu.VMEM((2 * g, lrows, d), _BF16),
                pltpu.SemaphoreType.DMA((2 * g,)),
            ],
        ),
        out_shape=jax.ShapeDtypeStruct((nt, tm, d), _F32),
        compiler_params=pltpu.CompilerParams(dimension_semantics=("arbitrary",)),
        name="combine",
    )(plan["nch"], plan["cmap"], xt, ada, row(g_post), meta, ys)
    return out.reshape(b, s, d)


def kernel(x, c, w_ada, b_ada, g_pre_mix, g_post_mix, w_in, ln_v_g, ln_v_b, w_spatial, b_spatial,
           lb_logits, g_hgrn_norm, w_out, g_pre_ffn, g_post_ffn, w_router_group, b_router_group,
           w_router_expert, b_router_expert, w1, w3, w2):
    depth = w_in.shape[0]
    b, s, d = x.shape
    lb_all = _lower_bounds(lb_logits)
    for layer in range(depth):
        ada = _ada(c, w_ada[layer], b_ada[layer]).reshape(b, 6, d)
        x, h2, meta, metat, cnt = _mixer(
            x, ada, g_pre_mix[layer], g_post_mix[layer], w_in[layer], ln_v_g[layer], ln_v_b[layer],
            w_spatial[layer], b_spatial[layer], lb_all[layer], g_hgrn_norm[layer], w_out[layer],
            g_pre_ffn[layer], w_router_group[layer], b_router_group[layer],
            w_router_expert[layer], b_router_expert[layer])
        x = _moe(x, h2, meta, metat, cnt, ada, g_post_ffn[layer], w_router_group.shape[-1],
                 w1[layer], w3[layer], w2[layer])
    return x
```

```python
import functools

import jax
import jax.numpy as jnp
from jax import lax
from jax.experimental import pallas as pl
from jax.experimental.pallas import tpu as pltpu

RMS_EPS = 1e-6
LN_EPS = 1e-5
HGRN_HEAD_DIM = 128
HGRN_CHUNK = 128
N_IN_SLICES = 8
TOP_K_IN_GROUP = 2
LANES = 128
MIXER_SEQ_TILE = 256
PROJ_COLS = 256
CHUNK_ROWS = 16
GMM_ROW_TILE = 1024
MOE_TILES_PER_STEP = 2
VMEM_LIMIT_BYTES = 56 * 1024 * 1024

_F32 = jnp.float32
_BF16 = jnp.bfloat16


def _sigmoid(x):
    return 0.5 * (jnp.tanh(0.5 * x) + 1.0)


def _silu(x):
    h = 0.5 * x
    return h + h * jnp.tanh(h)


def _gelu_tanh(x):
    c = 0.7978845608028654
    h = 0.5 * x
    return h + h * jnp.tanh(x * (c + (0.044715 * c) * (x * x)))


def _rms(x):
    return x * lax.rsqrt(jnp.mean(x * x, axis=-1, keepdims=True) + RMS_EPS)


def _dot(a, b):
    return jnp.dot(a, b, preferred_element_type=_F32)


def _dot_nt(a, b):
    return lax.dot_general(a, b, (((1,), (1,)), ((), ())), preferred_element_type=_F32)


def _dot_tn(a, b):
    return lax.dot_general(a, b, (((0,), (0,)), ((), ())), preferred_element_type=_F32)


def _split3(x):
    hi = x.astype(_BF16)
    r = x - hi.astype(_F32)
    mid = r.astype(_BF16)
    lo = (r - mid.astype(_F32)).astype(_BF16)
    return hi, mid, lo


def _ada_kernel(c_ref, w_ref, b_ref, o_ref):
    s = _silu(c_ref[...])
    w = w_ref[...]
    acc = jnp.zeros(o_ref.shape, _F32)
    for sp in _split3(s):
        for wp in _split3(w)[:2]:
            acc = acc + _dot(sp, wp)
    o_ref[...] = acc + b_ref[...]


def _ada(c, w_ada, b_ada):
    b, d = c.shape
    n = w_ada.shape[1]
    tn = d
    return pl.pallas_call(
        _ada_kernel,
        grid=(n // tn,),
        in_specs=[
            pl.BlockSpec((b, d), lambda j: (0, 0)),
            pl.BlockSpec((d, tn), lambda j: (0, j)),
            pl.BlockSpec((1, tn), lambda j: (0, j)),
        ],
        out_specs=pl.BlockSpec((b, tn), lambda j: (0, j)),
        out_shape=jax.ShapeDtypeStruct((b, n), _F32),
        name="ada",
    )(c, w_ada, b_ada.reshape(1, n))


def _lb_kernel(l_ref, o_ref):
    l = l_ref[...]
    e = jnp.exp(l - jnp.max(l, axis=0, keepdims=True))
    p = e / jnp.sum(e, axis=0, keepdims=True)
    rows = [p[0:1]]
    for i in range(1, p.shape[0]):
        rows.append(rows[-1] + p[i:i + 1])
    for i, r in enumerate(rows):
        o_ref[i:i + 1, :] = r


def _lower_bounds(lb_logits):
    return pl.pallas_call(
        _lb_kernel,
        out_shape=jax.ShapeDtypeStruct(lb_logits.shape, _F32),
        name="lower_bounds",
    )(lb_logits.astype(_F32))


def _mixer_kernel(x_ref, ada_ref, adap_ref, gpre_ref, gpost_ref, win_ref, lng_ref, lnb_ref, wsp_ref,
                  bspt_ref, lb_ref, ghg_ref, wout_ref, gpre2_ref, wrt_ref, brt_ref,
                  o_ref, h2_ref, meta_ref, metat_ref, cnt_ref, st_ref, ya_ref, ob_ref, x1p_ref,
                  act_ref, iv_ref, vn_ref,
                  *, n_route_groups, per_group, tiles_per_batch):
    ts, d = x_ref.shape[1], x_ref.shape[2]
    w = wout_ref.shape[0]
    n_groups, gchunk = wsp_ref.shape[0], wsp_ref.shape[1]
    gdim = w // n_groups
    n_heads = w // HGRN_HEAD_DIM
    c = HGRN_CHUNK
    t = pl.program_id(0)
    is_tile = t < pl.num_programs(0) - 1

    @pl.when(t % tiles_per_batch == 0)
    def _():
        st_ref[...] = jnp.zeros(st_ref.shape, _F32)

    @pl.when(t == 0)
    def _():
        x1p_ref[...] = jnp.zeros(x1p_ref.shape, _F32)

    x1_prev = x1p_ref[...]
    adap = adap_ref[0]
    _route_tile(x1_prev, adap[3:4], adap[4:5], gpre2_ref[...], wrt_ref[...], brt_ref[...],
                h2_ref, meta_ref, metat_ref, cnt_ref, n_route_groups, per_group)

    x = x_ref[0]
    ada = ada_ref[0]
    sh1, sc1, gt1 = ada[0:1], ada[1:2], ada[2:3]
    h = (_rms(x) * gpre_ref[...]) * (1.0 + sc1) + sh1
    h_bf = h.astype(_BF16)

    lb = lb_ref[...]
    f_mid, f_half = 0.5 * (1.0 + lb), 0.5 * (1.0 - lb)
    (s_v, s_u, s_q, s_f, s_i, s_og, s_ga, s_gb) = (1, 0, 2, 3, 4, 5, 6, 7)
    (a_u, a_v, a_q, a_lf, a_k, a_og, a_ga, a_gb) = range(8)

    def activate(j, cols, r):
        if j == s_u:
            act_ref[a_u, :, cols] = _gelu_tanh(r)
        elif j == s_v:
            act_ref[a_v, :, cols] = _gelu_tanh(r)
        elif j == s_q:
            act_ref[a_q, :, cols] = _silu(r)
        elif j == s_f:
            f = f_mid[:, cols] + f_half[:, cols] * jnp.tanh(0.5 * r)
            act_ref[a_lf, :, cols] = jnp.log(f)
            act_ref[a_k, :, cols] = 1.0 - f
        elif j == s_i:
            iv_ref[:, cols] = r.astype(_BF16)
        elif j == s_og:
            act_ref[a_og, :, cols] = _silu(r)
        elif j == s_ga:
            act_ref[a_ga, :, cols] = _sigmoid(r)
        else:
            act_ref[a_gb, :, cols] = _sigmoid(r)

    def layer_norm_v():
        v = act_ref[a_v]
        mu = jnp.mean(v, axis=-1, keepdims=True)
        vc = v - mu
        var = jnp.mean(vc * vc, axis=-1, keepdims=True)
        vn_ref[...] = ((vc * lax.rsqrt(var + LN_EPS)) * lng_ref[...] + lnb_ref[...]).astype(_BF16)

    order = (s_v, s_u, s_q, s_f, s_i, s_og, s_ga, s_gb)
    pending = None
    for pos, j in enumerate(order):
        for cb in range(w // PROJ_COLS):
            cols = slice(cb * PROJ_COLS, (cb + 1) * PROJ_COLS)
            r = _dot(h_bf, win_ref[:, j * w + cb * PROJ_COLS:j * w + (cb + 1) * PROJ_COLS])
            if pending is not None:
                pending()
            pending = functools.partial(activate, j, cols, r)
            if pos == 1 and cb == 0:
                layer_norm_v()
    pending()

    u = act_ref[a_u]
    vn = vn_ref[...]
    tri_g = (lax.broadcasted_iota(jnp.int32, (gchunk, gchunk), 0)
             >= lax.broadcasted_iota(jnp.int32, (gchunk, gchunk), 1))
    bspt = bspt_ref[...]
    n_gchunks = ts // gchunk
    for g in range(n_groups):
        wc = jnp.where(tri_g, wsp_ref[g], 0.0).astype(_BF16)
        cols = slice(g * gdim, (g + 1) * gdim)
        vg = jnp.concatenate([vn[n * gchunk:(n + 1) * gchunk, cols] for n in range(n_gchunks)], axis=1)
        zv = _dot(wc, vg)
        for n in range(n_gchunks):
            rows = slice(n * gchunk, (n + 1) * gchunk)
            ya_ref[rows, cols] = u[rows, cols] * (zv[:, n * gdim:(n + 1) * gdim] + bspt[:, g:g + 1])

    q = act_ref[a_q]
    lf = act_ref[a_lf]
    k = act_ref[a_k]
    iv = iv_ref[...]
    r_i = lax.broadcasted_iota(jnp.int32, (ts, ts), 0)
    c_i = lax.broadcasted_iota(jnp.int32, (ts, ts), 1)
    ltri = jnp.where((r_i // c == c_i // c) & (c_i <= r_i), 1.0, 0.0).astype(_BF16)
    bcum = jnp.zeros((ts, w), _F32)
    for part in _split3(lf):
        bcum = bcum + _dot(ltri, part)
    hb = c // 2
    row_h = lax.broadcasted_iota(jnp.int32, (hb, c), 0)
    lane_h = lax.broadcasted_iota(jnp.int32, (hb, c), 1)
    top_mask = lane_h <= row_h
    left = lane_h < hb
    bot_mask = lane_h - hb <= row_h
    pair = 2 * HGRN_HEAD_DIM
    pr = lax.broadcasted_iota(jnp.int32, (pair, pair), 0)
    pc_i = lax.broadcasted_iota(jnp.int32, (pair, pair), 1)
    same_head = (pr < HGRN_HEAD_DIM) == (pc_i < HGRN_HEAD_DIM)
    up_rows = lax.broadcasted_iota(jnp.int32, (c, pair), 1) < HGRN_HEAD_DIM
    for n in range(ts // c):
        rows = slice(n * c, (n + 1) * c)
        bc = bcum[rows]
        b_a, b_m, b_b, b_l = (bc[hb // 2 - 1:hb // 2], bc[hb - 1:hb],
                              bc[hb + hb // 2 - 1:hb + hb // 2], bc[c - 1:c])
        stack = lambda ra, rb: jnp.concatenate(
            [jnp.broadcast_to(ra, (hb, w)), jnp.broadcast_to(rb, (hb, w))], axis=0)
        ref = stack(b_a, b_b)
        qx = q[rows] * jnp.exp(bc - ref)
        kx = k[rows] * jnp.exp(ref - bc)
        qd = (qx * stack(jnp.exp(b_a), jnp.exp(b_b))).astype(_BF16)
        kd = (kx * stack(jnp.exp(b_l - b_a), jnp.exp(b_l - b_b))).astype(_BF16)
        q_off = qx[hb:] * jnp.exp(b_b - b_m)
        k_off = kx[:hb] * jnp.exp(b_m - b_a)
        q3 = jnp.concatenate([qx, q_off], axis=0).astype(_BF16)
        k3 = jnp.concatenate([kx, k_off, jnp.zeros((hb, w), _F32)], axis=0).astype(_BF16)
        dec = jnp.exp(b_l)
        iv_c = iv[rows]
        for p in range(n_heads // 2):
            pcols = slice(p * pair, (p + 1) * pair)
            a_heads = []
            for hd in (2 * p, 2 * p + 1):
                cols = slice(hd * HGRN_HEAD_DIM, (hd + 1) * HGRN_HEAD_DIM)
                sc = _dot_nt(q3[:, cols], k3[:, cols])
                top = jnp.where(top_mask, sc[0:hb, 0:c], 0.0)
                bot = jnp.where(left, sc[2 * hb:3 * hb, c:2 * c],
                                jnp.where(bot_mask, sc[hb:2 * hb, 0:c], 0.0))
                a_heads.append(jnp.concatenate([top, bot], axis=0))
            a_pair = jnp.concatenate(a_heads, axis=1).astype(_BF16)
            iv_p = iv_c[:, pcols]
            iv_blk = jnp.concatenate([jnp.where(up_rows, iv_p, 0.0).astype(_BF16),
                                      jnp.where(up_rows, 0.0, iv_p).astype(_BF16)], axis=0)
            st = st_ref[p]
            o = _dot(a_pair, iv_blk) + _dot_nt(qd[:, pcols], st.astype(_BF16))
            upd = _dot_tn(iv_p, kd[:, pcols])
            st_ref[p] = st * dec[:, pcols] + jnp.where(same_head, upd, 0.0)
            for j in range(2):
                hcols = slice(j * HGRN_HEAD_DIM, (j + 1) * HGRN_HEAD_DIM)
                ob_ref[rows, p * pair + j * HGRN_HEAD_DIM:p * pair + (j + 1) * HGRN_HEAD_DIM] = _rms(o[:, hcols])
    yb = (ob_ref[...] * ghg_ref[...]) * act_ref[a_og]

    y = act_ref[a_ga] * ya_ref[...] + act_ref[a_gb] * yb
    out = _dot(y.astype(_BF16), wout_ref[...])
    x1 = jnp.where(is_tile, x + gt1 * (_rms(out) * gpost_ref[...]), x1_prev)
    o_ref[0] = x1
    x1p_ref[...] = x1


def _mixer(x, ada, g_pre, g_post, w_in, ln_g, ln_b, w_spatial, b_spatial, lb, g_hgrn, w_out,
           g_pre_ffn, w_rg, b_rg, w_re, b_re):
    b, s, d = x.shape
    w = w_out.shape[0]
    n_groups, gchunk, _ = w_spatial.shape
    ts = min(MIXER_SEQ_TILE, s)
    assert s % ts == 0 and ts % gchunk == 0 and ts % HGRN_CHUNK == 0 and ts % LANES == 0
    n_rg, n_exp = w_rg.shape[1], w_re.shape[1]
    assert n_rg + n_exp <= LANES
    pad = LANES - n_rg - n_exp
    wrt = jnp.concatenate([w_rg, w_re, jnp.zeros((d, pad), _F32)], axis=1).T.astype(_BF16)
    brt = jnp.concatenate([b_rg, b_re, jnp.zeros((pad,), _F32)]).reshape(LANES, 1).astype(_F32)
    tiles = s // ts
    nt = b * tiles
    cur = lambda t: jnp.minimum(t, nt - 1)
    prev = lambda t: jnp.maximum(t - 1, 0)
    cur3 = lambda t: (cur(t), 0, 0)
    prev3 = lambda t: (prev(t), 0, 0)
    assert w % HGRN_HEAD_DIM == 0 and w // n_groups == LANES and w_in.shape == (d, N_IN_SLICES * w)
    assert w % PROJ_COLS == 0
    n_heads = w // HGRN_HEAD_DIM
    assert n_heads % 2 == 0 and HGRN_HEAD_DIM == LANES
    row = lambda a: a.reshape(1, -1).astype(_F32)
    const2 = lambda t: (0, 0)
    const3 = lambda t: (0, 0, 0)
    single = dict(pipeline_mode=pl.Buffered(1))
    x1, h2, meta, metat, cnt = pl.pallas_call(
        functools.partial(_mixer_kernel, n_route_groups=n_rg, per_group=n_exp // n_rg,
                          tiles_per_batch=tiles),
        grid=(nt + 1,),
        in_specs=[
            pl.BlockSpec((1, ts, d), cur3),
            pl.BlockSpec((1, ada.shape[1], d), lambda t: (cur(t) // tiles, 0, 0)),
            pl.BlockSpec((1, ada.shape[1], d), lambda t: (prev(t) // tiles, 0, 0)),
            pl.BlockSpec((1, d), const2),
            pl.BlockSpec((1, d), const2),
            pl.BlockSpec((d, N_IN_SLICES * w), const2, **single),
            pl.BlockSpec((1, w), const2),
            pl.BlockSpec((1, w), const2),
            pl.BlockSpec((n_groups, gchunk, gchunk), const3, **single),
            pl.BlockSpec((gchunk, n_groups), const2),
            pl.BlockSpec((1, w), const2),
            pl.BlockSpec((1, w), const2),
            pl.BlockSpec((w, d), const2, **single),
            pl.BlockSpec((1, d), const2),
            pl.BlockSpec((LANES, d), const2),
            pl.BlockSpec((LANES, 1), const2),
        ],
        out_specs=[
            pl.BlockSpec((1, ts, d), cur3),
            pl.BlockSpec((1, ts, d), prev3),
            pl.BlockSpec((1, ts, LANES), prev3),
            pl.BlockSpec((1, 8, ts), prev3),
            pl.BlockSpec((1, 8, LANES), prev3),
        ],
        out_shape=[
            jax.ShapeDtypeStruct((nt, ts, d), _F32),
            jax.ShapeDtypeStruct((nt, ts, d), _BF16),
            jax.ShapeDtypeStruct((nt, ts, LANES), _F32),
            jax.ShapeDtypeStruct((nt, 8, ts), _F32),
            jax.ShapeDtypeStruct((nt, 8, LANES), jnp.int32),
        ],
        scratch_shapes=[
            pltpu.VMEM((n_heads // 2, 2 * HGRN_HEAD_DIM, 2 * HGRN_HEAD_DIM), _F32),
            pltpu.VMEM((ts, w), _F32),
            pltpu.VMEM((ts, w), _F32),
            pltpu.VMEM((ts, d), _F32),
            pltpu.VMEM((N_IN_SLICES, ts, w), _F32),
            pltpu.VMEM((ts, w), _BF16),
            pltpu.VMEM((ts, w), _BF16),
        ],
        compiler_params=pltpu.CompilerParams(
            dimension_semantics=("arbitrary",),
            vmem_limit_bytes=VMEM_LIMIT_BYTES),
        name="mixer",
    )(x.reshape(nt, ts, d), ada, ada, row(g_pre), row(g_post), w_in.astype(_BF16), row(ln_g),
      row(ln_b), w_spatial.astype(_F32), b_spatial.T.astype(_F32), row(lb), row(g_hgrn),
      w_out.astype(_BF16), row(g_pre_ffn), wrt, brt)
    return x1.reshape(b, s, d), h2, meta, metat, cnt


def _first_max(vals):
    m = functools.reduce(jnp.maximum, vals)
    idx = jnp.full(m.shape, len(vals) - 1, jnp.int32)
    for j in range(len(vals) - 2, -1, -1):
        idx = jnp.where(vals[j] == m, j, idx)
    return m, idx


def _route(lt, n_groups, per_group):
    lg = [lt[g:g + 1] for g in range(n_groups)]
    mg, g_idx = _first_max(lg)
    p_top = 1.0 / functools.reduce(lambda a, b: a + b, [jnp.exp(l - mg) for l in lg])
    le = []
    for j in range(per_group):
        v = lt[n_groups + j:n_groups + j + 1]
        for g in range(1, n_groups):
            r0 = n_groups + g * per_group + j
            v = jnp.where(g_idx == g, lt[r0:r0 + 1], v)
        le.append(v)
    m1, j1 = _first_max(le)
    m2, j2 = _first_max([jnp.where(j1 == j, -jnp.inf, v) for j, v in enumerate(le)])
    r = jnp.exp(m2 - m1)
    w1 = p_top / (1.0 + r)
    w2 = w1 * r
    return g_idx * per_group + j1, g_idx * per_group + j2, w1, w2


def _route_tile(x, sh2, sc2, gpre, wrt, brt, h_ref, meta_ref, metat_ref, cnt_ref, n_groups, per_group):
    tm = x.shape[0]
    h = (_rms(x) * gpre) * (1.0 + sc2) + sh2
    hb = h.astype(_BF16)
    h_ref[0] = hb
    lt = _dot_nt(wrt, hb) + brt
    e1, e2, w1, w2 = _route(lt, n_groups, per_group)

    sub = lax.broadcasted_iota(jnp.int32, (LANES, tm), 0)
    ind = jnp.where((sub == e1) | (sub == e2), 1.0, 0.0)
    earlier = (lax.broadcasted_iota(jnp.int32, (tm, tm), 0)
               < lax.broadcasted_iota(jnp.int32, (tm, tm), 1))
    rank = _dot(ind.astype(_BF16), jnp.where(earlier, 1.0, 0.0).astype(_BF16))
    cnt = jnp.sum(ind, axis=1, keepdims=True)
    chunks = jnp.floor((cnt + (CHUNK_ROWS - 1)) * (1.0 / CHUNK_ROWS))
    below = (lax.broadcasted_iota(jnp.int32, (LANES, LANES), 1)
             < lax.broadcasted_iota(jnp.int32, (LANES, LANES), 0))
    first_chunk = _dot(jnp.where(below, 1.0, 0.0).astype(_BF16),
                       jnp.broadcast_to(chunks, (LANES, LANES)).astype(_BF16))
    dest = first_chunk[:, 0:1] * CHUNK_ROWS + rank
    n_rows = -(-n_groups * per_group // 8) * 8
    sub_e = lax.broadcasted_iota(jnp.int32, (n_rows, tm), 0)
    d1 = jnp.sum(jnp.where(sub_e == e1, dest[:n_rows], 0.0), axis=0, keepdims=True)
    d2 = jnp.sum(jnp.where(sub_e == e2, dest[:n_rows], 0.0), axis=0, keepdims=True)
    row8 = lax.broadcasted_iota(jnp.int32, (8, tm), 0)
    metat = jnp.where(row8 == 0, d1, jnp.where(row8 == 1, d2,
                      jnp.where(row8 == 2, w1, jnp.where(row8 == 3, w2, 0.0))))
    metat_ref[0] = metat
    meta_ref[0] = jnp.concatenate([metat, jnp.zeros((LANES - 8, tm), _F32)], axis=0).T
    cnt_ref[0] = jnp.broadcast_to(cnt, (LANES, LANES)).T[0:8].astype(jnp.int32)


def _chunk_loop(n, fn):
    def body(c, carry):
        fn(c)
        return carry
    lax.fori_loop(0, n, body, 0)


def _wait_chunks(n, max_chunks, copy_of_chunks):
    bit = 1
    while bit * 2 <= max_chunks:
        bit *= 2
    while bit >= 1:
        @pl.when((n & bit) != 0)
        def _(bit=bit):
            copy_of_chunks(bit).wait()
        bit //= 2


def _dispatch_kernel(nch_ref, cmap_ref, tbase_ref, ntail_ref, nu_ref, metat_ref, h_ref, xs_ref,
                     loc_ref, zero_ref, sem, zsem, *, max_chunks):
    i = pl.program_id(0)
    n_steps = pl.num_programs(0)
    g = h_ref.shape[0]
    lrows, tm = loc_ref.shape[1], h_ref.shape[1]
    buf = lambda step, j: (step % 2) * g + j

    def chunk_copy(slot_, c, dst_chunk):
        return pltpu.make_async_copy(
            loc_ref.at[slot_, pl.ds(pl.multiple_of(c * CHUNK_ROWS, CHUNK_ROWS), CHUNK_ROWS)],
            xs_ref.at[pl.ds(pl.multiple_of(dst_chunk * CHUNK_ROWS, CHUNK_ROWS), CHUNK_ROWS)],
            sem.at[slot_])

    def zero_copy(dst_chunk):
        return pltpu.make_async_copy(
            zero_ref.at[pl.ds(0, CHUNK_ROWS)],
            xs_ref.at[pl.ds(pl.multiple_of(dst_chunk * CHUNK_ROWS, CHUNK_ROWS), CHUNK_ROWS)],
            zsem.at[0])

    def zero_tile_copy(tile):
        return pltpu.make_async_copy(
            zero_ref,
            xs_ref.at[pl.ds(pl.multiple_of(tile * GMM_ROW_TILE, GMM_ROW_TILE), GMM_ROW_TILE)],
            zsem.at[1])

    @pl.when(i == 0)
    def _():
        zero_ref[...] = jnp.zeros(zero_ref.shape, zero_ref.dtype)
        n_unused = xs_ref.shape[0] // GMM_ROW_TILE - nu_ref[0]
        for e in range(tbase_ref.shape[0]):
            _chunk_loop(ntail_ref[e], lambda j, e=e: zero_copy(tbase_ref[e] + j).start())
        _chunk_loop(n_unused, lambda j: zero_tile_copy(nu_ref[0] + j).start())
        for e in range(tbase_ref.shape[0]):
            _chunk_loop(ntail_ref[e], lambda j: zero_copy(0).wait())
        _chunk_loop(n_unused, lambda j: zero_tile_copy(0).wait())

    def wait_slot(slot_, n_):
        _wait_chunks(n_, max_chunks, lambda k: pltpu.make_async_copy(
            loc_ref.at[slot_, pl.ds(0, k * CHUNK_ROWS)], xs_ref.at[pl.ds(0, k * CHUNK_ROWS)],
            sem.at[slot_]))

    @pl.when(i >= 2)
    def _():
        for j in range(g):
            wait_slot(buf(i, j), nch_ref[(i - 2) * g + j])

    r = lax.broadcasted_iota(jnp.int32, (lrows, tm), 0)
    for j in range(g):
        mt = metat_ref[j]
        d1 = mt[0:1].astype(jnp.int32)
        d2 = mt[1:2].astype(jnp.int32)
        sel = jnp.where((r == d1) | (r == d2), 1.0, 0.0).astype(_BF16)
        loc_ref[buf(i, j)] = _dot(sel, h_ref[j]).astype(_BF16)
    for j in range(g):
        tile = i * g + j
        _chunk_loop(nch_ref[tile], lambda c, j=j, tile=tile: chunk_copy(
            buf(i, j), c, cmap_ref[tile * max_chunks + c]).start())

    @pl.when(i == n_steps - 1)
    def _():
        for j in range(g):
            wait_slot(buf(i, j), nch_ref[i * g + j])

        @pl.when(i >= 1)
        def _():
            for j in range(g):
                wait_slot(buf(i - 1, j), nch_ref[(i - 1) * g + j])


def _experts_kernel(te_ref, tf_ref, nu_ref, xs_ref, w1_ref, w3_ref, w2_ref, ys_ref,
                    wb1_ref, wb3_ref, wb2_ref):
    i = pl.program_id(0)

    @pl.when(i < nu_ref[0])
    def _():
        @pl.when(tf_ref[i] == 1)
        def _():
            wb1_ref[...] = w1_ref[0].astype(_BF16)
            wb3_ref[...] = w3_ref[0].astype(_BF16)
            wb2_ref[...] = w2_ref[0].astype(_BF16)

        xs = xs_ref[...]
        half = wb1_ref.shape[1] // 2
        acc = None
        for hcols in (slice(0, half), slice(half, 2 * half)):
            a = _silu(_dot(xs, wb1_ref[:, hcols])) * _dot(xs, wb3_ref[:, hcols])
            part = _dot(a.astype(_BF16), wb2_ref[hcols, :])
            acc = part if acc is None else acc + part
        ys_ref[...] = acc.astype(_BF16)

    @pl.when(i >= nu_ref[0])
    def _():
        ys_ref[...] = jnp.zeros(ys_ref.shape, ys_ref.dtype)


def _combine_kernel(nch_ref, cmap_ref, x_ref, ada_ref, gpost_ref, meta_ref, ys_ref, o_ref,
                    loc_ref, sem, *, max_chunks):
    i = pl.program_id(0)
    n_steps = pl.num_programs(0)
    g = x_ref.shape[0]
    lrows, tm = loc_ref.shape[1], x_ref.shape[1]
    buf = lambda step, j: (step % 2) * g + j

    def chunk_copy(tile, slot_, c):
        src_chunk = cmap_ref[tile * max_chunks + c]
        return pltpu.make_async_copy(
            ys_ref.at[pl.ds(pl.multiple_of(src_chunk * CHUNK_ROWS, CHUNK_ROWS), CHUNK_ROWS)],
            loc_ref.at[slot_, pl.ds(pl.multiple_of(c * CHUNK_ROWS, CHUNK_ROWS), CHUNK_ROWS)],
            sem.at[slot_])

    def fetch(step):
        for j in range(g):
            tile = step * g + j
            _chunk_loop(nch_ref[tile], lambda c, j=j, tile=tile: chunk_copy(tile, buf(step, j), c).start())

    @pl.when(i == 0)
    def _():
        loc_ref[...] = jnp.zeros(loc_ref.shape, loc_ref.dtype)
        fetch(0)

    @pl.when(i + 1 < n_steps)
    def _():
        fetch(i + 1)

    for j in range(g):
        _wait_chunks(nch_ref[i * g + j], max_chunks, lambda k, j=j: pltpu.make_async_copy(
            ys_ref.at[pl.ds(0, k * CHUNK_ROWS)], loc_ref.at[buf(i, j), pl.ds(0, k * CHUNK_ROWS)],
            sem.at[buf(i, j)]))

    r = lax.broadcasted_iota(jnp.int32, (tm, lrows), 1)
    gt2 = ada_ref[0][5:6]
    for j in range(g):
        meta = meta_ref[j]
        d1 = meta[:, 0:1].astype(jnp.int32)
        d2 = meta[:, 1:2].astype(jnp.int32)
        w1, w2 = meta[:, 2:3], meta[:, 3:4]
        ys = loc_ref[buf(i, j)]
        y = (w1 * _dot(jnp.where(r == d1, 1.0, 0.0).astype(_BF16), ys)
             + w2 * _dot(jnp.where(r == d2, 1.0, 0.0).astype(_BF16), ys))
        o_ref[j] = x_ref[j] + gt2 * (_rms(y) * gpost_ref[...])


def _moe_plan(cnt, max_chunks, n_gmm_tiles):
    per_tile = GMM_ROW_TILE // CHUNK_ROWS
    pc = (cnt + (CHUNK_ROWS - 1)) // CHUNK_ROWS
    local_first = jnp.cumsum(pc, axis=1) - pc
    nch = jnp.sum(pc, axis=1)
    e_chunks = jnp.sum(pc, axis=0)
    e_region = ((e_chunks + per_tile - 1) // per_tile) * per_tile
    e_first = jnp.cumsum(e_region) - e_region
    seg_first = e_first[None, :] + jnp.cumsum(pc, axis=0) - pc
    c = jnp.arange(max_chunks, dtype=jnp.int32)[None, :, None]
    inside = (c >= local_first[:, None, :]) & (c < (local_first + pc)[:, None, :])
    cmap = jnp.sum(jnp.where(inside, seg_first[:, None, :] + c - local_first[:, None, :], 0), axis=-1)
    n_used = jnp.sum(e_region) // per_tile
    t = jnp.arange(n_gmm_tiles, dtype=jnp.int32)
    t_used = jnp.minimum(t, n_used - 1)
    tile_e = jnp.sum(t_used[:, None] * per_tile >= (e_first + e_region)[None, :], axis=1)
    tile_first = jnp.concatenate([jnp.ones((1,), jnp.int32),
                                  (tile_e[1:] != tile_e[:-1]).astype(jnp.int32)])
    i32 = lambda a: a.astype(jnp.int32)
    return dict(nch=i32(nch), cmap=i32(cmap.reshape(-1)), tail_first=i32(e_first + e_chunks),
                n_tail=i32(e_region - e_chunks), tile_e=i32(tile_e), tile_first=tile_first,
                n_used=i32(n_used.reshape(1)))


def _moe(x, h2, meta, metat, cnt, ada, g_post, n_groups, w1, w3, w2):
    b, s, d = x.shape
    n_exp, _, ff = w1.shape
    tm = h2.shape[1]
    assert s % tm == 0 and tm % LANES == 0 and GMM_ROW_TILE % CHUNK_ROWS == 0
    tiles_per_batch = s // tm
    nt = b * tiles_per_batch
    max_chunks = (TOP_K_IN_GROUP * tm + n_exp * (CHUNK_ROWS - 1)) // CHUNK_ROWS
    lrows = -(-max_chunks * CHUNK_ROWS // LANES) * LANES
    per_tile = GMM_ROW_TILE // CHUNK_ROWS
    n_gmm_tiles = -(-(nt * max_chunks + n_exp * (per_tile - 1)) // per_tile)
    n_rows = n_gmm_tiles * GMM_ROW_TILE

    row = lambda a: a.reshape(1, -1).astype(_F32)
    n_ada = ada.shape[1]
    xt = x.reshape(nt, tm, d)
    g = MOE_TILES_PER_STEP if tiles_per_batch % MOE_TILES_PER_STEP == 0 else 1
    steps_per_batch = tiles_per_batch // g

    plan = _moe_plan(cnt[:, 0, :n_exp], max_chunks, n_gmm_tiles)

    xs = pl.pallas_call(
        functools.partial(_dispatch_kernel, max_chunks=max_chunks),
        grid_spec=pltpu.PrefetchScalarGridSpec(
            num_scalar_prefetch=5,
            grid=(nt // g,),
            in_specs=[
                pl.BlockSpec((g, 8, tm), lambda i, *_: (i, 0, 0)),
                pl.BlockSpec((g, tm, d), lambda i, *_: (i, 0, 0)),
            ],
            out_specs=pl.BlockSpec(memory_space=pl.ANY),
            scratch_shapes=[
                pltpu.VMEM((2 * g, lrows, d), _BF16),
                pltpu.VMEM((GMM_ROW_TILE, d), _BF16),
                pltpu.SemaphoreType.DMA((2 * g,)),
                pltpu.SemaphoreType.DMA((2,)),
            ],
        ),
        out_shape=jax.ShapeDtypeStruct((n_rows, d), _BF16),
        compiler_params=pltpu.CompilerParams(dimension_semantics=("arbitrary",)),
        name="dispatch",
    )(plan["nch"], plan["cmap"], plan["tail_first"], plan["n_tail"], plan["n_used"], metat, h2)

    ys = pl.pallas_call(
        _experts_kernel,
        grid_spec=pltpu.PrefetchScalarGridSpec(
            num_scalar_prefetch=3,
            grid=(n_gmm_tiles,),
            in_specs=[
                pl.BlockSpec((GMM_ROW_TILE, d), lambda i, te, tf, nu: (jnp.minimum(i, nu[0] - 1), 0)),
                pl.BlockSpec((1, d, ff), lambda i, te, tf, nu: (te[i], 0, 0)),
                pl.BlockSpec((1, d, ff), lambda i, te, tf, nu: (te[i], 0, 0)),
                pl.BlockSpec((1, ff, d), lambda i, te, tf, nu: (te[i], 0, 0)),
            ],
            out_specs=pl.BlockSpec((GMM_ROW_TILE, d), lambda i, te, tf, nu: (i, 0)),
            scratch_shapes=[
                pltpu.VMEM((d, ff), _BF16),
                pltpu.VMEM((d, ff), _BF16),
                pltpu.VMEM((ff, d), _BF16),
            ],
        ),
        out_shape=jax.ShapeDtypeStruct((n_rows, d), _BF16),
        compiler_params=pltpu.CompilerParams(dimension_semantics=("arbitrary",),
                                             vmem_limit_bytes=VMEM_LIMIT_BYTES),
        name="experts",
    )(plan["tile_e"], plan["tile_first"], plan["n_used"], xs, w1, w3, w2)

    out = pl.pallas_call(
        functools.partial(_combine_kernel, max_chunks=max_chunks),
        grid_spec=pltpu.PrefetchScalarGridSpec(
            num_scalar_prefetch=2,
            grid=(nt // g,),
            in_specs=[
                pl.BlockSpec((g, tm, d), lambda i, *_: (i, 0, 0)),
                pl.BlockSpec((1, n_ada, d), lambda i, *_: (i // steps_per_batch, 0, 0)),
                pl.BlockSpec((1, d), lambda i, *_: (0, 0)),
                pl.BlockSpec((g, tm, LANES), lambda i, *_: (i, 0, 0)),
                pl.BlockSpec(memory_space=pl.ANY),
            ],
            out_specs=pl.BlockSpec((g, tm, d), lambda i, *_: (i, 0, 0)),
            scratch_shapes=[
                pltpu.VMEM((2 * g, lrows, d), _BF16),
                pltpu.SemaphoreType.DMA((2 * g,)),
            ],
        ),
        out_shape=jax.ShapeDtypeStruct((nt, tm, d), _F32),
        compiler_params=pltpu.CompilerParams(dimension_semantics=("arbitrary",)),
        name="combine",
    )(plan["nch"], plan["cmap"], xt, ada, row(g_post), meta, ys)
    return out.reshape(b, s, d)


def kernel(x, c, w_ada, b_ada, g_pre_mix, g_post_mix, w_in, ln_v_g, ln_v_b, w_spatial, b_spatial,
           lb_logits, g_hgrn_norm, w_out, g_pre_ffn, g_post_ffn, w_router_group, b_router_group,
           w_router_expert, b_router_expert, w1, w3, w2):
    depth = w_in.shape[0]
    b, s, d = x.shape
    lb_all = _lower_bounds(lb_logits)
    for layer in range(depth):
        ada = _ada(c, w_ada[layer], b_ada[layer]).reshape(b, 6, d)
        x, h2, meta, metat, cnt = _mixer(
            x, ada, g_pre_mix[layer], g_post_mix[layer], w_in[layer], ln_v_g[layer], ln_v_b[layer],
            w_spatial[layer], b_spatial[layer], lb_all[layer], g_hgrn_norm[layer], w_out[layer],
            g_pre_ffn[layer], w_router_group[layer], b_router_group[layer],
            w_router_expert[layer], b_router_expert[layer])
        x = _moe(x, h2, meta, metat, cnt, ada, g_post_ffn[layer], w_router_group.shape[-1],
                 w1[layer], w3[layer], w2[layer])
    return x
```

```python
import functools

import jax
import jax.numpy as jnp
from jax import lax
from jax.experimental import pallas as pl
from jax.experimental.pallas import tpu as pltpu

RMS_EPS = 1e-6
LN_EPS = 1e-5
HGRN_HEAD_DIM = 128
HGRN_CHUNK = 128
N_IN_SLICES = 8
TOP_K_IN_GROUP = 2
LANES = 128
MIXER_SEQ_TILE = 256
PROJ_COLS = 256
CHUNK_ROWS = 16
GMM_ROW_TILE = 1024
MOE_TILES_PER_STEP = 2
VMEM_LIMIT_BYTES = 56 * 1024 * 1024

_F32 = jnp.float32
_BF16 = jnp.bfloat16


def _sigmoid(x):
    return 0.5 * (jnp.tanh(0.5 * x) + 1.0)


def _silu(x):
    return x * _sigmoid(x)


def _gelu_tanh(x):
    c = 0.7978845608028654
    return 0.5 * x * (1.0 + jnp.tanh(c * (x + 0.044715 * (x * x * x))))


def _rms(x):
    return x * lax.rsqrt(jnp.mean(x * x, axis=-1, keepdims=True) + RMS_EPS)


def _dot(a, b):
    return jnp.dot(a, b, preferred_element_type=_F32)


def _dot_nt(a, b):
    return lax.dot_general(a, b, (((1,), (1,)), ((), ())), preferred_element_type=_F32)


def _dot_tn(a, b):
    return lax.dot_general(a, b, (((0,), (0,)), ((), ())), preferred_element_type=_F32)


def _split3(x):
    hi = x.astype(_BF16)
    r = x - hi.astype(_F32)
    mid = r.astype(_BF16)
    lo = (r - mid.astype(_F32)).astype(_BF16)
    return hi, mid, lo


def _ada_kernel(c_ref, w_ref, b_ref, o_ref):
    s = _silu(c_ref[...])
    w = w_ref[...]
    acc = jnp.zeros(o_ref.shape, _F32)
    for sp in _split3(s):
        for wp in _split3(w)[:2]:
            acc = acc + _dot(sp, wp)
    o_ref[...] = acc + b_ref[...]


def _ada(c, w_ada, b_ada):
    b, d = c.shape
    n = w_ada.shape[1]
    tn = d
    return pl.pallas_call(
        _ada_kernel,
        grid=(n // tn,),
        in_specs=[
            pl.BlockSpec((b, d), lambda j: (0, 0)),
            pl.BlockSpec((d, tn), lambda j: (0, j)),
            pl.BlockSpec((1, tn), lambda j: (0, j)),
        ],
        out_specs=pl.BlockSpec((b, tn), lambda j: (0, j)),
        out_shape=jax.ShapeDtypeStruct((b, n), _F32),
        name="ada",
    )(c, w_ada, b_ada.reshape(1, n))


def _lb_kernel(l_ref, o_ref):
    l = l_ref[...]
    e = jnp.exp(l - jnp.max(l, axis=0, keepdims=True))
    p = e / jnp.sum(e, axis=0, keepdims=True)
    rows = [p[0:1]]
    for i in range(1, p.shape[0]):
        rows.append(rows[-1] + p[i:i + 1])
    for i, r in enumerate(rows):
        o_ref[i:i + 1, :] = r


def _lower_bounds(lb_logits):
    return pl.pallas_call(
        _lb_kernel,
        out_shape=jax.ShapeDtypeStruct(lb_logits.shape, _F32),
        name="lower_bounds",
    )(lb_logits.astype(_F32))


def _mixer_kernel(x_ref, ada_ref, adap_ref, gpre_ref, gpost_ref, win_ref, lng_ref, lnb_ref, wsp_ref,
                  bspt_ref, lb_ref, ghg_ref, wout_ref, gpre2_ref, wrt_ref, brt_ref,
                  o_ref, h2_ref, meta_ref, metat_ref, cnt_ref, st_ref, ya_ref, ob_ref, x1p_ref,
                  act_ref, iv_ref, vn_ref,
                  *, n_route_groups, per_group, tiles_per_batch):
    ts, d = x_ref.shape[1], x_ref.shape[2]
    w = wout_ref.shape[0]
    n_groups, gchunk = wsp_ref.shape[0], wsp_ref.shape[1]
    gdim = w // n_groups
    n_heads = w // HGRN_HEAD_DIM
    c = HGRN_CHUNK
    t = pl.program_id(0)
    is_tile = t < pl.num_programs(0) - 1

    @pl.when(t % tiles_per_batch == 0)
    def _():
        st_ref[...] = jnp.zeros(st_ref.shape, _F32)

    @pl.when(t == 0)
    def _():
        x1p_ref[...] = jnp.zeros(x1p_ref.shape, _F32)

    x1_prev = x1p_ref[...]
    adap = adap_ref[0]
    _route_tile(x1_prev, adap[3:4], adap[4:5], gpre2_ref[...], wrt_ref[...], brt_ref[...],
                h2_ref, meta_ref, metat_ref, cnt_ref, n_route_groups, per_group)

    x = x_ref[0]
    ada = ada_ref[0]
    sh1, sc1, gt1 = ada[0:1], ada[1:2], ada[2:3]
    h = (_rms(x) * gpre_ref[...]) * (1.0 + sc1) + sh1
    h_bf = h.astype(_BF16)

    lb = lb_ref[...]
    (s_v, s_u, s_q, s_f, s_i, s_og, s_ga, s_gb) = (1, 0, 2, 3, 4, 5, 6, 7)
    (a_u, a_v, a_q, a_lf, a_k, a_og, a_ga, a_gb) = range(8)

    def activate(j, cols, r):
        if j == s_u:
            act_ref[a_u, :, cols] = _gelu_tanh(r)
        elif j == s_v:
            act_ref[a_v, :, cols] = _gelu_tanh(r)
        elif j == s_q:
            act_ref[a_q, :, cols] = _silu(r)
        elif j == s_f:
            f = lb[:, cols] + (1.0 - lb[:, cols]) * _sigmoid(r)
            act_ref[a_lf, :, cols] = jnp.log(f)
            act_ref[a_k, :, cols] = 1.0 - f
        elif j == s_i:
            iv_ref[:, cols] = r.astype(_BF16)
        elif j == s_og:
            act_ref[a_og, :, cols] = _silu(r)
        elif j == s_ga:
            act_ref[a_ga, :, cols] = _sigmoid(r)
        else:
            act_ref[a_gb, :, cols] = _sigmoid(r)

    def layer_norm_v():
        v = act_ref[a_v]
        mu = jnp.mean(v, axis=-1, keepdims=True)
        vc = v - mu
        var = jnp.mean(vc * vc, axis=-1, keepdims=True)
        vn_ref[...] = ((vc * lax.rsqrt(var + LN_EPS)) * lng_ref[...] + lnb_ref[...]).astype(_BF16)

    order = (s_v, s_u, s_q, s_f, s_i, s_og, s_ga, s_gb)
    pending = None
    for pos, j in enumerate(order):
        for cb in range(w // PROJ_COLS):
            cols = slice(cb * PROJ_COLS, (cb + 1) * PROJ_COLS)
            r = _dot(h_bf, win_ref[:, j * w + cb * PROJ_COLS:j * w + (cb + 1) * PROJ_COLS])
            if pending is not None:
                pending()
            pending = functools.partial(activate, j, cols, r)
            if pos == 1 and cb == 0:
                layer_norm_v()
    pending()

    u = act_ref[a_u]
    vn = vn_ref[...]
    tri_g = (lax.broadcasted_iota(jnp.int32, (gchunk, gchunk), 0)
             >= lax.broadcasted_iota(jnp.int32, (gchunk, gchunk), 1))
    bspt = bspt_ref[...]
    n_gchunks = ts // gchunk
    for g in range(n_groups):
        wc = jnp.where(tri_g, wsp_ref[g], 0.0).astype(_BF16)
        cols = slice(g * gdim, (g + 1) * gdim)
        vg = jnp.concatenate([vn[n * gchunk:(n + 1) * gchunk, cols] for n in range(n_gchunks)], axis=1)
        zv = _dot(wc, vg)
        for n in range(n_gchunks):
            rows = slice(n * gchunk, (n + 1) * gchunk)
            ya_ref[rows, cols] = u[rows, cols] * (zv[:, n * gdim:(n + 1) * gdim] + bspt[:, g:g + 1])

    q = act_ref[a_q]
    lf = act_ref[a_lf]
    k = act_ref[a_k]
    iv = iv_ref[...]
    r_i = lax.broadcasted_iota(jnp.int32, (ts, ts), 0)
    c_i = lax.broadcasted_iota(jnp.int32, (ts, ts), 1)
    ltri = jnp.where((r_i // c == c_i // c) & (c_i <= r_i), 1.0, 0.0).astype(_BF16)
    bcum = jnp.zeros((ts, w), _F32)
    for part in _split3(lf):
        bcum = bcum + _dot(ltri, part)
    hb = c // 2
    row_h = lax.broadcasted_iota(jnp.int32, (hb, c), 0)
    lane_h = lax.broadcasted_iota(jnp.int32, (hb, c), 1)
    top_mask = lane_h <= row_h
    left = lane_h < hb
    bot_mask = lane_h - hb <= row_h
    pair = 2 * HGRN_HEAD_DIM
    pr = lax.broadcasted_iota(jnp.int32, (pair, pair), 0)
    pc_i = lax.broadcasted_iota(jnp.int32, (pair, pair), 1)
    same_head = (pr < HGRN_HEAD_DIM) == (pc_i < HGRN_HEAD_DIM)
    up_rows = lax.broadcasted_iota(jnp.int32, (c, pair), 1) < HGRN_HEAD_DIM
    for n in range(ts // c):
        rows = slice(n * c, (n + 1) * c)
        bc = bcum[rows]
        b_a, b_m, b_b, b_l = (bc[hb // 2 - 1:hb // 2], bc[hb - 1:hb],
                              bc[hb + hb // 2 - 1:hb + hb // 2], bc[c - 1:c])
        stack = lambda ra, rb: jnp.concatenate(
            [jnp.broadcast_to(ra, (hb, w)), jnp.broadcast_to(rb, (hb, w))], axis=0)
        ref = stack(b_a, b_b)
        qx = q[rows] * jnp.exp(bc - ref)
        kx = k[rows] * jnp.exp(ref - bc)
        qd = (qx * stack(jnp.exp(b_a), jnp.exp(b_b))).astype(_BF16)
        kd = (kx * stack(jnp.exp(b_l - b_a), jnp.exp(b_l - b_b))).astype(_BF16)
        q_off = qx[hb:] * jnp.exp(b_b - b_m)
        k_off = kx[:hb] * jnp.exp(b_m - b_a)
        q3 = jnp.concatenate([qx, q_off], axis=0).astype(_BF16)
        k3 = jnp.concatenate([kx, k_off, jnp.zeros((hb, w), _F32)], axis=0).astype(_BF16)
        dec = jnp.exp(b_l)
        iv_c = iv[rows]
        for p in range(n_heads // 2):
            pcols = slice(p * pair, (p + 1) * pair)
            a_heads = []
            for hd in (2 * p, 2 * p + 1):
                cols = slice(hd * HGRN_HEAD_DIM, (hd + 1) * HGRN_HEAD_DIM)
                sc = _dot_nt(q3[:, cols], k3[:, cols])
                top = jnp.where(top_mask, sc[0:hb, 0:c], 0.0)
                bot = jnp.where(left, sc[2 * hb:3 * hb, c:2 * c],
                                jnp.where(bot_mask, sc[hb:2 * hb, 0:c], 0.0))
                a_heads.append(jnp.concatenate([top, bot], axis=0))
            a_pair = jnp.concatenate(a_heads, axis=1).astype(_BF16)
            iv_p = iv_c[:, pcols]
            iv_blk = jnp.concatenate([jnp.where(up_rows, iv_p, 0.0).astype(_BF16),
                                      jnp.where(up_rows, 0.0, iv_p).astype(_BF16)], axis=0)
            st = st_ref[p]
            o = _dot(a_pair, iv_blk) + _dot_nt(qd[:, pcols], st.astype(_BF16))
            upd = _dot_tn(iv_p, kd[:, pcols])
            st_ref[p] = st * dec[:, pcols] + jnp.where(same_head, upd, 0.0)
            for j in range(2):
                hcols = slice(j * HGRN_HEAD_DIM, (j + 1) * HGRN_HEAD_DIM)
                ob_ref[rows, p * pair + j * HGRN_HEAD_DIM:p * pair + (j + 1) * HGRN_HEAD_DIM] = _rms(o[:, hcols])
    yb = (ob_ref[...] * ghg_ref[...]) * act_ref[a_og]

    y = act_ref[a_ga] * ya_ref[...] + act_ref[a_gb] * yb
    out = _dot(y.astype(_BF16), wout_ref[...])
    x1 = jnp.where(is_tile, x + gt1 * (_rms(out) * gpost_ref[...]), x1_prev)
    o_ref[0] = x1
    x1p_ref[...] = x1


def _mixer(x, ada, g_pre, g_post, w_in, ln_g, ln_b, w_spatial, b_spatial, lb, g_hgrn, w_out,
           g_pre_ffn, w_rg, b_rg, w_re, b_re):
    b, s, d = x.shape
    w = w_out.shape[0]
    n_groups, gchunk, _ = w_spatial.shape
    ts = min(MIXER_SEQ_TILE, s)
    assert s % ts == 0 and ts % gchunk == 0 and ts % HGRN_CHUNK == 0 and ts % LANES == 0
    n_rg, n_exp = w_rg.shape[1], w_re.shape[1]
    assert n_rg + n_exp <= LANES
    pad = LANES - n_rg - n_exp
    wrt = jnp.concatenate([w_rg, w_re, jnp.zeros((d, pad), _F32)], axis=1).T.astype(_BF16)
    brt = jnp.concatenate([b_rg, b_re, jnp.zeros((pad,), _F32)]).reshape(LANES, 1).astype(_F32)
    tiles = s // ts
    nt = b * tiles
    cur = lambda t: jnp.minimum(t, nt - 1)
    prev = lambda t: jnp.maximum(t - 1, 0)
    cur3 = lambda t: (cur(t), 0, 0)
    prev3 = lambda t: (prev(t), 0, 0)
    assert w % HGRN_HEAD_DIM == 0 and w // n_groups == LANES and w_in.shape == (d, N_IN_SLICES * w)
    assert w % PROJ_COLS == 0
    n_heads = w // HGRN_HEAD_DIM
    assert n_heads % 2 == 0 and HGRN_HEAD_DIM == LANES
    row = lambda a: a.reshape(1, -1).astype(_F32)
    const2 = lambda t: (0, 0)
    const3 = lambda t: (0, 0, 0)
    single = dict(pipeline_mode=pl.Buffered(1))
    x1, h2, meta, metat, cnt = pl.pallas_call(
        functools.partial(_mixer_kernel, n_route_groups=n_rg, per_group=n_exp // n_rg,
                          tiles_per_batch=tiles),
        grid=(nt + 1,),
        in_specs=[
            pl.BlockSpec((1, ts, d), cur3),
            pl.BlockSpec((1, ada.shape[1], d), lambda t: (cur(t) // tiles, 0, 0)),
            pl.BlockSpec((1, ada.shape[1], d), lambda t: (prev(t) // tiles, 0, 0)),
            pl.BlockSpec((1, d), const2),
            pl.BlockSpec((1, d), const2),
            pl.BlockSpec((d, N_IN_SLICES * w), const2, **single),
            pl.BlockSpec((1, w), const2),
            pl.BlockSpec((1, w), const2),
            pl.BlockSpec((n_groups, gchunk, gchunk), const3, **single),
            pl.BlockSpec((gchunk, n_groups), const2),
            pl.BlockSpec((1, w), const2),
            pl.BlockSpec((1, w), const2),
            pl.BlockSpec((w, d), const2, **single),
            pl.BlockSpec((1, d), const2),
            pl.BlockSpec((LANES, d), const2),
            pl.BlockSpec((LANES, 1), const2),
        ],
        out_specs=[
            pl.BlockSpec((1, ts, d), cur3),
            pl.BlockSpec((1, ts, d), prev3),
            pl.BlockSpec((1, ts, LANES), prev3),
            pl.BlockSpec((1, 8, ts), prev3),
            pl.BlockSpec((1, 8, LANES), prev3),
        ],
        out_shape=[
            jax.ShapeDtypeStruct((nt, ts, d), _F32),
            jax.ShapeDtypeStruct((nt, ts, d), _BF16),
            jax.ShapeDtypeStruct((nt, ts, LANES), _F32),
            jax.ShapeDtypeStruct((nt, 8, ts), _F32),
            jax.ShapeDtypeStruct((nt, 8, LANES), jnp.int32),
        ],
        scratch_shapes=[
            pltpu.VMEM((n_heads // 2, 2 * HGRN_HEAD_DIM, 2 * HGRN_HEAD_DIM), _F32),
            pltpu.VMEM((ts, w), _F32),
            pltpu.VMEM((ts, w), _F32),
            pltpu.VMEM((ts, d), _F32),
            pltpu.VMEM((N_IN_SLICES, ts, w), _F32),
            pltpu.VMEM((ts, w), _BF16),
            pltpu.VMEM((ts, w), _BF16),
        ],
        compiler_params=pltpu.CompilerParams(
            dimension_semantics=("arbitrary",),
            vmem_limit_bytes=VMEM_LIMIT_BYTES),
        name="mixer",
    )(x.reshape(nt, ts, d), ada, ada, row(g_pre), row(g_post), w_in.astype(_BF16), row(ln_g),
      row(ln_b), w_spatial.astype(_F32), b_spatial.T.astype(_F32), row(lb), row(g_hgrn),
      w_out.astype(_BF16), row(g_pre_ffn), wrt, brt)
    return x1.reshape(b, s, d), h2, meta, metat, cnt


def _first_max(vals):
    m = functools.reduce(jnp.maximum, vals)
    idx = jnp.full(m.shape, len(vals) - 1, jnp.int32)
    for j in range(len(vals) - 2, -1, -1):
        idx = jnp.where(vals[j] == m, j, idx)
    return m, idx


def _route(lt, n_groups, per_group):
    lg = [lt[g:g + 1] for g in range(n_groups)]
    mg, g_idx = _first_max(lg)
    p_top = 1.0 / functools.reduce(lambda a, b: a + b, [jnp.exp(l - mg) for l in lg])
    le = []
    for j in range(per_group):
        v = lt[n_groups + j:n_groups + j + 1]
        for g in range(1, n_groups):
            r0 = n_groups + g * per_group + j
            v = jnp.where(g_idx == g, lt[r0:r0 + 1], v)
        le.append(v)
    m1, j1 = _first_max(le)
    m2, j2 = _first_max([jnp.where(j1 == j, -jnp.inf, v) for j, v in enumerate(le)])
    r = jnp.exp(m2 - m1)
    w1 = p_top / (1.0 + r)
    w2 = w1 * r
    return g_idx * per_group + j1, g_idx * per_group + j2, w1, w2


def _route_tile(x, sh2, sc2, gpre, wrt, brt, h_ref, meta_ref, metat_ref, cnt_ref, n_groups, per_group):
    tm = x.shape[0]
    h = (_rms(x) * gpre) * (1.0 + sc2) + sh2
    hb = h.astype(_BF16)
    h_ref[0] = hb
    lt = _dot_nt(wrt, hb) + brt
    e1, e2, w1, w2 = _route(lt, n_groups, per_group)

    sub = lax.broadcasted_iota(jnp.int32, (LANES, tm), 0)
    ind = jnp.where((sub == e1) | (sub == e2), 1.0, 0.0)
    earlier = (lax.broadcasted_iota(jnp.int32, (tm, tm), 0)
               < lax.broadcasted_iota(jnp.int32, (tm, tm), 1))
    rank = _dot(ind.astype(_BF16), jnp.where(earlier, 1.0, 0.0).astype(_BF16))
    cnt = jnp.sum(ind, axis=1, keepdims=True)
    chunks = jnp.floor((cnt + (CHUNK_ROWS - 1)) * (1.0 / CHUNK_ROWS))
    below = (lax.broadcasted_iota(jnp.int32, (LANES, LANES), 1)
             < lax.broadcasted_iota(jnp.int32, (LANES, LANES), 0))
    first_chunk = _dot(jnp.where(below, 1.0, 0.0).astype(_BF16),
                       jnp.broadcast_to(chunks, (LANES, LANES)).astype(_BF16))
    dest = first_chunk[:, 0:1] * CHUNK_ROWS + rank
    n_rows = -(-n_groups * per_group // 8) * 8
    sub_e = lax.broadcasted_iota(jnp.int32, (n_rows, tm), 0)
    d1 = jnp.sum(jnp.where(sub_e == e1, dest[:n_rows], 0.0), axis=0, keepdims=True)
    d2 = jnp.sum(jnp.where(sub_e == e2, dest[:n_rows], 0.0), axis=0, keepdims=True)
    row8 = lax.broadcasted_iota(jnp.int32, (8, tm), 0)
    metat = jnp.where(row8 == 0, d1, jnp.where(row8 == 1, d2,
                      jnp.where(row8 == 2, w1, jnp.where(row8 == 3, w2, 0.0))))
    metat_ref[0] = metat
    meta_ref[0] = jnp.concatenate([metat, jnp.zeros((LANES - 8, tm), _F32)], axis=0).T
    cnt_ref[0] = jnp.broadcast_to(cnt, (LANES, LANES)).T[0:8].astype(jnp.int32)


def _chunk_loop(n, fn):
    def body(c, carry):
        fn(c)
        return carry
    lax.fori_loop(0, n, body, 0)


def _wait_chunks(n, max_chunks, copy_of_chunks):
    bit = 1
    while bit * 2 <= max_chunks:
        bit *= 2
    while bit >= 1:
        @pl.when((n & bit) != 0)
        def _(bit=bit):
            copy_of_chunks(bit).wait()
        bit //= 2


def _dispatch_kernel(nch_ref, cmap_ref, tbase_ref, ntail_ref, nu_ref, metat_ref, h_ref, xs_ref,
                     loc_ref, zero_ref, sem, zsem, *, max_chunks):
    i = pl.program_id(0)
    n_steps = pl.num_programs(0)
    g = h_ref.shape[0]
    lrows, tm = loc_ref.shape[1], h_ref.shape[1]
    buf = lambda step, j: (step % 2) * g + j

    def chunk_copy(slot_, c, dst_chunk):
        return pltpu.make_async_copy(
            loc_ref.at[slot_, pl.ds(pl.multiple_of(c * CHUNK_ROWS, CHUNK_ROWS), CHUNK_ROWS)],
            xs_ref.at[pl.ds(pl.multiple_of(dst_chunk * CHUNK_ROWS, CHUNK_ROWS), CHUNK_ROWS)],
            sem.at[slot_])

    def zero_copy(dst_chunk):
        return pltpu.make_async_copy(
            zero_ref.at[pl.ds(0, CHUNK_ROWS)],
            xs_ref.at[pl.ds(pl.multiple_of(dst_chunk * CHUNK_ROWS, CHUNK_ROWS), CHUNK_ROWS)],
            zsem.at[0])

    def zero_tile_copy(tile):
        return pltpu.make_async_copy(
            zero_ref,
            xs_ref.at[pl.ds(pl.multiple_of(tile * GMM_ROW_TILE, GMM_ROW_TILE), GMM_ROW_TILE)],
            zsem.at[1])

    @pl.when(i == 0)
    def _():
        zero_ref[...] = jnp.zeros(zero_ref.shape, zero_ref.dtype)
        n_unused = xs_ref.shape[0] // GMM_ROW_TILE - nu_ref[0]
        for e in range(tbase_ref.shape[0]):
            _chunk_loop(ntail_ref[e], lambda j, e=e: zero_copy(tbase_ref[e] + j).start())
        _chunk_loop(n_unused, lambda j: zero_tile_copy(nu_ref[0] + j).start())
        for e in range(tbase_ref.shape[0]):
            _chunk_loop(ntail_ref[e], lambda j: zero_copy(0).wait())
        _chunk_loop(n_unused, lambda j: zero_tile_copy(0).wait())

    def wait_slot(slot_, n_):
        _wait_chunks(n_, max_chunks, lambda k: pltpu.make_async_copy(
            loc_ref.at[slot_, pl.ds(0, k * CHUNK_ROWS)], xs_ref.at[pl.ds(0, k * CHUNK_ROWS)],
            sem.at[slot_]))

    @pl.when(i >= 2)
    def _():
        for j in range(g):
            wait_slot(buf(i, j), nch_ref[(i - 2) * g + j])

    r = lax.broadcasted_iota(jnp.int32, (lrows, tm), 0)
    lane = lax.broadcasted_iota(jnp.int32, (lrows, LANES), 1)
    d = h_ref.shape[2]
    for j in range(g):
        mt = metat_ref[j]
        is1 = r == mt[0:1].astype(jnp.int32)
        is2 = r == mt[1:2].astype(jnp.int32)
        sel = jnp.where(is1 | is2, 1.0, 0.0).astype(_BF16)
        loc_ref[buf(i, j), :, 0:d] = _dot(sel, h_ref[j]).astype(_BF16)
        ws = jnp.sum(jnp.where(is1, mt[2:3], 0.0) + jnp.where(is2, mt[3:4], 0.0),
                     axis=1, keepdims=True)
        hi, mid, lo = (p.astype(_F32) for p in _split3(ws))
        loc_ref[buf(i, j), :, d:d + LANES] = jnp.where(
            lane == 0, hi, jnp.where(lane == 1, mid, jnp.where(lane == 2, lo, 0.0))).astype(_BF16)
    for j in range(g):
        tile = i * g + j
        _chunk_loop(nch_ref[tile], lambda c, j=j, tile=tile: chunk_copy(
            buf(i, j), c, cmap_ref[tile * max_chunks + c]).start())

    @pl.when(i == n_steps - 1)
    def _():
        for j in range(g):
            wait_slot(buf(i, j), nch_ref[i * g + j])

        @pl.when(i >= 1)
        def _():
            for j in range(g):
                wait_slot(buf(i - 1, j), nch_ref[(i - 1) * g + j])


def _experts_kernel(te_ref, tf_ref, nu_ref, xs_ref, w1_ref, w3_ref, w2_ref, ys_ref,
                    wb1_ref, wb3_ref, wb2_ref):
    i = pl.program_id(0)

    @pl.when(i < nu_ref[0])
    def _():
        @pl.when(tf_ref[i] == 1)
        def _():
            wb1_ref[...] = w1_ref[0].astype(_BF16)
            wb3_ref[...] = w3_ref[0].astype(_BF16)
            wb2_ref[...] = w2_ref[0].astype(_BF16)

        d = wb1_ref.shape[0]
        xs = xs_ref[:, 0:d]
        half = wb1_ref.shape[1] // 2
        acc = None
        for hcols in (slice(0, half), slice(half, 2 * half)):
            a = _silu(_dot(xs, wb1_ref[:, hcols])) * _dot(xs, wb3_ref[:, hcols])
            part = _dot(a.astype(_BF16), wb2_ref[hcols, :])
            acc = part if acc is None else acc + part
        wp = xs_ref[:, d:d + LANES].astype(_F32)
        ys_ref[...] = (acc * (wp[:, 0:1] + wp[:, 1:2] + wp[:, 2:3])).astype(_BF16)

    @pl.when(i >= nu_ref[0])
    def _():
        ys_ref[...] = jnp.zeros(ys_ref.shape, ys_ref.dtype)


def _combine_kernel(nch_ref, cmap_ref, x_ref, ada_ref, gpost_ref, meta_ref, ys_ref, o_ref,
                    loc_ref, sem, *, max_chunks):
    i = pl.program_id(0)
    n_steps = pl.num_programs(0)
    g = x_ref.shape[0]
    lrows, tm = loc_ref.shape[1], x_ref.shape[1]
    buf = lambda step, j: (step % 2) * g + j

    def chunk_copy(tile, slot_, c):
        src_chunk = cmap_ref[tile * max_chunks + c]
        return pltpu.make_async_copy(
            ys_ref.at[pl.ds(pl.multiple_of(src_chunk * CHUNK_ROWS, CHUNK_ROWS), CHUNK_ROWS)],
            loc_ref.at[slot_, pl.ds(pl.multiple_of(c * CHUNK_ROWS, CHUNK_ROWS), CHUNK_ROWS)],
            sem.at[slot_])

    def fetch(step):
        for j in range(g):
            tile = step * g + j
            _chunk_loop(nch_ref[tile], lambda c, j=j, tile=tile: chunk_copy(tile, buf(step, j), c).start())

    @pl.when(i == 0)
    def _():
        loc_ref[...] = jnp.zeros(loc_ref.shape, loc_ref.dtype)
        fetch(0)

    @pl.when(i + 1 < n_steps)
    def _():
        fetch(i + 1)

    for j in range(g):
        _wait_chunks(nch_ref[i * g + j], max_chunks, lambda k, j=j: pltpu.make_async_copy(
            ys_ref.at[pl.ds(0, k * CHUNK_ROWS)], loc_ref.at[buf(i, j), pl.ds(0, k * CHUNK_ROWS)],
            sem.at[buf(i, j)]))

    r = lax.broadcasted_iota(jnp.int32, (tm, lrows), 1)
    gt2 = ada_ref[0][5:6]
    for j in range(g):
        meta = meta_ref[j]
        d1 = meta[:, 0:1].astype(jnp.int32)
        d2 = meta[:, 1:2].astype(jnp.int32)
        sel = jnp.where((r == d1) | (r == d2), 1.0, 0.0).astype(_BF16)
        y = _dot(sel, loc_ref[buf(i, j)])
        o_ref[j] = x_ref[j] + gt2 * (_rms(y) * gpost_ref[...])


def _moe_plan(cnt, max_chunks, n_gmm_tiles):
    per_tile = GMM_ROW_TILE // CHUNK_ROWS
    pc = (cnt + (CHUNK_ROWS - 1)) // CHUNK_ROWS
    local_first = jnp.cumsum(pc, axis=1) - pc
    nch = jnp.sum(pc, axis=1)
    e_chunks = jnp.sum(pc, axis=0)
    e_region = ((e_chunks + per_tile - 1) // per_tile) * per_tile
    e_first = jnp.cumsum(e_region) - e_region
    seg_first = e_first[None, :] + jnp.cumsum(pc, axis=0) - pc
    c = jnp.arange(max_chunks, dtype=jnp.int32)[None, :, None]
    inside = (c >= local_first[:, None, :]) & (c < (local_first + pc)[:, None, :])
    cmap = jnp.sum(jnp.where(inside, seg_first[:, None, :] + c - local_first[:, None, :], 0), axis=-1)
    n_used = jnp.sum(e_region) // per_tile
    t = jnp.arange(n_gmm_tiles, dtype=jnp.int32)
    t_used = jnp.minimum(t, n_used - 1)
    tile_e = jnp.sum(t_used[:, None] * per_tile >= (e_first + e_region)[None, :], axis=1)
    tile_first = jnp.concatenate([jnp.ones((1,), jnp.int32),
                                  (tile_e[1:] != tile_e[:-1]).astype(jnp.int32)])
    i32 = lambda a: a.astype(jnp.int32)
    return dict(nch=i32(nch), cmap=i32(cmap.reshape(-1)), tail_first=i32(e_first + e_chunks),
                n_tail=i32(e_region - e_chunks), tile_e=i32(tile_e), tile_first=tile_first,
                n_used=i32(n_used.reshape(1)))


def _moe(x, h2, meta, metat, cnt, ada, g_post, n_groups, w1, w3, w2):
    b, s, d = x.shape
    n_exp, _, ff = w1.shape
    tm = h2.shape[1]
    assert s % tm == 0 and tm % LANES == 0 and GMM_ROW_TILE % CHUNK_ROWS == 0
    tiles_per_batch = s // tm
    nt = b * tiles_per_batch
    max_chunks = (TOP_K_IN_GROUP * tm + n_exp * (CHUNK_ROWS - 1)) // CHUNK_ROWS
    lrows = -(-max_chunks * CHUNK_ROWS // LANES) * LANES
    per_tile = GMM_ROW_TILE // CHUNK_ROWS
    n_gmm_tiles = -(-(nt * max_chunks + n_exp * (per_tile - 1)) // per_tile)
    n_rows = n_gmm_tiles * GMM_ROW_TILE
    dx = d + LANES

    row = lambda a: a.reshape(1, -1).astype(_F32)
    n_ada = ada.shape[1]
    xt = x.reshape(nt, tm, d)
    g = MOE_TILES_PER_STEP if tiles_per_batch % MOE_TILES_PER_STEP == 0 else 1
    steps_per_batch = tiles_per_batch // g

    plan = _moe_plan(cnt[:, 0, :n_exp], max_chunks, n_gmm_tiles)

    xs = pl.pallas_call(
        functools.partial(_dispatch_kernel, max_chunks=max_chunks),
        grid_spec=pltpu.PrefetchScalarGridSpec(
            num_scalar_prefetch=5,
            grid=(nt // g,),
            in_specs=[
                pl.BlockSpec((g, 8, tm), lambda i, *_: (i, 0, 0)),
                pl.BlockSpec((g, tm, d), lambda i, *_: (i, 0, 0)),
            ],
            out_specs=pl.BlockSpec(memory_space=pl.ANY),
            scratch_shapes=[
                pltpu.VMEM((2 * g, lrows, dx), _BF16),
                pltpu.VMEM((GMM_ROW_TILE, dx), _BF16),
                pltpu.SemaphoreType.DMA((2 * g,)),
                pltpu.SemaphoreType.DMA((2,)),
            ],
        ),
        out_shape=jax.ShapeDtypeStruct((n_rows, dx), _BF16),
        compiler_params=pltpu.CompilerParams(dimension_semantics=("arbitrary",)),
        name="dispatch",
    )(plan["nch"], plan["cmap"], plan["tail_first"], plan["n_tail"], plan["n_used"], metat, h2)

    ys = pl.pallas_call(
        _experts_kernel,
        grid_spec=pltpu.PrefetchScalarGridSpec(
            num_scalar_prefetch=3,
            grid=(n_gmm_tiles,),
            in_specs=[
                pl.BlockSpec((GMM_ROW_TILE, dx), lambda i, te, tf, nu: (jnp.minimum(i, nu[0] - 1), 0)),
                pl.BlockSpec((1, d, ff), lambda i, te, tf, nu: (te[i], 0, 0)),
                pl.BlockSpec((1, d, ff), lambda i, te, tf, nu: (te[i], 0, 0)),
                pl.BlockSpec((1, ff, d), lambda i, te, tf, nu: (te[i], 0, 0)),
            ],
            out_specs=pl.BlockSpec((GMM_ROW_TILE, d), lambda i, te, tf, nu: (i, 0)),
            scratch_shapes=[
                pltpu.VMEM((d, ff), _BF16),
                pltpu.VMEM((d, ff), _BF16),
                pltpu.VMEM((ff, d), _BF16),
            ],
        ),
        out_shape=jax.ShapeDtypeStruct((n_rows, d), _BF16),
        compiler_params=pltpu.CompilerParams(dimension_semantics=("arbitrary",),
                                             vmem_limit_bytes=VMEM_LIMIT_BYTES),
        name="experts",
    )(plan["tile_e"], plan["tile_first"], plan["n_used"], xs, w1, w3, w2)

    out = pl.pallas_call(
        functools.partial(_combine_kernel, max_chunks=max_chunks),
        grid_spec=pltpu.PrefetchScalarGridSpec(
            num_scalar_prefetch=2,
            grid=(nt // g,),
            in_specs=[
                pl.BlockSpec((g, tm, d), lambda i, *_: (i, 0, 0)),
                pl.BlockSpec((1, n_ada, d), lambda i, *_: (i // steps_per_batch, 0, 0)),
                pl.BlockSpec((1, d), lambda i, *_: (0, 0)),
                pl.BlockSpec((g, tm, LANES), lambda i, *_: (i, 0, 0)),
                pl.BlockSpec(memory_space=pl.ANY),
            ],
            out_specs=pl.BlockSpec((g, tm, d), lambda i, *_: (i, 0, 0)),
            scratch_shapes=[
                pltpu.VMEM((2 * g, lrows, d), _BF16),
                pltpu.SemaphoreType.DMA((2 * g,)),
            ],
        ),
        out_shape=jax.ShapeDtypeStruct((nt, tm, d), _F32),
        compiler_params=pltpu.CompilerParams(dimension_semantics=("arbitrary",)),
        name="combine",
    )(plan["nch"], plan["cmap"], xt, ada, row(g_post), meta, ys)
    return out.reshape(b, s, d)


def kernel(x, c, w_ada, b_ada, g_pre_mix, g_post_mix, w_in, ln_v_g, ln_v_b, w_spatial, b_spatial,
           lb_logits, g_hgrn_norm, w_out, g_pre_ffn, g_post_ffn, w_router_group, b_router_group,
           w_router_expert, b_router_expert, w1, w3, w2):
    depth = w_in.shape[0]
    b, s, d = x.shape
    lb_all = _lower_bounds(lb_logits)
    for layer in range(depth):
        ada = _ada(c, w_ada[layer], b_ada[layer]).reshape(b, 6, d)
        x, h2, meta, metat, cnt = _mixer(
            x, ada, g_pre_mix[layer], g_post_mix[layer], w_in[layer], ln_v_g[layer], ln_v_b[layer],
            w_spatial[layer], b_spatial[layer], lb_all[layer], g_hgrn_norm[layer], w_out[layer],
            g_pre_ffn[layer], w_router_group[layer], b_router_group[layer],
            w_router_expert[layer], b_router_expert[layer])
        x = _moe(x, h2, meta, metat, cnt, ada, g_post_ffn[layer], w_router_group.shape[-1],
                 w1[layer], w3[layer], w2[layer])
    return x
```

```python
import functools

import jax
import jax.numpy as jnp
from jax import lax
from jax.experimental import pallas as pl
from jax.experimental.pallas import tpu as pltpu

RMS_EPS = 1e-6
LN_EPS = 1e-5
HGRN_HEAD_DIM = 128
HGRN_CHUNK = 128
N_IN_SLICES = 8
TOP_K_IN_GROUP = 2
LANES = 128
MIXER_SEQ_TILE = 256
PROJ_COLS = 256
CHUNK_ROWS = 16
GMM_ROW_TILE = 1024
MOE_TILES_PER_STEP = 4
VMEM_LIMIT_BYTES = 56 * 1024 * 1024

_F32 = jnp.float32
_BF16 = jnp.bfloat16


def _sigmoid(x):
    return 0.5 * (jnp.tanh(0.5 * x) + 1.0)


def _silu(x):
    return x * _sigmoid(x)


def _gelu_tanh(x):
    c = 0.7978845608028654
    return 0.5 * x * (1.0 + jnp.tanh(c * (x + 0.044715 * (x * x * x))))


def _rms(x):
    return x * lax.rsqrt(jnp.mean(x * x, axis=-1, keepdims=True) + RMS_EPS)


def _dot(a, b):
    return jnp.dot(a, b, preferred_element_type=_F32)


def _dot_nt(a, b):
    return lax.dot_general(a, b, (((1,), (1,)), ((), ())), preferred_element_type=_F32)


def _dot_tn(a, b):
    return lax.dot_general(a, b, (((0,), (0,)), ((), ())), preferred_element_type=_F32)


def _split3(x):
    hi = x.astype(_BF16)
    r = x - hi.astype(_F32)
    mid = r.astype(_BF16)
    lo = (r - mid.astype(_F32)).astype(_BF16)
    return hi, mid, lo


def _ada_kernel(c_ref, w_ref, b_ref, o_ref):
    s = _silu(c_ref[...])
    w = w_ref[...]
    acc = jnp.zeros(o_ref.shape, _F32)
    for sp in _split3(s):
        for wp in _split3(w)[:2]:
            acc = acc + _dot(sp, wp)
    o_ref[...] = acc + b_ref[...]


def _ada(c, w_ada, b_ada):
    b, d = c.shape
    n = w_ada.shape[1]
    tn = d
    return pl.pallas_call(
        _ada_kernel,
        grid=(n // tn,),
        in_specs=[
            pl.BlockSpec((b, d), lambda j: (0, 0)),
            pl.BlockSpec((d, tn), lambda j: (0, j)),
            pl.BlockSpec((1, tn), lambda j: (0, j)),
        ],
        out_specs=pl.BlockSpec((b, tn), lambda j: (0, j)),
        out_shape=jax.ShapeDtypeStruct((b, n), _F32),
        name="ada",
    )(c, w_ada, b_ada.reshape(1, n))


def _lb_kernel(l_ref, o_ref):
    l = l_ref[...]
    e = jnp.exp(l - jnp.max(l, axis=0, keepdims=True))
    p = e / jnp.sum(e, axis=0, keepdims=True)
    rows = [p[0:1]]
    for i in range(1, p.shape[0]):
        rows.append(rows[-1] + p[i:i + 1])
    for i, r in enumerate(rows):
        o_ref[i:i + 1, :] = r


def _lower_bounds(lb_logits):
    return pl.pallas_call(
        _lb_kernel,
        out_shape=jax.ShapeDtypeStruct(lb_logits.shape, _F32),
        name="lower_bounds",
    )(lb_logits.astype(_F32))


def _mixer_kernel(x_ref, ada_ref, adap_ref, gpre_ref, gpost_ref, win_ref, lng_ref, lnb_ref, wsp_ref,
                  bspt_ref, lb_ref, ghg_ref, wout_ref, gpre2_ref, wrt_ref, brt_ref,
                  o_ref, h2_ref, meta_ref, metat_ref, cnt_ref, st_ref, ya_ref, ob_ref, x1p_ref,
                  act_ref, iv_ref, vn_ref, bcum_ref,
                  *, n_route_groups, per_group, tiles_per_batch):
    ts, d = x_ref.shape[1], x_ref.shape[2]
    w = wout_ref.shape[0]
    n_groups, gchunk = wsp_ref.shape[0], wsp_ref.shape[1]
    gdim = w // n_groups
    n_heads = w // HGRN_HEAD_DIM
    c = HGRN_CHUNK
    t = pl.program_id(0)
    is_tile = t < pl.num_programs(0) - 1

    @pl.when(t % tiles_per_batch == 0)
    def _():
        st_ref[...] = jnp.zeros(st_ref.shape, _F32)

    @pl.when(t == 0)
    def _():
        x1p_ref[...] = jnp.zeros(x1p_ref.shape, _F32)

    ada = ada_ref[0]
    sh1, sc1, gt1 = ada[0:1], ada[1:2], ada[2:3]
    h = (_rms(x_ref[0]) * gpre_ref[...]) * (1.0 + sc1) + sh1
    h_bf = h.astype(_BF16)

    lb = lb_ref[...]
    (s_v, s_u, s_q, s_f, s_i, s_og, s_ga, s_gb) = (1, 0, 2, 3, 4, 5, 6, 7)
    (a_u, a_v, a_q, a_lf, a_k, a_og, a_ga, a_gb) = range(8)

    def activate(j, cols, r):
        if j in (s_u, s_v):
            act_ref[a_u if j == s_u else a_v, :, cols] = _gelu_tanh(r)
        elif j in (s_q, s_og):
            act_ref[a_q if j == s_q else a_og, :, cols] = _silu(r)
        elif j == s_f:
            f = lb[:, cols] + (1.0 - lb[:, cols]) * _sigmoid(r)
            act_ref[a_lf, :, cols] = jnp.log(f)
            act_ref[a_k, :, cols] = 1.0 - f
        elif j == s_i:
            iv_ref[:, cols] = r.astype(_BF16)
        else:
            act_ref[a_ga if j == s_ga else a_gb, :, cols] = _sigmoid(r)

    def layer_norm_v():
        v = act_ref[a_v]
        mu = jnp.mean(v, axis=-1, keepdims=True)
        vc = v - mu
        var = jnp.mean(vc * vc, axis=-1, keepdims=True)
        vn_ref[...] = ((vc * lax.rsqrt(var + LN_EPS)) * lng_ref[...] + lnb_ref[...]).astype(_BF16)

    for j in (s_v, s_u, s_q, s_f, s_i, s_og, s_ga, s_gb):
        for cb in range(w // PROJ_COLS):
            cols = slice(cb * PROJ_COLS, (cb + 1) * PROJ_COLS)
            activate(j, cols, _dot(h_bf, win_ref[:, j * w + cb * PROJ_COLS:j * w + (cb + 1) * PROJ_COLS]))
        if j == s_v:
            layer_norm_v()

    adap = adap_ref[0]
    _route_tile(x1p_ref[...], adap[3:4], adap[4:5], gpre2_ref[...], wrt_ref[...], brt_ref[...],
                h2_ref, meta_ref, metat_ref, cnt_ref, n_route_groups, per_group)

    tri_g = (lax.broadcasted_iota(jnp.int32, (gchunk, gchunk), 0)
             >= lax.broadcasted_iota(jnp.int32, (gchunk, gchunk), 1))
    bspt = bspt_ref[...]
    n_gchunks = ts // gchunk
    for g in range(n_groups):
        wc = jnp.where(tri_g, wsp_ref[g], 0.0).astype(_BF16)
        cols = slice(g * gdim, (g + 1) * gdim)
        vg = jnp.concatenate([vn_ref[n * gchunk:(n + 1) * gchunk, cols] for n in range(n_gchunks)],
                             axis=1)
        zv = _dot(wc, vg)
        for n in range(n_gchunks):
            rows = slice(n * gchunk, (n + 1) * gchunk)
            ya_ref[rows, cols] = act_ref[a_u, rows, cols] * (zv[:, n * gdim:(n + 1) * gdim]
                                                             + bspt[:, g:g + 1])

    lf = act_ref[a_lf]
    r_i = lax.broadcasted_iota(jnp.int32, (ts, ts), 0)
    c_i = lax.broadcasted_iota(jnp.int32, (ts, ts), 1)
    ltri = jnp.where((r_i // c == c_i // c) & (c_i <= r_i), 1.0, 0.0).astype(_BF16)
    bcum = jnp.zeros((ts, w), _F32)
    for part in _split3(lf):
        bcum = bcum + _dot(ltri, part)
    bcum_ref[...] = bcum
    hb = c // 2
    row_h = lax.broadcasted_iota(jnp.int32, (hb, c), 0)
    lane_h = lax.broadcasted_iota(jnp.int32, (hb, c), 1)
    top_mask = lane_h <= row_h
    left = lane_h < hb
    bot_mask = lane_h - hb <= row_h
    pair = 2 * HGRN_HEAD_DIM
    pr = lax.broadcasted_iota(jnp.int32, (pair, pair), 0)
    pc_i = lax.broadcasted_iota(jnp.int32, (pair, pair), 1)
    same_head = (pr < HGRN_HEAD_DIM) == (pc_i < HGRN_HEAD_DIM)
    up_rows = lax.broadcasted_iota(jnp.int32, (c, pair), 1) < HGRN_HEAD_DIM
    for n in range(ts // c):
        rows = slice(n * c, (n + 1) * c)
        bc = bcum_ref[rows, :]
        b_a, b_m, b_b, b_l = (bc[hb // 2 - 1:hb // 2], bc[hb - 1:hb],
                              bc[hb + hb // 2 - 1:hb + hb // 2], bc[c - 1:c])
        stack = lambda ra, rb: jnp.concatenate(
            [jnp.broadcast_to(ra, (hb, w)), jnp.broadcast_to(rb, (hb, w))], axis=0)
        ref = stack(b_a, b_b)
        qx = act_ref[a_q, rows, :] * jnp.exp(bc - ref)
        kx = act_ref[a_k, rows, :] * jnp.exp(ref - bc)
        qd = (qx * stack(jnp.exp(b_a), jnp.exp(b_b))).astype(_BF16)
        kd = (kx * stack(jnp.exp(b_l - b_a), jnp.exp(b_l - b_b))).astype(_BF16)
        q_off = qx[hb:] * jnp.exp(b_b - b_m)
        k_off = kx[:hb] * jnp.exp(b_m - b_a)
        q3 = jnp.concatenate([qx, q_off], axis=0).astype(_BF16)
        k3 = jnp.concatenate([kx, k_off, jnp.zeros((hb, w), _F32)], axis=0).astype(_BF16)
        dec = jnp.exp(b_l)
        for p in range(n_heads // 2):
            pcols = slice(p * pair, (p + 1) * pair)
            a_heads = []
            for hd in (2 * p, 2 * p + 1):
                cols = slice(hd * HGRN_HEAD_DIM, (hd + 1) * HGRN_HEAD_DIM)
                sc = _dot_nt(q3[:, cols], k3[:, cols])
                top = jnp.where(top_mask, sc[0:hb, 0:c], 0.0)
                bot = jnp.where(left, sc[2 * hb:3 * hb, c:2 * c],
                                jnp.where(bot_mask, sc[hb:2 * hb, 0:c], 0.0))
                a_heads.append(jnp.concatenate([top, bot], axis=0))
            a_pair = jnp.concatenate(a_heads, axis=1).astype(_BF16)
            iv_p = iv_ref[rows, pcols]
            iv_blk = jnp.concatenate([jnp.where(up_rows, iv_p, 0.0).astype(_BF16),
                                      jnp.where(up_rows, 0.0, iv_p).astype(_BF16)], axis=0)
            st = st_ref[p]
            o = _dot(a_pair, iv_blk) + _dot_nt(qd[:, pcols], st.astype(_BF16))
            upd = _dot_tn(iv_p, kd[:, pcols])
            st_ref[p] = st * dec[:, pcols] + jnp.where(same_head, upd, 0.0)
            for j in range(2):
                hcols = slice(j * HGRN_HEAD_DIM, (j + 1) * HGRN_HEAD_DIM)
                ob_ref[rows, p * pair + j * HGRN_HEAD_DIM:p * pair + (j + 1) * HGRN_HEAD_DIM] = _rms(o[:, hcols])
    yb = (ob_ref[...] * ghg_ref[...]) * act_ref[a_og]

    y = act_ref[a_ga] * ya_ref[...] + act_ref[a_gb] * yb
    out = _dot(y.astype(_BF16), wout_ref[...])
    x1 = jnp.where(is_tile, x_ref[0] + gt1 * (_rms(out) * gpost_ref[...]), x1p_ref[...])
    o_ref[0] = x1
    x1p_ref[...] = x1


def _mixer(x, ada, g_pre, g_post, w_in, ln_g, ln_b, w_spatial, b_spatial, lb, g_hgrn, w_out,
           g_pre_ffn, w_rg, b_rg, w_re, b_re):
    b, s, d = x.shape
    w = w_out.shape[0]
    n_groups, gchunk, _ = w_spatial.shape
    ts = min(MIXER_SEQ_TILE, s)
    assert s % ts == 0 and ts % gchunk == 0 and ts % HGRN_CHUNK == 0 and ts % LANES == 0
    n_rg, n_exp = w_rg.shape[1], w_re.shape[1]
    assert n_rg + n_exp <= LANES
    pad = LANES - n_rg - n_exp
    wrt = jnp.concatenate([w_rg, w_re, jnp.zeros((d, pad), _F32)], axis=1).T.astype(_BF16)
    brt = jnp.concatenate([b_rg, b_re, jnp.zeros((pad,), _F32)]).reshape(LANES, 1).astype(_F32)
    tiles = s // ts
    nt = b * tiles
    cur = lambda t: jnp.minimum(t, nt - 1)
    prev = lambda t: jnp.maximum(t - 1, 0)
    cur3 = lambda t: (cur(t), 0, 0)
    prev3 = lambda t: (prev(t), 0, 0)
    assert w % HGRN_HEAD_DIM == 0 and w // n_groups == LANES and w_in.shape == (d, N_IN_SLICES * w)
    assert w % PROJ_COLS == 0
    n_heads = w // HGRN_HEAD_DIM
    assert n_heads % 2 == 0 and HGRN_HEAD_DIM == LANES
    row = lambda a: a.reshape(1, -1).astype(_F32)
    const2 = lambda t: (0, 0)
    const3 = lambda t: (0, 0, 0)
    single = dict(pipeline_mode=pl.Buffered(1))
    x1, h2, meta, metat, cnt = pl.pallas_call(
        functools.partial(_mixer_kernel, n_route_groups=n_rg, per_group=n_exp // n_rg,
                          tiles_per_batch=tiles),
        grid=(nt + 1,),
        in_specs=[
            pl.BlockSpec((1, ts, d), cur3),
            pl.BlockSpec((1, ada.shape[1], d), lambda t: (cur(t) // tiles, 0, 0)),
            pl.BlockSpec((1, ada.shape[1], d), lambda t: (prev(t) // tiles, 0, 0)),
            pl.BlockSpec((1, d), const2),
            pl.BlockSpec((1, d), const2),
            pl.BlockSpec((d, N_IN_SLICES * w), const2, **single),
            pl.BlockSpec((1, w), const2),
            pl.BlockSpec((1, w), const2),
            pl.BlockSpec((n_groups, gchunk, gchunk), const3, **single),
            pl.BlockSpec((gchunk, n_groups), const2),
            pl.BlockSpec((1, w), const2),
            pl.BlockSpec((1, w), const2),
            pl.BlockSpec((w, d), const2, **single),
            pl.BlockSpec((1, d), const2),
            pl.BlockSpec((LANES, d), const2),
            pl.BlockSpec((LANES, 1), const2),
        ],
        out_specs=[
            pl.BlockSpec((1, ts, d), cur3),
            pl.BlockSpec((1, ts, d), prev3),
            pl.BlockSpec((1, ts, LANES), prev3),
            pl.BlockSpec((1, 8, ts), prev3),
            pl.BlockSpec((1, 8, LANES), prev3),
        ],
        out_shape=[
            jax.ShapeDtypeStruct((nt, ts, d), _F32),
            jax.ShapeDtypeStruct((nt, ts, d), _BF16),
            jax.ShapeDtypeStruct((nt, ts, LANES), _F32),
            jax.ShapeDtypeStruct((nt, 8, ts), _F32),
            jax.ShapeDtypeStruct((nt, 8, LANES), jnp.int32),
        ],
        scratch_shapes=[
            pltpu.VMEM((n_heads // 2, 2 * HGRN_HEAD_DIM, 2 * HGRN_HEAD_DIM), _F32),
            pltpu.VMEM((ts, w), _F32),
            pltpu.VMEM((ts, w), _F32),
            pltpu.VMEM((ts, d), _F32),
            pltpu.VMEM((N_IN_SLICES, ts, w), _F32),
            pltpu.VMEM((ts, w), _BF16),
            pltpu.VMEM((ts, w), _BF16),
            pltpu.VMEM((ts, w), _F32),
        ],
        compiler_params=pltpu.CompilerParams(
            dimension_semantics=("arbitrary",),
            vmem_limit_bytes=VMEM_LIMIT_BYTES),
        name="mixer",
    )(x.reshape(nt, ts, d), ada, ada, row(g_pre), row(g_post), w_in.astype(_BF16), row(ln_g),
      row(ln_b), w_spatial.astype(_F32), b_spatial.T.astype(_F32), row(lb), row(g_hgrn),
      w_out.astype(_BF16), row(g_pre_ffn), wrt, brt)
    return x1.reshape(b, s, d), h2, meta, metat, cnt


def _first_max(vals):
    m = functools.reduce(jnp.maximum, vals)
    idx = jnp.full(m.shape, len(vals) - 1, jnp.int32)
    for j in range(len(vals) - 2, -1, -1):
        idx = jnp.where(vals[j] == m, j, idx)
    return m, idx


def _route(lt, n_groups, per_group):
    lg = [lt[g:g + 1] for g in range(n_groups)]
    mg, g_idx = _first_max(lg)
    p_top = 1.0 / functools.reduce(lambda a, b: a + b, [jnp.exp(l - mg) for l in lg])
    le = []
    for j in range(per_group):
        v = lt[n_groups + j:n_groups + j + 1]
        for g in range(1, n_groups):
            r0 = n_groups + g * per_group + j
            v = jnp.where(g_idx == g, lt[r0:r0 + 1], v)
        le.append(v)
    m1, j1 = _first_max(le)
    m2, j2 = _first_max([jnp.where(j1 == j, -jnp.inf, v) for j, v in enumerate(le)])
    r = jnp.exp(m2 - m1)
    w1 = p_top / (1.0 + r)
    w2 = w1 * r
    return g_idx * per_group + j1, g_idx * per_group + j2, w1, w2


def _route_tile(x, sh2, sc2, gpre, wrt, brt, h_ref, meta_ref, metat_ref, cnt_ref, n_groups, per_group):
    tm = x.shape[0]
    h = (_rms(x) * gpre) * (1.0 + sc2) + sh2
    hb = h.astype(_BF16)
    h_ref[0] = hb
    lt = _dot_nt(wrt, hb) + brt
    e1, e2, w1, w2 = _route(lt, n_groups, per_group)

    sub = lax.broadcasted_iota(jnp.int32, (LANES, tm), 0)
    ind = jnp.where((sub == e1) | (sub == e2), 1.0, 0.0)
    earlier = (lax.broadcasted_iota(jnp.int32, (tm, tm), 0)
               < lax.broadcasted_iota(jnp.int32, (tm, tm), 1))
    rank = _dot(ind.astype(_BF16), jnp.where(earlier, 1.0, 0.0).astype(_BF16))
    cnt = jnp.sum(ind, axis=1, keepdims=True)
    chunks = jnp.floor((cnt + (CHUNK_ROWS - 1)) * (1.0 / CHUNK_ROWS))
    below = (lax.broadcasted_iota(jnp.int32, (LANES, LANES), 1)
             < lax.broadcasted_iota(jnp.int32, (LANES, LANES), 0))
    first_chunk = _dot(jnp.where(below, 1.0, 0.0).astype(_BF16),
                       jnp.broadcast_to(chunks, (LANES, LANES)).astype(_BF16))
    dest = first_chunk[:, 0:1] * CHUNK_ROWS + rank
    n_rows = -(-n_groups * per_group // 8) * 8
    sub_e = lax.broadcasted_iota(jnp.int32, (n_rows, tm), 0)
    d1 = jnp.sum(jnp.where(sub_e == e1, dest[:n_rows], 0.0), axis=0, keepdims=True)
    d2 = jnp.sum(jnp.where(sub_e == e2, dest[:n_rows], 0.0), axis=0, keepdims=True)
    row8 = lax.broadcasted_iota(jnp.int32, (8, tm), 0)
    metat = jnp.where(row8 == 0, d1, jnp.where(row8 == 1, d2,
                      jnp.where(row8 == 2, w1, jnp.where(row8 == 3, w2, 0.0))))
    metat_ref[0] = metat
    meta_ref[0] = jnp.concatenate([metat, jnp.zeros((LANES - 8, tm), _F32)], axis=0).T
    cnt_ref[0] = jnp.broadcast_to(cnt, (LANES, LANES)).T[0:8].astype(jnp.int32)


def _chunk_loop(n, fn):
    def body(c, carry):
        fn(c)
        return carry
    lax.fori_loop(0, n, body, 0)


def _wait_chunks(n, max_chunks, copy_of_chunks):
    bit = 1
    while bit * 2 <= max_chunks:
        bit *= 2
    while bit >= 1:
        @pl.when((n & bit) != 0)
        def _(bit=bit):
            copy_of_chunks(bit).wait()
        bit //= 2


def _dispatch_kernel(nch_ref, cmap_ref, tbase_ref, ntail_ref, nu_ref, metat_ref, h_ref, xs_ref,
                     loc_ref, zero_ref, sem, zsem, *, max_chunks):
    i = pl.program_id(0)
    n_steps = pl.num_programs(0)
    g = h_ref.shape[0]
    lrows, tm = loc_ref.shape[1], h_ref.shape[1]
    buf = lambda step, j: (step % 2) * g + j

    def chunk_copy(slot_, c, dst_chunk):
        return pltpu.make_async_copy(
            loc_ref.at[slot_, pl.ds(pl.multiple_of(c * CHUNK_ROWS, CHUNK_ROWS), CHUNK_ROWS)],
            xs_ref.at[pl.ds(pl.multiple_of(dst_chunk * CHUNK_ROWS, CHUNK_ROWS), CHUNK_ROWS)],
            sem.at[slot_])

    def zero_copy(dst_chunk):
        return pltpu.make_async_copy(
            zero_ref.at[pl.ds(0, CHUNK_ROWS)],
            xs_ref.at[pl.ds(pl.multiple_of(dst_chunk * CHUNK_ROWS, CHUNK_ROWS), CHUNK_ROWS)],
            zsem.at[0])

    def zero_tile_copy(tile):
        return pltpu.make_async_copy(
            zero_ref,
            xs_ref.at[pl.ds(pl.multiple_of(tile * GMM_ROW_TILE, GMM_ROW_TILE), GMM_ROW_TILE)],
            zsem.at[1])

    @pl.when(i == 0)
    def _():
        zero_ref[...] = jnp.zeros(zero_ref.shape, zero_ref.dtype)
        n_unused = xs_ref.shape[0] // GMM_ROW_TILE - nu_ref[0]
        for e in range(tbase_ref.shape[0]):
            _chunk_loop(ntail_ref[e], lambda j, e=e: zero_copy(tbase_ref[e] + j).start())
        _chunk_loop(n_unused, lambda j: zero_tile_copy(nu_ref[0] + j).start())
        for e in range(tbase_ref.shape[0]):
            _chunk_loop(ntail_ref[e], lambda j: zero_copy(0).wait())
        _chunk_loop(n_unused, lambda j: zero_tile_copy(0).wait())

    def wait_slot(slot_, n_):
        _wait_chunks(n_, max_chunks, lambda k: pltpu.make_async_copy(
            loc_ref.at[slot_, pl.ds(0, k * CHUNK_ROWS)], xs_ref.at[pl.ds(0, k * CHUNK_ROWS)],
            sem.at[slot_]))

    @pl.when(i >= 2)
    def _():
        for j in range(g):
            wait_slot(buf(i, j), nch_ref[(i - 2) * g + j])

    r = lax.broadcasted_iota(jnp.int32, (lrows, tm), 0)
    lane = lax.broadcasted_iota(jnp.int32, (lrows, LANES), 1)
    d = h_ref.shape[2]
    for j in range(g):
        mt = metat_ref[j]
        is1 = r == mt[0:1].astype(jnp.int32)
        is2 = r == mt[1:2].astype(jnp.int32)
        sel = jnp.where(is1 | is2, 1.0, 0.0).astype(_BF16)
        loc_ref[buf(i, j), :, 0:d] = _dot(sel, h_ref[j]).astype(_BF16)
        ws = jnp.sum(jnp.where(is1, mt[2:3], 0.0) + jnp.where(is2, mt[3:4], 0.0),
                     axis=1, keepdims=True)
        hi, mid, lo = (p.astype(_F32) for p in _split3(ws))
        loc_ref[buf(i, j), :, d:d + LANES] = jnp.where(
            lane == 0, hi, jnp.where(lane == 1, mid, jnp.where(lane == 2, lo, 0.0))).astype(_BF16)
    for j in range(g):
        tile = i * g + j
        _chunk_loop(nch_ref[tile], lambda c, j=j, tile=tile: chunk_copy(
            buf(i, j), c, cmap_ref[tile * max_chunks + c]).start())

    @pl.when(i == n_steps - 1)
    def _():
        for j in range(g):
            wait_slot(buf(i, j), nch_ref[i * g + j])

        @pl.when(i >= 1)
        def _():
            for j in range(g):
                wait_slot(buf(i - 1, j), nch_ref[(i - 1) * g + j])


def _experts_kernel(te_ref, tf_ref, nu_ref, xs_ref, w1_ref, w3_ref, w2_ref, ys_ref,
                    wb1_ref, wb3_ref, wb2_ref):
    i = pl.program_id(0)

    @pl.when(i < nu_ref[0])
    def _():
        @pl.when(tf_ref[i] == 1)
        def _():
            wb1_ref[...] = w1_ref[0].astype(_BF16)
            wb3_ref[...] = w3_ref[0].astype(_BF16)
            wb2_ref[...] = w2_ref[0].astype(_BF16)

        d = wb1_ref.shape[0]
        xs = xs_ref[:, 0:d]
        half = wb1_ref.shape[1] // 2
        acc = None
        for hcols in (slice(0, half), slice(half, 2 * half)):
            a = _silu(_dot(xs, wb1_ref[:, hcols])) * _dot(xs, wb3_ref[:, hcols])
            part = _dot(a.astype(_BF16), wb2_ref[hcols, :])
            acc = part if acc is None else acc + part
        wp = xs_ref[:, d:d + LANES].astype(_F32)
        ys_ref[...] = (acc * (wp[:, 0:1] + wp[:, 1:2] + wp[:, 2:3])).astype(_BF16)

    @pl.when(i >= nu_ref[0])
    def _():
        ys_ref[...] = jnp.zeros(ys_ref.shape, ys_ref.dtype)


def _combine_kernel(nch_ref, cmap_ref, x_ref, ada_ref, gpost_ref, meta_ref, ys_ref, o_ref,
                    loc_ref, sem, *, max_chunks):
    i = pl.program_id(0)
    n_steps = pl.num_programs(0)
    g = x_ref.shape[0]
    lrows, tm = loc_ref.shape[1], x_ref.shape[1]
    buf = lambda step, j: (step % 2) * g + j

    def chunk_copy(tile, slot_, c):
        src_chunk = cmap_ref[tile * max_chunks + c]
        return pltpu.make_async_copy(
            ys_ref.at[pl.ds(pl.multiple_of(src_chunk * CHUNK_ROWS, CHUNK_ROWS), CHUNK_ROWS)],
            loc_ref.at[slot_, pl.ds(pl.multiple_of(c * CHUNK_ROWS, CHUNK_ROWS), CHUNK_ROWS)],
            sem.at[slot_])

    def fetch(step):
        for j in range(g):
            tile = step * g + j
            _chunk_loop(nch_ref[tile], lambda c, j=j, tile=tile: chunk_copy(tile, buf(step, j), c).start())

    @pl.when(i == 0)
    def _():
        loc_ref[...] = jnp.zeros(loc_ref.shape, loc_ref.dtype)
        fetch(0)

    @pl.when(i + 1 < n_steps)
    def _():
        fetch(i + 1)

    for j in range(g):
        _wait_chunks(nch_ref[i * g + j], max_chunks, lambda k, j=j: pltpu.make_async_copy(
            ys_ref.at[pl.ds(0, k * CHUNK_ROWS)], loc_ref.at[buf(i, j), pl.ds(0, k * CHUNK_ROWS)],
            sem.at[buf(i, j)]))

    r = lax.broadcasted_iota(jnp.int32, (tm, lrows), 1)
    gt2 = ada_ref[0][5:6]
    for j in range(g):
        meta = meta_ref[j]
        d1 = meta[:, 0:1].astype(jnp.int32)
        d2 = meta[:, 1:2].astype(jnp.int32)
        sel = jnp.where((r == d1) | (r == d2), 1.0, 0.0).astype(_BF16)
        y = _dot(sel, loc_ref[buf(i, j)])
        o_ref[j] = x_ref[j] + gt2 * (_rms(y) * gpost_ref[...])


def _moe_plan(cnt, max_chunks, n_gmm_tiles):
    per_tile = GMM_ROW_TILE // CHUNK_ROWS
    pc = (cnt + (CHUNK_ROWS - 1)) // CHUNK_ROWS
    local_first = jnp.cumsum(pc, axis=1) - pc
    nch = jnp.sum(pc, axis=1)
    e_chunks = jnp.sum(pc, axis=0)
    e_region = ((e_chunks + per_tile - 1) // per_tile) * per_tile
    e_first = jnp.cumsum(e_region) - e_region
    seg_first = e_first[None, :] + jnp.cumsum(pc, axis=0) - pc
    c = jnp.arange(max_chunks, dtype=jnp.int32)[None, :, None]
    inside = (c >= local_first[:, None, :]) & (c < (local_first + pc)[:, None, :])
    cmap = jnp.sum(jnp.where(inside, seg_first[:, None, :] + c - local_first[:, None, :], 0), axis=-1)
    n_used = jnp.sum(e_region) // per_tile
    t = jnp.arange(n_gmm_tiles, dtype=jnp.int32)
    t_used = jnp.minimum(t, n_used - 1)
    tile_e = jnp.sum(t_used[:, None] * per_tile >= (e_first + e_region)[None, :], axis=1)
    tile_first = jnp.concatenate([jnp.ones((1,), jnp.int32),
                                  (tile_e[1:] != tile_e[:-1]).astype(jnp.int32)])
    i32 = lambda a: a.astype(jnp.int32)
    return dict(nch=i32(nch), cmap=i32(cmap.reshape(-1)), tail_first=i32(e_first + e_chunks),
                n_tail=i32(e_region - e_chunks), tile_e=i32(tile_e), tile_first=tile_first,
                n_used=i32(n_used.reshape(1)))


def _moe(x, h2, meta, metat, cnt, ada, g_post, n_groups, w1, w3, w2):
    b, s, d = x.shape
    n_exp, _, ff = w1.shape
    tm = h2.shape[1]
    assert s % tm == 0 and tm % LANES == 0 and GMM_ROW_TILE % CHUNK_ROWS == 0
    tiles_per_batch = s // tm
    nt = b * tiles_per_batch
    max_chunks = (TOP_K_IN_GROUP * tm + n_exp * (CHUNK_ROWS - 1)) // CHUNK_ROWS
    lrows = -(-max_chunks * CHUNK_ROWS // LANES) * LANES
    per_tile = GMM_ROW_TILE // CHUNK_ROWS
    n_gmm_tiles = -(-(nt * max_chunks + n_exp * (per_tile - 1)) // per_tile)
    n_rows = n_gmm_tiles * GMM_ROW_TILE
    dx = d + LANES

    row = lambda a: a.reshape(1, -1).astype(_F32)
    n_ada = ada.shape[1]
    xt = x.reshape(nt, tm, d)
    g = MOE_TILES_PER_STEP if tiles_per_batch % MOE_TILES_PER_STEP == 0 else 1
    steps_per_batch = tiles_per_batch // g

    plan = _moe_plan(cnt[:, 0, :n_exp], max_chunks, n_gmm_tiles)

    xs = pl.pallas_call(
        functools.partial(_dispatch_kernel, max_chunks=max_chunks),
        grid_spec=pltpu.PrefetchScalarGridSpec(
            num_scalar_prefetch=5,
            grid=(nt // g,),
            in_specs=[
                pl.BlockSpec((g, 8, tm), lambda i, *_: (i, 0, 0)),
                pl.BlockSpec((g, tm, d), lambda i, *_: (i, 0, 0)),
            ],
            out_specs=pl.BlockSpec(memory_space=pl.ANY),
            scratch_shapes=[
                pltpu.VMEM((2 * g, lrows, dx), _BF16),
                pltpu.VMEM((GMM_ROW_TILE, dx), _BF16),
                pltpu.SemaphoreType.DMA((2 * g,)),
                pltpu.SemaphoreType.DMA((2,)),
            ],
        ),
        out_shape=jax.ShapeDtypeStruct((n_rows, dx), _BF16),
        compiler_params=pltpu.CompilerParams(dimension_semantics=("arbitrary",)),
        name="dispatch",
    )(plan["nch"], plan["cmap"], plan["tail_first"], plan["n_tail"], plan["n_used"], metat, h2)

    ys = pl.pallas_call(
        _experts_kernel,
        grid_spec=pltpu.PrefetchScalarGridSpec(
            num_scalar_prefetch=3,
            grid=(n_gmm_tiles,),
            in_specs=[
                pl.BlockSpec((GMM_ROW_TILE, dx), lambda i, te, tf, nu: (jnp.minimum(i, nu[0] - 1), 0)),
                pl.BlockSpec((1, d, ff), lambda i, te, tf, nu: (te[i], 0, 0)),
                pl.BlockSpec((1, d, ff), lambda i, te, tf, nu: (te[i], 0, 0)),
                pl.BlockSpec((1, ff, d), lambda i, te, tf, nu: (te[i], 0, 0)),
            ],
            out_specs=pl.BlockSpec((GMM_ROW_TILE, d), lambda i, te, tf, nu: (i, 0)),
            scratch_shapes=[
                pltpu.VMEM((d, ff), _BF16),
                pltpu.VMEM((d, ff), _BF16),
                pltpu.VMEM((ff, d), _BF16),
            ],
        ),
        out_shape=jax.ShapeDtypeStruct((n_rows, d), _BF16),
        compiler_params=pltpu.CompilerParams(dimension_semantics=("arbitrary",),
                                             vmem_limit_bytes=VMEM_LIMIT_BYTES),
        name="experts",
    )(plan["tile_e"], plan["tile_first"], plan["n_used"], xs, w1, w3, w2)

    out = pl.pallas_call(
        functools.partial(_combine_kernel, max_chunks=max_chunks),
        grid_spec=pltpu.PrefetchScalarGridSpec(
            num_scalar_prefetch=2,
            grid=(nt // g,),
            in_specs=[
                pl.BlockSpec((g, tm, d), lambda i, *_: (i, 0, 0)),
                pl.BlockSpec((1, n_ada, d), lambda i, *_: (i // steps_per_batch, 0, 0)),
                pl.BlockSpec((1, d), lambda i, *_: (0, 0)),
                pl.BlockSpec((g, tm, LANES), lambda i, *_: (i, 0, 0)),
                pl.BlockSpec(memory_space=pl.ANY),
            ],
            out_specs=pl.BlockSpec((g, tm, d), lambda i, *_: (i, 0, 0)),
            scratch_shapes=[
                pltpu.VMEM((2 * g, lrows, d), _BF16),
                pltpu.SemaphoreType.DMA((2 * g,)),
            ],
        ),
        out_shape=jax.ShapeDtypeStruct((nt, tm, d), _F32),
        compiler_params=pltpu.CompilerParams(dimension_semantics=("arbitrary",)),
        name="combine",
    )(plan["nch"], plan["cmap"], xt, ada, row(g_post), meta, ys)
    return out.reshape(b, s, d)


def kernel(x, c, w_ada, b_ada, g_pre_mix, g_post_mix, w_in, ln_v_g, ln_v_b, w_spatial, b_spatial,
           lb_logits, g_hgrn_norm, w_out, g_pre_ffn, g_post_ffn, w_router_group, b_router_group,
           w_router_expert, b_router_expert, w1, w3, w2):
    depth = w_in.shape[0]
    b, s, d = x.shape
    lb_all = _lower_bounds(lb_logits)
    for layer in range(depth):
        ada = _ada(c, w_ada[layer], b_ada[layer]).reshape(b, 6, d)
        x, h2, meta, metat, cnt = _mixer(
            x, ada, g_pre_mix[layer], g_post_mix[layer], w_in[layer], ln_v_g[layer], ln_v_b[layer],
            w_spatial[layer], b_spatial[layer], lb_all[layer], g_hgrn_norm[layer], w_out[layer],
            g_pre_ffn[layer], w_router_group[layer], b_router_group[layer],
            w_router_expert[layer], b_router_expert[layer])
        x = _moe(x, h2, meta, metat, cnt, ada, g_post_ffn[layer], w_router_group.shape[-1],
                 w1[layer], w3[layer], w2[layer])
    return x
```

```python
import functools

import jax
import jax.numpy as jnp
from jax import lax
from jax.experimental import pallas as pl
from jax.experimental.pallas import tpu as pltpu

RMS_EPS = 1e-6
LN_EPS = 1e-5
HGRN_HEAD_DIM = 128
HGRN_CHUNK = 128
N_IN_SLICES = 8
TOP_K_IN_GROUP = 2
LANES = 128
MIXER_SEQ_TILE = 256
PROJ_COLS = 256
CHUNK_ROWS = 16
GMM_ROW_TILE = 1024
MOE_TILES_PER_STEP = 4
VMEM_LIMIT_BYTES = 56 * 1024 * 1024

_F32 = jnp.float32
_BF16 = jnp.bfloat16


def _sigmoid(x):
    return 0.5 * (jnp.tanh(0.5 * x) + 1.0)


def _silu(x):
    return x * _sigmoid(x)


def _gelu_tanh(x):
    c = 0.7978845608028654
    return 0.5 * x * (1.0 + jnp.tanh(c * (x + 0.044715 * (x * x * x))))


def _rms(x):
    return x * lax.rsqrt(jnp.mean(x * x, axis=-1, keepdims=True) + RMS_EPS)


def _dot(a, b):
    return jnp.dot(a, b, preferred_element_type=_F32)


def _dot_nt(a, b):
    return lax.dot_general(a, b, (((1,), (1,)), ((), ())), preferred_element_type=_F32)


def _dot_tn(a, b):
    return lax.dot_general(a, b, (((0,), (0,)), ((), ())), preferred_element_type=_F32)


def _split3(x):
    hi = x.astype(_BF16)
    r = x - hi.astype(_F32)
    mid = r.astype(_BF16)
    lo = (r - mid.astype(_F32)).astype(_BF16)
    return hi, mid, lo


def _ada_kernel(c_ref, w_ref, b_ref, o_ref):
    s = _silu(c_ref[...])
    w = w_ref[...]
    acc = jnp.zeros(o_ref.shape, _F32)
    for sp in _split3(s):
        for wp in _split3(w)[:2]:
            acc = acc + _dot(sp, wp)
    o_ref[...] = acc + b_ref[...]


def _ada(c, w_ada, b_ada):
    b, d = c.shape
    n = w_ada.shape[1]
    tn = d
    return pl.pallas_call(
        _ada_kernel,
        grid=(n // tn,),
        in_specs=[
            pl.BlockSpec((b, d), lambda j: (0, 0)),
            pl.BlockSpec((d, tn), lambda j: (0, j)),
            pl.BlockSpec((1, tn), lambda j: (0, j)),
        ],
        out_specs=pl.BlockSpec((b, tn), lambda j: (0, j)),
        out_shape=jax.ShapeDtypeStruct((b, n), _F32),
        name="ada",
    )(c, w_ada, b_ada.reshape(1, n))


def _lb_kernel(l_ref, o_ref):
    l = l_ref[...]
    e = jnp.exp(l - jnp.max(l, axis=0, keepdims=True))
    p = e / jnp.sum(e, axis=0, keepdims=True)
    rows = [p[0:1]]
    for i in range(1, p.shape[0]):
        rows.append(rows[-1] + p[i:i + 1])
    for i, r in enumerate(rows):
        o_ref[i:i + 1, :] = r


def _lower_bounds(lb_logits):
    return pl.pallas_call(
        _lb_kernel,
        out_shape=jax.ShapeDtypeStruct(lb_logits.shape, _F32),
        name="lower_bounds",
    )(lb_logits.astype(_F32))


def _mixer_kernel(x_ref, ada_ref, adap_ref, gpre_ref, gpost_ref, win_ref, lng_ref, lnb_ref, wsp_ref,
                  bspt_ref, lb_ref, ghg_ref, wout_ref, gpre2_ref, wrt_ref, brt_ref,
                  o_ref, h2_ref, meta_ref, metat_ref, cnt_ref, st_ref, ya_ref, ob_ref, x1p_ref,
                  act_ref, iv_ref, vn_ref,
                  *, n_route_groups, per_group, tiles_per_batch):
    ts, d = x_ref.shape[1], x_ref.shape[2]
    w = wout_ref.shape[0]
    n_groups, gchunk = wsp_ref.shape[0], wsp_ref.shape[1]
    gdim = w // n_groups
    n_heads = w // HGRN_HEAD_DIM
    c = HGRN_CHUNK
    t = pl.program_id(0)
    is_tile = t < pl.num_programs(0) - 1

    @pl.when(t % tiles_per_batch == 0)
    def _():
        st_ref[...] = jnp.zeros(st_ref.shape, _F32)

    @pl.when(t == 0)
    def _():
        x1p_ref[...] = jnp.zeros(x1p_ref.shape, _F32)

    x1_prev = x1p_ref[...]
    adap = adap_ref[0]
    _route_tile(x1_prev, adap[3:4], adap[4:5], gpre2_ref[...], wrt_ref[...], brt_ref[...],
                h2_ref, meta_ref, metat_ref, cnt_ref, n_route_groups, per_group)

    x = x_ref[0]
    ada = ada_ref[0]
    sh1, sc1, gt1 = ada[0:1], ada[1:2], ada[2:3]
    h = (_rms(x) * gpre_ref[...]) * (1.0 + sc1) + sh1
    h_bf = h.astype(_BF16)

    lb = lb_ref[...]
    (s_v, s_u, s_q, s_f, s_i, s_og, s_ga, s_gb) = (1, 0, 2, 3, 4, 5, 6, 7)
    (a_u, a_v, a_q, a_lf, a_k, a_og, a_ga, a_gb) = range(8)

    def activate(j, cols, r):
        if j == s_u:
            act_ref[a_u, :, cols] = _gelu_tanh(r)
        elif j == s_v:
            act_ref[a_v, :, cols] = _gelu_tanh(r)
        elif j == s_q:
            act_ref[a_q, :, cols] = _silu(r)
        elif j == s_f:
            f = lb[:, cols] + (1.0 - lb[:, cols]) * _sigmoid(r)
            act_ref[a_lf, :, cols] = jnp.log(f)
            act_ref[a_k, :, cols] = 1.0 - f
        elif j == s_i:
            iv_ref[:, cols] = r.astype(_BF16)
        elif j == s_og:
            act_ref[a_og, :, cols] = _silu(r)
        elif j == s_ga:
            act_ref[a_ga, :, cols] = _sigmoid(r)
        else:
            act_ref[a_gb, :, cols] = _sigmoid(r)

    def layer_norm_v():
        v = act_ref[a_v]
        mu = jnp.mean(v, axis=-1, keepdims=True)
        vc = v - mu
        var = jnp.mean(vc * vc, axis=-1, keepdims=True)
        vn_ref[...] = ((vc * lax.rsqrt(var + LN_EPS)) * lng_ref[...] + lnb_ref[...]).astype(_BF16)

    order = (s_v, s_u, s_q, s_f, s_i, s_og, s_ga, s_gb)
    pending = None
    for pos, j in enumerate(order):
        for cb in range(w // PROJ_COLS):
            cols = slice(cb * PROJ_COLS, (cb + 1) * PROJ_COLS)
            r = _dot(h_bf, win_ref[:, j * w + cb * PROJ_COLS:j * w + (cb + 1) * PROJ_COLS])
            if pending is not None:
                pending()
            pending = functools.partial(activate, j, cols, r)
            if pos == 1 and cb == 0:
                layer_norm_v()
    pending()

    u = act_ref[a_u]
    vn = vn_ref[...]
    tri_g = (lax.broadcasted_iota(jnp.int32, (gchunk, gchunk), 0)
             >= lax.broadcasted_iota(jnp.int32, (gchunk, gchunk), 1))
    bspt = bspt_ref[...]
    n_gchunks = ts // gchunk
    for g in range(n_groups):
        wc = jnp.where(tri_g, wsp_ref[g], 0.0).astype(_BF16)
        cols = slice(g * gdim, (g + 1) * gdim)
        vg = jnp.concatenate([vn[n * gchunk:(n + 1) * gchunk, cols] for n in range(n_gchunks)], axis=1)
        zv = _dot(wc, vg)
        for n in range(n_gchunks):
            rows = slice(n * gchunk, (n + 1) * gchunk)
            ya_ref[rows, cols] = u[rows, cols] * (zv[:, n * gdim:(n + 1) * gdim] + bspt[:, g:g + 1])

    q = act_ref[a_q]
    lf = act_ref[a_lf]
    k = act_ref[a_k]
    iv = iv_ref[...]
    r_i = lax.broadcasted_iota(jnp.int32, (ts, ts), 0)
    c_i = lax.broadcasted_iota(jnp.int32, (ts, ts), 1)
    ltri = jnp.where((r_i // c == c_i // c) & (c_i <= r_i), 1.0, 0.0).astype(_BF16)
    bcum = jnp.zeros((ts, w), _F32)
    for part in _split3(lf):
        bcum = bcum + _dot(ltri, part)
    hb = c // 2
    row_h = lax.broadcasted_iota(jnp.int32, (hb, c), 0)
    lane_h = lax.broadcasted_iota(jnp.int32, (hb, c), 1)
    top_mask = lane_h <= row_h
    left = lane_h < hb
    bot_mask = lane_h - hb <= row_h
    pair = 2 * HGRN_HEAD_DIM
    pr = lax.broadcasted_iota(jnp.int32, (pair, pair), 0)
    pc_i = lax.broadcasted_iota(jnp.int32, (pair, pair), 1)
    same_head = (pr < HGRN_HEAD_DIM) == (pc_i < HGRN_HEAD_DIM)
    up_rows = lax.broadcasted_iota(jnp.int32, (c, pair), 1) < HGRN_HEAD_DIM
    for n in range(ts // c):
        rows = slice(n * c, (n + 1) * c)
        bc = bcum[rows]
        b_a, b_m, b_b, b_l = (bc[hb // 2 - 1:hb // 2], bc[hb - 1:hb],
                              bc[hb + hb // 2 - 1:hb + hb // 2], bc[c - 1:c])
        stack = lambda ra, rb: jnp.concatenate(
            [jnp.broadcast_to(ra, (hb, w)), jnp.broadcast_to(rb, (hb, w))], axis=0)
        ref = stack(b_a, b_b)
        qx = q[rows] * jnp.exp(bc - ref)
        kx = k[rows] * jnp.exp(ref - bc)
        qd = (qx * stack(jnp.exp(b_a), jnp.exp(b_b))).astype(_BF16)
        kd = (kx * stack(jnp.exp(b_l - b_a), jnp.exp(b_l - b_b))).astype(_BF16)
        q_off = qx[hb:] * jnp.exp(b_b - b_m)
        k_off = kx[:hb] * jnp.exp(b_m - b_a)
        q3 = jnp.concatenate([qx, q_off], axis=0).astype(_BF16)
        k3 = jnp.concatenate([kx, k_off, jnp.zeros((hb, w), _F32)], axis=0).astype(_BF16)
        dec = jnp.exp(b_l)
        iv_c = iv[rows]
        for p in range(n_heads // 2):
            pcols = slice(p * pair, (p + 1) * pair)
            a_heads = []
            for hd in (2 * p, 2 * p + 1):
                cols = slice(hd * HGRN_HEAD_DIM, (hd + 1) * HGRN_HEAD_DIM)
                sc = _dot_nt(q3[:, cols], k3[:, cols])
                top = jnp.where(top_mask, sc[0:hb, 0:c], 0.0)
                bot = jnp.where(left, sc[2 * hb:3 * hb, c:2 * c],
                                jnp.where(bot_mask, sc[hb:2 * hb, 0:c], 0.0))
                a_heads.append(jnp.concatenate([top, bot], axis=0))
            a_pair = jnp.concatenate(a_heads, axis=1).astype(_BF16)
            iv_p = iv_c[:, pcols]
            iv_blk = jnp.concatenate([jnp.where(up_rows, iv_p, 0.0).astype(_BF16),
                                      jnp.where(up_rows, 0.0, iv_p).astype(_BF16)], axis=0)
            st = st_ref[p]
            o = _dot(a_pair, iv_blk) + _dot_nt(qd[:, pcols], st.astype(_BF16))
            upd = _dot_tn(iv_p, kd[:, pcols])
            st_ref[p] = st * dec[:, pcols] + jnp.where(same_head, upd, 0.0)
            for j in range(2):
                hcols = slice(j * HGRN_HEAD_DIM, (j + 1) * HGRN_HEAD_DIM)
                ob_ref[rows, p * pair + j * HGRN_HEAD_DIM:p * pair + (j + 1) * HGRN_HEAD_DIM] = _rms(o[:, hcols])
    yb = (ob_ref[...] * ghg_ref[...]) * act_ref[a_og]

    y = act_ref[a_ga] * ya_ref[...] + act_ref[a_gb] * yb
    out = _dot(y.astype(_BF16), wout_ref[...])
    x1 = jnp.where(is_tile, x + gt1 * (_rms(out) * gpost_ref[...]), x1_prev)
    o_ref[0] = x1
    x1p_ref[...] = x1


def _mixer(x, ada, g_pre, g_post, w_in, ln_g, ln_b, w_spatial, b_spatial, lb, g_hgrn, w_out,
           g_pre_ffn, w_rg, b_rg, w_re, b_re):
    b, s, d = x.shape
    w = w_out.shape[0]
    n_groups, gchunk, _ = w_spatial.shape
    ts = min(MIXER_SEQ_TILE, s)
    assert s % ts == 0 and ts % gchunk == 0 and ts % HGRN_CHUNK == 0 and ts % LANES == 0
    n_rg, n_exp = w_rg.shape[1], w_re.shape[1]
    assert n_rg + n_exp <= LANES
    pad = LANES - n_rg - n_exp
    wrt = jnp.concatenate([w_rg, w_re, jnp.zeros((d, pad), _F32)], axis=1).T.astype(_BF16)
    brt = jnp.concatenate([b_rg, b_re, jnp.zeros((pad,), _F32)]).reshape(LANES, 1).astype(_F32)
    tiles = s // ts
    nt = b * tiles
    cur = lambda t: jnp.minimum(t, nt - 1)
    prev = lambda t: jnp.maximum(t - 1, 0)
    cur3 = lambda t: (cur(t), 0, 0)
    prev3 = lambda t: (prev(t), 0, 0)
    assert w % HGRN_HEAD_DIM == 0 and w // n_groups == LANES and w_in.shape == (d, N_IN_SLICES * w)
    assert w % PROJ_COLS == 0
    n_heads = w // HGRN_HEAD_DIM
    assert n_heads % 2 == 0 and HGRN_HEAD_DIM == LANES
    row = lambda a: a.reshape(1, -1).astype(_F32)
    const2 = lambda t: (0, 0)
    const3 = lambda t: (0, 0, 0)
    single = dict(pipeline_mode=pl.Buffered(1))
    x1, h2, meta, metat, cnt = pl.pallas_call(
        functools.partial(_mixer_kernel, n_route_groups=n_rg, per_group=n_exp // n_rg,
                          tiles_per_batch=tiles),
        grid=(nt + 1,),
        in_specs=[
            pl.BlockSpec((1, ts, d), cur3),
            pl.BlockSpec((1, ada.shape[1], d), lambda t: (cur(t) // tiles, 0, 0)),
            pl.BlockSpec((1, ada.shape[1], d), lambda t: (prev(t) // tiles, 0, 0)),
            pl.BlockSpec((1, d), const2),
            pl.BlockSpec((1, d), const2),
            pl.BlockSpec((d, N_IN_SLICES * w), const2, **single),
            pl.BlockSpec((1, w), const2),
            pl.BlockSpec((1, w), const2),
            pl.BlockSpec((n_groups, gchunk, gchunk), const3, **single),
            pl.BlockSpec((gchunk, n_groups), const2),
            pl.BlockSpec((1, w), const2),
            pl.BlockSpec((1, w), const2),
            pl.BlockSpec((w, d), const2, **single),
            pl.BlockSpec((1, d), const2),
            pl.BlockSpec((LANES, d), const2),
            pl.BlockSpec((LANES, 1), const2),
        ],
        out_specs=[
            pl.BlockSpec((1, ts, d), cur3),
            pl.BlockSpec((1, ts, d), prev3),
            pl.BlockSpec((1, ts, LANES), prev3),
            pl.BlockSpec((1, 8, ts), prev3),
            pl.BlockSpec((1, 8, LANES), prev3),
        ],
        out_shape=[
            jax.ShapeDtypeStruct((nt, ts, d), _F32),
            jax.ShapeDtypeStruct((nt, ts, d), _BF16),
            jax.ShapeDtypeStruct((nt, ts, LANES), _F32),
            jax.ShapeDtypeStruct((nt, 8, ts), _F32),
            jax.ShapeDtypeStruct((nt, 8, LANES), jnp.int32),
        ],
        scratch_shapes=[
            pltpu.VMEM((n_heads // 2, 2 * HGRN_HEAD_DIM, 2 * HGRN_HEAD_DIM), _F32),
            pltpu.VMEM((ts, w), _F32),
            pltpu.VMEM((ts, w), _F32),
            pltpu.VMEM((ts, d), _F32),
            pltpu.VMEM((N_IN_SLICES, ts, w), _F32),
            pltpu.VMEM((ts, w), _BF16),
            pltpu.VMEM((ts, w), _BF16),
        ],
        compiler_params=pltpu.CompilerParams(
            dimension_semantics=("arbitrary",),
            vmem_limit_bytes=VMEM_LIMIT_BYTES),
        name="mixer",
    )(x.reshape(nt, ts, d), ada, ada, row(g_pre), row(g_post), w_in.astype(_BF16), row(ln_g),
      row(ln_b), w_spatial.astype(_F32), b_spatial.T.astype(_F32), row(lb), row(g_hgrn),
      w_out.astype(_BF16), row(g_pre_ffn), wrt, brt)
    return x1.reshape(b, s, d), h2, meta, metat, cnt


def _first_max(vals):
    m = functools.reduce(jnp.maximum, vals)
    idx = jnp.full(m.shape, len(vals) - 1, jnp.int32)
    for j in range(len(vals) - 2, -1, -1):
        idx = jnp.where(vals[j] == m, j, idx)
    return m, idx


def _route(lt, n_groups, per_group):
    lg = [lt[g:g + 1] for g in range(n_groups)]
    mg, g_idx = _first_max(lg)
    p_top = 1.0 / functools.reduce(lambda a, b: a + b, [jnp.exp(l - mg) for l in lg])
    le = []
    for j in range(per_group):
        v = lt[n_groups + j:n_groups + j + 1]
        for g in range(1, n_groups):
            r0 = n_groups + g * per_group + j
            v = jnp.where(g_idx == g, lt[r0:r0 + 1], v)
        le.append(v)
    m1, j1 = _first_max(le)
    m2, j2 = _first_max([jnp.where(j1 == j, -jnp.inf, v) for j, v in enumerate(le)])
    r = jnp.exp(m2 - m1)
    w1 = p_top / (1.0 + r)
    w2 = w1 * r
    return g_idx * per_group + j1, g_idx * per_group + j2, w1, w2


def _route_tile(x, sh2, sc2, gpre, wrt, brt, h_ref, meta_ref, metat_ref, cnt_ref, n_groups, per_group):
    tm = x.shape[0]
    h = (_rms(x) * gpre) * (1.0 + sc2) + sh2
    hb = h.astype(_BF16)
    h_ref[0] = hb
    lt = _dot_nt(wrt, hb) + brt
    e1, e2, w1, w2 = _route(lt, n_groups, per_group)

    sub = lax.broadcasted_iota(jnp.int32, (LANES, tm), 0)
    ind = jnp.where((sub == e1) | (sub == e2), 1.0, 0.0)
    earlier = (lax.broadcasted_iota(jnp.int32, (tm, tm), 0)
               < lax.broadcasted_iota(jnp.int32, (tm, tm), 1))
    rank = _dot(ind.astype(_BF16), jnp.where(earlier, 1.0, 0.0).astype(_BF16))
    cnt = jnp.sum(ind, axis=1, keepdims=True)
    chunks = jnp.floor((cnt + (CHUNK_ROWS - 1)) * (1.0 / CHUNK_ROWS))
    below = (lax.broadcasted_iota(jnp.int32, (LANES, LANES), 1)
             < lax.broadcasted_iota(jnp.int32, (LANES, LANES), 0))
    first_chunk = _dot(jnp.where(below, 1.0, 0.0).astype(_BF16),
                       jnp.broadcast_to(chunks, (LANES, LANES)).astype(_BF16))
    dest = first_chunk[:, 0:1] * CHUNK_ROWS + rank
    n_rows = -(-n_groups * per_group // 8) * 8
    sub_e = lax.broadcasted_iota(jnp.int32, (n_rows, tm), 0)
    d1 = jnp.sum(jnp.where(sub_e == e1, dest[:n_rows], 0.0), axis=0, keepdims=True)
    d2 = jnp.sum(jnp.where(sub_e == e2, dest[:n_rows], 0.0), axis=0, keepdims=True)
    row8 = lax.broadcasted_iota(jnp.int32, (8, tm), 0)
    metat = jnp.where(row8 == 0, d1, jnp.where(row8 == 1, d2,
                      jnp.where(row8 == 2, w1, jnp.where(row8 == 3, w2, 0.0))))
    metat_ref[0] = metat
    meta_ref[0] = jnp.concatenate([metat, jnp.zeros((LANES - 8, tm), _F32)], axis=0).T
    cnt_ref[0] = jnp.broadcast_to(cnt, (LANES, LANES)).T[0:8].astype(jnp.int32)


def _chunk_loop(n, fn):
    def body(c, carry):
        fn(c)
        return carry
    lax.fori_loop(0, n, body, 0)


def _wait_chunks(n, max_chunks, copy_of_chunks):
    bit = 1
    while bit * 2 <= max_chunks:
        bit *= 2
    while bit >= 1:
        @pl.when((n & bit) != 0)
        def _(bit=bit):
            copy_of_chunks(bit).wait()
        bit //= 2


def _dispatch_kernel(nch_ref, cmap_ref, tbase_ref, ntail_ref, nu_ref, metat_ref, h_ref, xs_ref,
                     loc_ref, zero_ref, sem, zsem, *, max_chunks):
    i = pl.program_id(0)
    n_steps = pl.num_programs(0)
    g = h_ref.shape[0]
    lrows, tm = loc_ref.shape[1], h_ref.shape[1]
    buf = lambda step, j: (step % 2) * g + j

    def chunk_copy(slot_, c, dst_chunk):
        return pltpu.make_async_copy(
            loc_ref.at[slot_, pl.ds(pl.multiple_of(c * CHUNK_ROWS, CHUNK_ROWS), CHUNK_ROWS)],
            xs_ref.at[pl.ds(pl.multiple_of(dst_chunk * CHUNK_ROWS, CHUNK_ROWS), CHUNK_ROWS)],
            sem.at[slot_])

    def zero_copy(dst_chunk):
        return pltpu.make_async_copy(
            zero_ref.at[pl.ds(0, CHUNK_ROWS)],
            xs_ref.at[pl.ds(pl.multiple_of(dst_chunk * CHUNK_ROWS, CHUNK_ROWS), CHUNK_ROWS)],
            zsem.at[0])

    def zero_tile_copy(tile):
        return pltpu.make_async_copy(
            zero_ref,
            xs_ref.at[pl.ds(pl.multiple_of(tile * GMM_ROW_TILE, GMM_ROW_TILE), GMM_ROW_TILE)],
            zsem.at[1])

    @pl.when(i == 0)
    def _():
        zero_ref[...] = jnp.zeros(zero_ref.shape, zero_ref.dtype)
        n_unused = xs_ref.shape[0] // GMM_ROW_TILE - nu_ref[0]
        for e in range(tbase_ref.shape[0]):
            _chunk_loop(ntail_ref[e], lambda j, e=e: zero_copy(tbase_ref[e] + j).start())
        _chunk_loop(n_unused, lambda j: zero_tile_copy(nu_ref[0] + j).start())
        for e in range(tbase_ref.shape[0]):
            _chunk_loop(ntail_ref[e], lambda j: zero_copy(0).wait())
        _chunk_loop(n_unused, lambda j: zero_tile_copy(0).wait())

    def wait_slot(slot_, n_):
        _wait_chunks(n_, max_chunks, lambda k: pltpu.make_async_copy(
            loc_ref.at[slot_, pl.ds(0, k * CHUNK_ROWS)], xs_ref.at[pl.ds(0, k * CHUNK_ROWS)],
            sem.at[slot_]))

    @pl.when(i >= 2)
    def _():
        for j in range(g):
            wait_slot(buf(i, j), nch_ref[(i - 2) * g + j])

    r = lax.broadcasted_iota(jnp.int32, (lrows, tm), 0)
    lane = lax.broadcasted_iota(jnp.int32, (lrows, LANES), 1)
    d = h_ref.shape[2]
    for j in range(g):
        mt = metat_ref[j]
        is1 = r == mt[0:1].astype(jnp.int32)
        is2 = r == mt[1:2].astype(jnp.int32)
        sel = jnp.where(is1 | is2, 1.0, 0.0).astype(_BF16)
        loc_ref[buf(i, j), :, 0:d] = _dot(sel, h_ref[j]).astype(_BF16)
        ws = jnp.sum(jnp.where(is1, mt[2:3], 0.0) + jnp.where(is2, mt[3:4], 0.0),
                     axis=1, keepdims=True)
        hi, mid, lo = (p.astype(_F32) for p in _split3(ws))
        loc_ref[buf(i, j), :, d:d + LANES] = jnp.where(
            lane == 0, hi, jnp.where(lane == 1, mid, jnp.where(lane == 2, lo, 0.0))).astype(_BF16)
    for j in range(g):
        tile = i * g + j
        _chunk_loop(nch_ref[tile], lambda c, j=j, tile=tile: chunk_copy(
            buf(i, j), c, cmap_ref[tile * max_chunks + c]).start())

    @pl.when(i == n_steps - 1)
    def _():
        for j in range(g):
            wait_slot(buf(i, j), nch_ref[i * g + j])

        @pl.when(i >= 1)
        def _():
            for j in range(g):
                wait_slot(buf(i - 1, j), nch_ref[(i - 1) * g + j])


def _experts_kernel(te_ref, tf_ref, nu_ref, xs_ref, w1_ref, w3_ref, w2_ref, ys_ref,
                    wb1_ref, wb3_ref, wb2_ref):
    i = pl.program_id(0)

    @pl.when(i < nu_ref[0])
    def _():
        @pl.when(tf_ref[i] == 1)
        def _():
            wb1_ref[...] = w1_ref[0].astype(_BF16)
            wb3_ref[...] = w3_ref[0].astype(_BF16)
            wb2_ref[...] = w2_ref[0].astype(_BF16)

        d = wb1_ref.shape[0]
        xs = xs_ref[:, 0:d]
        half = wb1_ref.shape[1] // 2
        acc = None
        for hcols in (slice(0, half), slice(half, 2 * half)):
            a = _silu(_dot(xs, wb1_ref[:, hcols])) * _dot(xs, wb3_ref[:, hcols])
            part = _dot(a.astype(_BF16), wb2_ref[hcols, :])
            acc = part if acc is None else acc + part
        wp = xs_ref[:, d:d + LANES].astype(_F32)
        ys_ref[...] = (acc * (wp[:, 0:1] + wp[:, 1:2] + wp[:, 2:3])).astype(_BF16)

    @pl.when(i >= nu_ref[0])
    def _():
        ys_ref[...] = jnp.zeros(ys_ref.shape, ys_ref.dtype)


def _combine_kernel(nch_ref, cmap_ref, x_ref, ada_ref, gpost_ref, meta_ref, ys_ref, o_ref,
                    loc_ref, sem, *, max_chunks):
    i = pl.program_id(0)
    n_steps = pl.num_programs(0)
    g = x_ref.shape[0]
    lrows, tm = loc_ref.shape[1], x_ref.shape[1]
    buf = lambda step, j: (step % 2) * g + j

    def chunk_copy(tile, slot_, c):
        src_chunk = cmap_ref[tile * max_chunks + c]
        return pltpu.make_async_copy(
            ys_ref.at[pl.ds(pl.multiple_of(src_chunk * CHUNK_ROWS, CHUNK_ROWS), CHUNK_ROWS)],
            loc_ref.at[slot_, pl.ds(pl.multiple_of(c * CHUNK_ROWS, CHUNK_ROWS), CHUNK_ROWS)],
            sem.at[slot_])

    def fetch(step):
        for j in range(g):
            tile = step * g + j
            _chunk_loop(nch_ref[tile], lambda c, j=j, tile=tile: chunk_copy(tile, buf(step, j), c).start())

    @pl.when(i == 0)
    def _():
        loc_ref[...] = jnp.zeros(loc_ref.shape, loc_ref.dtype)
        fetch(0)

    @pl.when(i + 1 < n_steps)
    def _():
        fetch(i + 1)

    for j in range(g):
        _wait_chunks(nch_ref[i * g + j], max_chunks, lambda k, j=j: pltpu.make_async_copy(
            ys_ref.at[pl.ds(0, k * CHUNK_ROWS)], loc_ref.at[buf(i, j), pl.ds(0, k * CHUNK_ROWS)],
            sem.at[buf(i, j)]))

    r = lax.broadcasted_iota(jnp.int32, (tm, lrows), 1)
    gt2 = ada_ref[0][5:6]
    for j in range(g):
        meta = meta_ref[j]
        d1 = meta[:, 0:1].astype(jnp.int32)
        d2 = meta[:, 1:2].astype(jnp.int32)
        sel = jnp.where((r == d1) | (r == d2), 1.0, 0.0).astype(_BF16)
        y = _dot(sel, loc_ref[buf(i, j)])
        o_ref[j] = x_ref[j] + gt2 * (_rms(y) * gpost_ref[...])


def _moe_plan(cnt, max_chunks, n_gmm_tiles):
    per_tile = GMM_ROW_TILE // CHUNK_ROWS
    pc = (cnt + (CHUNK_ROWS - 1)) // CHUNK_ROWS
    local_first = jnp.cumsum(pc, axis=1) - pc
    nch = jnp.sum(pc, axis=1)
    e_chunks = jnp.sum(pc, axis=0)
    e_region = ((e_chunks + per_tile - 1) // per_tile) * per_tile
    e_first = jnp.cumsum(e_region) - e_region
    seg_first = e_first[None, :] + jnp.cumsum(pc, axis=0) - pc
    c = jnp.arange(max_chunks, dtype=jnp.int32)[None, :, None]
    inside = (c >= local_first[:, None, :]) & (c < (local_first + pc)[:, None, :])
    cmap = jnp.sum(jnp.where(inside, seg_first[:, None, :] + c - local_first[:, None, :], 0), axis=-1)
    n_used = jnp.sum(e_region) // per_tile
    t = jnp.arange(n_gmm_tiles, dtype=jnp.int32)
    t_used = jnp.minimum(t, n_used - 1)
    tile_e = jnp.sum(t_used[:, None] * per_tile >= (e_first + e_region)[None, :], axis=1)
    tile_first = jnp.concatenate([jnp.ones((1,), jnp.int32),
                                  (tile_e[1:] != tile_e[:-1]).astype(jnp.int32)])
    i32 = lambda a: a.astype(jnp.int32)
    return dict(nch=i32(nch), cmap=i32(cmap.reshape(-1)), tail_first=i32(e_first + e_chunks),
                n_tail=i32(e_region - e_chunks), tile_e=i32(tile_e), tile_first=tile_first,
                n_used=i32(n_used.reshape(1)))


def _moe(x, h2, meta, metat, cnt, ada, g_post, n_groups, w1, w3, w2):
    b, s, d = x.shape
    n_exp, _, ff = w1.shape
    tm = h2.shape[1]
    assert s % tm == 0 and tm % LANES == 0 and GMM_ROW_TILE % CHUNK_ROWS == 0
    tiles_per_batch = s // tm
    nt = b * tiles_per_batch
    max_chunks = (TOP_K_IN_GROUP * tm + n_exp * (CHUNK_ROWS - 1)) // CHUNK_ROWS
    lrows = -(-max_chunks * CHUNK_ROWS // LANES) * LANES
    per_tile = GMM_ROW_TILE // CHUNK_ROWS
    n_gmm_tiles = -(-(nt * max_chunks + n_exp * (per_tile - 1)) // per_tile)
    n_rows = n_gmm_tiles * GMM_ROW_TILE
    dx = d + LANES

    row = lambda a: a.reshape(1, -1).astype(_F32)
    n_ada = ada.shape[1]
    xt = x.reshape(nt, tm, d)
    g = MOE_TILES_PER_STEP if tiles_per_batch % MOE_TILES_PER_STEP == 0 else 1
    steps_per_batch = tiles_per_batch // g

    plan = _moe_plan(cnt[:, 0, :n_exp], max_chunks, n_gmm_tiles)

    xs = pl.pallas_call(
        functools.partial(_dispatch_kernel, max_chunks=max_chunks),
        grid_spec=pltpu.PrefetchScalarGridSpec(
            num_scalar_prefetch=5,
            grid=(nt // g,),
            in_specs=[
                pl.BlockSpec((g, 8, tm), lambda i, *_: (i, 0, 0)),
                pl.BlockSpec((g, tm, d), lambda i, *_: (i, 0, 0)),
            ],
            out_specs=pl.BlockSpec(memory_space=pl.ANY),
            scratch_shapes=[
                pltpu.VMEM((2 * g, lrows, dx), _BF16),
                pltpu.VMEM((GMM_ROW_TILE, dx), _BF16),
                pltpu.SemaphoreType.DMA((2 * g,)),
                pltpu.SemaphoreType.DMA((2,)),
            ],
        ),
        out_shape=jax.ShapeDtypeStruct((n_rows, dx), _BF16),
        compiler_params=pltpu.CompilerParams(dimension_semantics=("arbitrary",)),
        name="dispatch",
    )(plan["nch"], plan["cmap"], plan["tail_first"], plan["n_tail"], plan["n_used"], metat, h2)

    ys = pl.pallas_call(
        _experts_kernel,
        grid_spec=pltpu.PrefetchScalarGridSpec(
            num_scalar_prefetch=3,
            grid=(n_gmm_tiles,),
            in_specs=[
                pl.BlockSpec((GMM_ROW_TILE, dx), lambda i, te, tf, nu: (jnp.minimum(i, nu[0] - 1), 0)),
                pl.BlockSpec((1, d, ff), lambda i, te, tf, nu: (te[i], 0, 0)),
                pl.BlockSpec((1, d, ff), lambda i, te, tf, nu: (te[i], 0, 0)),
                pl.BlockSpec((1, ff, d), lambda i, te, tf, nu: (te[i], 0, 0)),
            ],
            out_specs=pl.BlockSpec((GMM_ROW_TILE, d), lambda i, te, tf, nu: (i, 0)),
            scratch_shapes=[
                pltpu.VMEM((d, ff), _BF16),
                pltpu.VMEM((d, ff), _BF16),
                pltpu.VMEM((ff, d), _BF16),
            ],
        ),
        out_shape=jax.ShapeDtypeStruct((n_rows, d), _BF16),
        compiler_params=pltpu.CompilerParams(dimension_semantics=("arbitrary",),
                                             vmem_limit_bytes=VMEM_LIMIT_BYTES),
        name="experts",
    )(plan["tile_e"], plan["tile_first"], plan["n_used"], xs, w1, w3, w2)

    out = pl.pallas_call(
        functools.partial(_combine_kernel, max_chunks=max_chunks),
        grid_spec=pltpu.PrefetchScalarGridSpec(
            num_scalar_prefetch=2,
            grid=(nt // g,),
            in_specs=[
                pl.BlockSpec((g, tm, d), lambda i, *_: (i, 0, 0)),
                pl.BlockSpec((1, n_ada, d), lambda i, *_: (i // steps_per_batch, 0, 0)),
                pl.BlockSpec((1, d), lambda i, *_: (0, 0)),
                pl.BlockSpec((g, tm, LANES), lambda i, *_: (i, 0, 0)),
                pl.BlockSpec(memory_space=pl.ANY),
            ],
            out_specs=pl.BlockSpec((g, tm, d), lambda i, *_: (i, 0, 0)),
            scratch_shapes=[
                pltpu.VMEM((2 * g, lrows, d), _BF16),
                pltpu.SemaphoreType.DMA((2 * g,)),
            ],
        ),
        out_shape=jax.ShapeDtypeStruct((nt, tm, d), _F32),
        compiler_params=pltpu.CompilerParams(dimension_semantics=("arbitrary",)),
        name="combine",
    )(plan["nch"], plan["cmap"], xt, ada, row(g_post), meta, ys)
    return out.reshape(b, s, d)


def kernel(x, c, w_ada, b_ada, g_pre_mix, g_post_mix, w_in, ln_v_g, ln_v_b, w_spatial, b_spatial,
           lb_logits, g_hgrn_norm, w_out, g_pre_ffn, g_post_ffn, w_router_group, b_router_group,
           w_router_expert, b_router_expert, w1, w3, w2):
    depth = w_in.shape[0]
    b, s, d = x.shape
    lb_all = _lower_bounds(lb_logits)
    for layer in range(depth):
        ada = _ada(c, w_ada[layer], b_ada[layer]).reshape(b, 6, d)
        x, h2, meta, metat, cnt = _mixer(
            x, ada, g_pre_mix[layer], g_post_mix[layer], w_in[layer], ln_v_g[layer], ln_v_b[layer],
            w_spatial[layer], b_spatial[layer], lb_all[layer], g_hgrn_norm[layer], w_out[layer],
            g_pre_ffn[layer], w_router_group[layer], b_router_group[layer],
            w_router_expert[layer], b_router_expert[layer])
        x = _moe(x, h2, meta, metat, cnt, ada, g_post_ffn[layer], w_router_group.shape[-1],
                 w1[layer], w3[layer], w2[layer])
    return x
```

```python
import functools

import jax
import jax.numpy as jnp
from jax import lax
from jax.experimental import pallas as pl
from jax.experimental.pallas import tpu as pltpu

RMS_EPS = 1e-6
LN_EPS = 1e-5
HGRN_HEAD_DIM = 128
HGRN_CHUNK = 128
HGRN_MAX_SPAN = 60.0
N_IN_SLICES = 8
TOP_K_IN_GROUP = 2
LANES = 128
MIXER_SEQ_TILE = 256
PROJ_COLS = 256
CHUNK_ROWS = 16
GMM_ROW_TILE = 1024
MOE_TILES_PER_STEP = 4
VMEM_LIMIT_BYTES = 56 * 1024 * 1024

_F32 = jnp.float32
_BF16 = jnp.bfloat16


def _sigmoid(x):
    return 0.5 * (jnp.tanh(0.5 * x) + 1.0)


def _silu(x):
    return x * _sigmoid(x)


def _gelu_tanh(x):
    c = 0.7978845608028654
    return 0.5 * x * (1.0 + jnp.tanh(c * (x + 0.044715 * (x * x * x))))


def _rms(x):
    return x * lax.rsqrt(jnp.mean(x * x, axis=-1, keepdims=True) + RMS_EPS)


def _dot(a, b):
    return jnp.dot(a, b, preferred_element_type=_F32)


def _dot_nt(a, b):
    return lax.dot_general(a, b, (((1,), (1,)), ((), ())), preferred_element_type=_F32)


def _dot_tn(a, b):
    return lax.dot_general(a, b, (((0,), (0,)), ((), ())), preferred_element_type=_F32)


def _split3(x):
    hi = x.astype(_BF16)
    r = x - hi.astype(_F32)
    mid = r.astype(_BF16)
    lo = (r - mid.astype(_F32)).astype(_BF16)
    return hi, mid, lo


def _ada_kernel(c_ref, w_ref, b_ref, o_ref):
    s = _silu(c_ref[...])
    w = w_ref[...]
    acc = jnp.zeros(o_ref.shape, _F32)
    for sp in _split3(s):
        for wp in _split3(w)[:2]:
            acc = acc + _dot(sp, wp)
    o_ref[...] = acc + b_ref[...]


def _ada(c, w_ada, b_ada):
    b, d = c.shape
    n = w_ada.shape[1]
    tn = d
    return pl.pallas_call(
        _ada_kernel,
        grid=(n // tn,),
        in_specs=[
            pl.BlockSpec((b, d), lambda j: (0, 0)),
            pl.BlockSpec((d, tn), lambda j: (0, j)),
            pl.BlockSpec((1, tn), lambda j: (0, j)),
        ],
        out_specs=pl.BlockSpec((b, tn), lambda j: (0, j)),
        out_shape=jax.ShapeDtypeStruct((b, n), _F32),
        name="ada",
    )(c, w_ada, b_ada.reshape(1, n))


def _lb_kernel(l_ref, o_ref):
    l = l_ref[...]
    e = jnp.exp(l - jnp.max(l, axis=0, keepdims=True))
    p = e / jnp.sum(e, axis=0, keepdims=True)
    rows = [p[0:1]]
    for i in range(1, p.shape[0]):
        rows.append(rows[-1] + p[i:i + 1])
    for i, r in enumerate(rows):
        o_ref[i:i + 1, :] = r


def _lower_bounds(lb_logits):
    return pl.pallas_call(
        _lb_kernel,
        out_shape=jax.ShapeDtypeStruct(lb_logits.shape, _F32),
        name="lower_bounds",
    )(lb_logits.astype(_F32))


def _mixer_kernel(x_ref, ada_ref, adap_ref, gpre_ref, gpost_ref, win_ref, lng_ref, lnb_ref, wsp_ref,
                  bspt_ref, lb_ref, ghg_ref, wout_ref, gpre2_ref, wrt_ref, brt_ref,
                  o_ref, h2_ref, meta_ref, metat_ref, cnt_ref, st_ref, ya_ref, ob_ref, x1p_ref,
                  act_ref, iv_ref, vn_ref, st0_ref,
                  *, n_route_groups, per_group, tiles_per_batch):
    ts, d = x_ref.shape[1], x_ref.shape[2]
    w = wout_ref.shape[0]
    n_groups, gchunk = wsp_ref.shape[0], wsp_ref.shape[1]
    gdim = w // n_groups
    n_heads = w // HGRN_HEAD_DIM
    c = HGRN_CHUNK
    t = pl.program_id(0)
    is_tile = t < pl.num_programs(0) - 1

    @pl.when(t % tiles_per_batch == 0)
    def _():
        st_ref[...] = jnp.zeros(st_ref.shape, _F32)

    @pl.when(t == 0)
    def _():
        x1p_ref[...] = jnp.zeros(x1p_ref.shape, _F32)

    x1_prev = x1p_ref[...]
    adap = adap_ref[0]
    _route_tile(x1_prev, adap[3:4], adap[4:5], gpre2_ref[...], wrt_ref[...], brt_ref[...],
                h2_ref, meta_ref, metat_ref, cnt_ref, n_route_groups, per_group)

    x = x_ref[0]
    ada = ada_ref[0]
    sh1, sc1, gt1 = ada[0:1], ada[1:2], ada[2:3]
    h = (_rms(x) * gpre_ref[...]) * (1.0 + sc1) + sh1
    h_bf = h.astype(_BF16)

    lb = lb_ref[...]
    (s_v, s_u, s_q, s_f, s_i, s_og, s_ga, s_gb) = (1, 0, 2, 3, 4, 5, 6, 7)
    (a_u, a_v, a_q, a_lf, a_k, a_og, a_ga, a_gb) = range(8)

    def activate(j, cols, r):
        if j == s_u:
            act_ref[a_u, :, cols] = _gelu_tanh(r)
        elif j == s_v:
            act_ref[a_v, :, cols] = _gelu_tanh(r)
        elif j == s_q:
            act_ref[a_q, :, cols] = _silu(r)
        elif j == s_f:
            f = lb[:, cols] + (1.0 - lb[:, cols]) * _sigmoid(r)
            act_ref[a_lf, :, cols] = jnp.log(f)
            act_ref[a_k, :, cols] = 1.0 - f
        elif j == s_i:
            iv_ref[:, cols] = r.astype(_BF16)
        elif j == s_og:
            act_ref[a_og, :, cols] = _silu(r)
        elif j == s_ga:
            act_ref[a_ga, :, cols] = _sigmoid(r)
        else:
            act_ref[a_gb, :, cols] = _sigmoid(r)

    def layer_norm_v():
        v = act_ref[a_v]
        mu = jnp.mean(v, axis=-1, keepdims=True)
        vc = v - mu
        var = jnp.mean(vc * vc, axis=-1, keepdims=True)
        vn_ref[...] = ((vc * lax.rsqrt(var + LN_EPS)) * lng_ref[...] + lnb_ref[...]).astype(_BF16)

    order = (s_v, s_u, s_q, s_f, s_i, s_og, s_ga, s_gb)
    pending = None
    for pos, j in enumerate(order):
        for cb in range(w // PROJ_COLS):
            cols = slice(cb * PROJ_COLS, (cb + 1) * PROJ_COLS)
            r = _dot(h_bf, win_ref[:, j * w + cb * PROJ_COLS:j * w + (cb + 1) * PROJ_COLS])
            if pending is not None:
                pending()
            pending = functools.partial(activate, j, cols, r)
            if pos == 1 and cb == 0:
                layer_norm_v()
    pending()

    u = act_ref[a_u]
    vn = vn_ref[...]
    tri_g = (lax.broadcasted_iota(jnp.int32, (gchunk, gchunk), 0)
             >= lax.broadcasted_iota(jnp.int32, (gchunk, gchunk), 1))
    bspt = bspt_ref[...]
    n_gchunks = ts // gchunk
    for g in range(n_groups):
        wc = jnp.where(tri_g, wsp_ref[g], 0.0).astype(_BF16)
        cols = slice(g * gdim, (g + 1) * gdim)
        vg = jnp.concatenate([vn[n * gchunk:(n + 1) * gchunk, cols] for n in range(n_gchunks)], axis=1)
        zv = _dot(wc, vg)
        for n in range(n_gchunks):
            rows = slice(n * gchunk, (n + 1) * gchunk)
            ya_ref[rows, cols] = u[rows, cols] * (zv[:, n * gdim:(n + 1) * gdim] + bspt[:, g:g + 1])

    q = act_ref[a_q]
    lf = act_ref[a_lf]
    k = act_ref[a_k]
    iv = iv_ref[...]
    r_i = lax.broadcasted_iota(jnp.int32, (ts, ts), 0)
    c_i = lax.broadcasted_iota(jnp.int32, (ts, ts), 1)
    ltri = jnp.where((r_i // c == c_i // c) & (c_i <= r_i), 1.0, 0.0).astype(_BF16)
    bcum = jnp.zeros((ts, w), _F32)
    for part in _split3(lf):
        bcum = bcum + _dot(ltri, part)
    hb = c // 2
    row_h = lax.broadcasted_iota(jnp.int32, (hb, c), 0)
    lane_h = lax.broadcasted_iota(jnp.int32, (hb, c), 1)
    top_mask = lane_h <= row_h
    left = lane_h < hb
    bot_mask = lane_h - hb <= row_h
    pair = 2 * HGRN_HEAD_DIM
    pr = lax.broadcasted_iota(jnp.int32, (pair, pair), 0)
    pc_i = lax.broadcasted_iota(jnp.int32, (pair, pair), 1)
    same_head = (pr < HGRN_HEAD_DIM) == (pc_i < HGRN_HEAD_DIM)
    up_rows = lax.broadcasted_iota(jnp.int32, (c, pair), 1) < HGRN_HEAD_DIM
    st0_ref[...] = st_ref[...]
    span = jnp.zeros((1, 1), _F32)
    for n in range(ts // c):
        rows = slice(n * c, (n + 1) * c)
        bc = bcum[rows]
        b_a, b_m, b_b, b_l = (bc[hb // 2 - 1:hb // 2], bc[hb - 1:hb],
                              bc[hb + hb // 2 - 1:hb + hb // 2], bc[c - 1:c])
        stack = lambda ra, rb: jnp.concatenate(
            [jnp.broadcast_to(ra, (hb, w)), jnp.broadcast_to(rb, (hb, w))], axis=0)
        ref = stack(b_a, b_b)
        ends = jnp.maximum(jnp.maximum(bc[0:1] - b_a, b_a - b_m),
                           jnp.maximum(bc[hb:hb + 1] - b_b, b_b - b_l))
        span = jnp.maximum(span, jnp.max(ends, axis=1, keepdims=True))
        qx = q[rows] * jnp.exp(bc - ref)
        kx = k[rows] * jnp.exp(ref - bc)
        qd = (qx * stack(jnp.exp(b_a), jnp.exp(b_b))).astype(_BF16)
        kd = (kx * stack(jnp.exp(b_l - b_a), jnp.exp(b_l - b_b))).astype(_BF16)
        q_off = qx[hb:] * jnp.exp(b_b - b_m)
        k_off = kx[:hb] * jnp.exp(b_m - b_a)
        q3 = jnp.concatenate([qx, q_off], axis=0).astype(_BF16)
        k3 = jnp.concatenate([kx, k_off, jnp.zeros((hb, w), _F32)], axis=0).astype(_BF16)
        dec = jnp.exp(b_l)
        iv_c = iv[rows]
        for p in range(n_heads // 2):
            pcols = slice(p * pair, (p + 1) * pair)
            a_heads = []
            for hd in (2 * p, 2 * p + 1):
                cols = slice(hd * HGRN_HEAD_DIM, (hd + 1) * HGRN_HEAD_DIM)
                sc = _dot_nt(q3[:, cols], k3[:, cols])
                top = jnp.where(top_mask, sc[0:hb, 0:c], 0.0)
                bot = jnp.where(left, sc[2 * hb:3 * hb, c:2 * c],
                                jnp.where(bot_mask, sc[hb:2 * hb, 0:c], 0.0))
                a_heads.append(jnp.concatenate([top, bot], axis=0))
            a_pair = jnp.concatenate(a_heads, axis=1).astype(_BF16)
            iv_p = iv_c[:, pcols]
            iv_blk = jnp.concatenate([jnp.where(up_rows, iv_p, 0.0).astype(_BF16),
                                      jnp.where(up_rows, 0.0, iv_p).astype(_BF16)], axis=0)
            st = st_ref[p]
            o = _dot(a_pair, iv_blk) + _dot_nt(qd[:, pcols], st.astype(_BF16))
            upd = _dot_tn(iv_p, kd[:, pcols])
            st_ref[p] = st * dec[:, pcols] + jnp.where(same_head, upd, 0.0)
            for j in range(2):
                hcols = slice(j * HGRN_HEAD_DIM, (j + 1) * HGRN_HEAD_DIM)
                ob_ref[rows, p * pair + j * HGRN_HEAD_DIM:p * pair + (j + 1) * HGRN_HEAD_DIM] = _rms(o[:, hcols])
    def finish(x_in, gate, keep):
        yb = (ob_ref[...] * ghg_ref[...]) * act_ref[a_og]
        y = act_ref[a_ga] * ya_ref[...] + act_ref[a_gb] * yb
        out = _dot(y.astype(_BF16), wout_ref[...])
        x1 = x_in + gate * (_rms(out) * gpost_ref[...])
        if keep is not None:
            x1 = jnp.where(is_tile, x1, keep)
        o_ref[0] = x1
        x1p_ref[...] = x1

    finish(x, gt1, x1_prev)

    @pl.when((span[0, 0] > HGRN_MAX_SPAN) & is_tile)
    def _():
        st_ref[...] = st0_ref[...]
        sub = 16

        def group(gi, carry):
            r0 = pl.multiple_of(gi * sub, sub)
            q_g = act_ref[a_q, pl.ds(r0, sub), :]
            k_g = act_ref[a_k, pl.ds(r0, sub), :]
            v_g = iv_ref[pl.ds(r0, sub), :].astype(_F32)
            o_rows = []
            for i in range(sub):
                q_t, k_t, v_t = q_g[i:i + 1], k_g[i:i + 1], v_g[i:i + 1]
                o_t = []
                for p in range(n_heads // 2):
                    pcols = slice(p * pair, (p + 1) * pair)
                    upd = _dot_tn(v_t[:, pcols].astype(_BF16), k_t[:, pcols].astype(_BF16))
                    st = st_ref[p] * (1.0 - k_t[:, pcols]) + jnp.where(same_head, upd, 0.0)
                    st_ref[p] = st
                    o_p = _dot_nt(q_t[:, pcols].astype(_BF16), st.astype(_BF16))
                    o_t += [_rms(o_p[:, :HGRN_HEAD_DIM]), _rms(o_p[:, HGRN_HEAD_DIM:])]
                o_rows.append(jnp.concatenate(o_t, axis=1))
            ob_ref[pl.ds(r0, sub), :] = jnp.concatenate(o_rows, axis=0)
            return carry

        lax.fori_loop(0, ts // sub, group, 0)
        finish(x_ref[0], ada_ref[0][2:3], None)


def _mixer(x, ada, g_pre, g_post, w_in, ln_g, ln_b, w_spatial, b_spatial, lb, g_hgrn, w_out,
           g_pre_ffn, w_rg, b_rg, w_re, b_re):
    b, s, d = x.shape
    w = w_out.shape[0]
    n_groups, gchunk, _ = w_spatial.shape
    ts = min(MIXER_SEQ_TILE, s)
    assert s % ts == 0 and ts % gchunk == 0 and ts % HGRN_CHUNK == 0 and ts % LANES == 0
    n_rg, n_exp = w_rg.shape[1], w_re.shape[1]
    assert n_rg + n_exp <= LANES
    pad = LANES - n_rg - n_exp
    wrt = jnp.concatenate([w_rg, w_re, jnp.zeros((d, pad), _F32)], axis=1).T.astype(_BF16)
    brt = jnp.concatenate([b_rg, b_re, jnp.zeros((pad,), _F32)]).reshape(LANES, 1).astype(_F32)
    tiles = s // ts
    nt = b * tiles
    cur = lambda t: jnp.minimum(t, nt - 1)
    prev = lambda t: jnp.maximum(t - 1, 0)
    cur3 = lambda t: (cur(t), 0, 0)
    prev3 = lambda t: (prev(t), 0, 0)
    assert w % HGRN_HEAD_DIM == 0 and w // n_groups == LANES and w_in.shape == (d, N_IN_SLICES * w)
    assert w % PROJ_COLS == 0
    n_heads = w // HGRN_HEAD_DIM
    assert n_heads % 2 == 0 and HGRN_HEAD_DIM == LANES
    row = lambda a: a.reshape(1, -1).astype(_F32)
    const2 = lambda t: (0, 0)
    const3 = lambda t: (0, 0, 0)
    single = dict(pipeline_mode=pl.Buffered(1))
    x1, h2, meta, metat, cnt = pl.pallas_call(
        functools.partial(_mixer_kernel, n_route_groups=n_rg, per_group=n_exp // n_rg,
                          tiles_per_batch=tiles),
        grid=(nt + 1,),
        in_specs=[
            pl.BlockSpec((1, ts, d), cur3),
            pl.BlockSpec((1, ada.shape[1], d), lambda t: (cur(t) // tiles, 0, 0)),
            pl.BlockSpec((1, ada.shape[1], d), lambda t: (prev(t) // tiles, 0, 0)),
            pl.BlockSpec((1, d), const2),
            pl.BlockSpec((1, d), const2),
            pl.BlockSpec((d, N_IN_SLICES * w), const2, **single),
            pl.BlockSpec((1, w), const2),
            pl.BlockSpec((1, w), const2),
            pl.BlockSpec((n_groups, gchunk, gchunk), const3, **single),
            pl.BlockSpec((gchunk, n_groups), const2),
            pl.BlockSpec((1, w), const2),
            pl.BlockSpec((1, w), const2),
            pl.BlockSpec((w, d), const2, **single),
            pl.BlockSpec((1, d), const2),
            pl.BlockSpec((LANES, d), const2),
            pl.BlockSpec((LANES, 1), const2),
        ],
        out_specs=[
            pl.BlockSpec((1, ts, d), cur3),
            pl.BlockSpec((1, ts, d), prev3),
            pl.BlockSpec((1, ts, LANES), prev3),
            pl.BlockSpec((1, 8, ts), prev3),
            pl.BlockSpec((1, 8, LANES), prev3),
        ],
        out_shape=[
            jax.ShapeDtypeStruct((nt, ts, d), _F32),
            jax.ShapeDtypeStruct((nt, ts, d), _BF16),
            jax.ShapeDtypeStruct((nt, ts, LANES), _F32),
            jax.ShapeDtypeStruct((nt, 8, ts), _F32),
            jax.ShapeDtypeStruct((nt, 8, LANES), jnp.int32),
        ],
        scratch_shapes=[
            pltpu.VMEM((n_heads // 2, 2 * HGRN_HEAD_DIM, 2 * HGRN_HEAD_DIM), _F32),
            pltpu.VMEM((ts, w), _F32),
            pltpu.VMEM((ts, w), _F32),
            pltpu.VMEM((ts, d), _F32),
            pltpu.VMEM((N_IN_SLICES, ts, w), _F32),
            pltpu.VMEM((ts, w), _BF16),
            pltpu.VMEM((ts, w), _BF16),
            pltpu.VMEM((n_heads // 2, 2 * HGRN_HEAD_DIM, 2 * HGRN_HEAD_DIM), _F32),
        ],
        compiler_params=pltpu.CompilerParams(
            dimension_semantics=("arbitrary",),
            vmem_limit_bytes=VMEM_LIMIT_BYTES),
        name="mixer",
    )(x.reshape(nt, ts, d), ada, ada, row(g_pre), row(g_post), w_in.astype(_BF16), row(ln_g),
      row(ln_b), w_spatial.astype(_F32), b_spatial.T.astype(_F32), row(lb), row(g_hgrn),
      w_out.astype(_BF16), row(g_pre_ffn), wrt, brt)
    return x1.reshape(b, s, d), h2, meta, metat, cnt


def _first_max(vals):
    m = functools.reduce(jnp.maximum, vals)
    idx = jnp.full(m.shape, len(vals) - 1, jnp.int32)
    for j in range(len(vals) - 2, -1, -1):
        idx = jnp.where(vals[j] == m, j, idx)
    return m, idx


def _route(lt, n_groups, per_group):
    lg = [lt[g:g + 1] for g in range(n_groups)]
    mg, g_idx = _first_max(lg)
    p_top = 1.0 / functools.reduce(lambda a, b: a + b, [jnp.exp(l - mg) for l in lg])
    le = []
    for j in range(per_group):
        v = lt[n_groups + j:n_groups + j + 1]
        for g in range(1, n_groups):
            r0 = n_groups + g * per_group + j
            v = jnp.where(g_idx == g, lt[r0:r0 + 1], v)
        le.append(v)
    m1, j1 = _first_max(le)
    m2, j2 = _first_max([jnp.where(j1 == j, -jnp.inf, v) for j, v in enumerate(le)])
    r = jnp.exp(m2 - m1)
    w1 = p_top / (1.0 + r)
    w2 = w1 * r
    return g_idx * per_group + j1, g_idx * per_group + j2, w1, w2


def _route_tile(x, sh2, sc2, gpre, wrt, brt, h_ref, meta_ref, metat_ref, cnt_ref, n_groups, per_group):
    tm = x.shape[0]
    h = (_rms(x) * gpre) * (1.0 + sc2) + sh2
    hb = h.astype(_BF16)
    h_ref[0] = hb
    lt = _dot_nt(wrt, hb) + brt
    e1, e2, w1, w2 = _route(lt, n_groups, per_group)

    sub = lax.broadcasted_iota(jnp.int32, (LANES, tm), 0)
    ind = jnp.where((sub == e1) | (sub == e2), 1.0, 0.0)
    earlier = (lax.broadcasted_iota(jnp.int32, (tm, tm), 0)
               < lax.broadcasted_iota(jnp.int32, (tm, tm), 1))
    rank = _dot(ind.astype(_BF16), jnp.where(earlier, 1.0, 0.0).astype(_BF16))
    cnt = jnp.sum(ind, axis=1, keepdims=True)
    chunks = jnp.floor((cnt + (CHUNK_ROWS - 1)) * (1.0 / CHUNK_ROWS))
    below = (lax.broadcasted_iota(jnp.int32, (LANES, LANES), 1)
             < lax.broadcasted_iota(jnp.int32, (LANES, LANES), 0))
    first_chunk = _dot(jnp.where(below, 1.0, 0.0).astype(_BF16),
                       jnp.broadcast_to(chunks, (LANES, LANES)).astype(_BF16))
    dest = first_chunk[:, 0:1] * CHUNK_ROWS + rank
    n_rows = -(-n_groups * per_group // 8) * 8
    sub_e = lax.broadcasted_iota(jnp.int32, (n_rows, tm), 0)
    d1 = jnp.sum(jnp.where(sub_e == e1, dest[:n_rows], 0.0), axis=0, keepdims=True)
    d2 = jnp.sum(jnp.where(sub_e == e2, dest[:n_rows], 0.0), axis=0, keepdims=True)
    row8 = lax.broadcasted_iota(jnp.int32, (8, tm), 0)
    metat = jnp.where(row8 == 0, d1, jnp.where(row8 == 1, d2,
                      jnp.where(row8 == 2, w1, jnp.where(row8 == 3, w2, 0.0))))
    metat_ref[0] = metat
    meta_ref[0] = jnp.concatenate([metat, jnp.zeros((LANES - 8, tm), _F32)], axis=0).T
    cnt_ref[0] = jnp.broadcast_to(cnt, (LANES, LANES)).T[0:8].astype(jnp.int32)


def _chunk_loop(n, fn):
    def body(c, carry):
        fn(c)
        return carry
    lax.fori_loop(0, n, body, 0)


def _wait_chunks(n, max_chunks, copy_of_chunks):
    bit = 1
    while bit * 2 <= max_chunks:
        bit *= 2
    while bit >= 1:
        @pl.when((n & bit) != 0)
        def _(bit=bit):
            copy_of_chunks(bit).wait()
        bit //= 2


def _dispatch_kernel(nch_ref, cmap_ref, tbase_ref, ntail_ref, nu_ref, metat_ref, h_ref, xs_ref,
                     loc_ref, zero_ref, sem, zsem, *, max_chunks):
    i = pl.program_id(0)
    n_steps = pl.num_programs(0)
    g = h_ref.shape[0]
    lrows, tm = loc_ref.shape[1], h_ref.shape[1]
    buf = lambda step, j: (step % 2) * g + j

    def chunk_copy(slot_, c, dst_chunk):
        return pltpu.make_async_copy(
            loc_ref.at[slot_, pl.ds(pl.multiple_of(c * CHUNK_ROWS, CHUNK_ROWS), CHUNK_ROWS)],
            xs_ref.at[pl.ds(pl.multiple_of(dst_chunk * CHUNK_ROWS, CHUNK_ROWS), CHUNK_ROWS)],
            sem.at[slot_])

    def zero_copy(dst_chunk):
        return pltpu.make_async_copy(
            zero_ref.at[pl.ds(0, CHUNK_ROWS)],
            xs_ref.at[pl.ds(pl.multiple_of(dst_chunk * CHUNK_ROWS, CHUNK_ROWS), CHUNK_ROWS)],
            zsem.at[0])

    def zero_tile_copy(tile):
        return pltpu.make_async_copy(
            zero_ref,
            xs_ref.at[pl.ds(pl.multiple_of(tile * GMM_ROW_TILE, GMM_ROW_TILE), GMM_ROW_TILE)],
            zsem.at[1])

    @pl.when(i == 0)
    def _():
        zero_ref[...] = jnp.zeros(zero_ref.shape, zero_ref.dtype)
        n_unused = xs_ref.shape[0] // GMM_ROW_TILE - nu_ref[0]
        for e in range(tbase_ref.shape[0]):
            _chunk_loop(ntail_ref[e], lambda j, e=e: zero_copy(tbase_ref[e] + j).start())
        _chunk_loop(n_unused, lambda j: zero_tile_copy(nu_ref[0] + j).start())
        for e in range(tbase_ref.shape[0]):
            _chunk_loop(ntail_ref[e], lambda j: zero_copy(0).wait())
        _chunk_loop(n_unused, lambda j: zero_tile_copy(0).wait())

    def wait_slot(slot_, n_):
        _wait_chunks(n_, max_chunks, lambda k: pltpu.make_async_copy(
            loc_ref.at[slot_, pl.ds(0, k * CHUNK_ROWS)], xs_ref.at[pl.ds(0, k * CHUNK_ROWS)],
            sem.at[slot_]))

    @pl.when(i >= 2)
    def _():
        for j in range(g):
            wait_slot(buf(i, j), nch_ref[(i - 2) * g + j])

    r = lax.broadcasted_iota(jnp.int32, (lrows, tm), 0)
    lane = lax.broadcasted_iota(jnp.int32, (lrows, LANES), 1)
    d = h_ref.shape[2]
    for j in range(g):
        mt = metat_ref[j]
        is1 = r == mt[0:1].astype(jnp.int32)
        is2 = r == mt[1:2].astype(jnp.int32)
        sel = jnp.where(is1 | is2, 1.0, 0.0).astype(_BF16)
        loc_ref[buf(i, j), :, 0:d] = _dot(sel, h_ref[j]).astype(_BF16)
        ws = jnp.sum(jnp.where(is1, mt[2:3], 0.0) + jnp.where(is2, mt[3:4], 0.0),
                     axis=1, keepdims=True)
        hi, mid, lo = (p.astype(_F32) for p in _split3(ws))
        loc_ref[buf(i, j), :, d:d + LANES] = jnp.where(
            lane == 0, hi, jnp.where(lane == 1, mid, jnp.where(lane == 2, lo, 0.0))).astype(_BF16)
    for j in range(g):
        tile = i * g + j
        _chunk_loop(nch_ref[tile], lambda c, j=j, tile=tile: chunk_copy(
            buf(i, j), c, cmap_ref[tile * max_chunks + c]).start())

    @pl.when(i == n_steps - 1)
    def _():
        for j in range(g):
            wait_slot(buf(i, j), nch_ref[i * g + j])

        @pl.when(i >= 1)
        def _():
            for j in range(g):
                wait_slot(buf(i - 1, j), nch_ref[(i - 1) * g + j])


def _experts_kernel(te_ref, tf_ref, nu_ref, xs_ref, w1_ref, w3_ref, w2_ref, ys_ref,
                    wb1_ref, wb3_ref, wb2_ref):
    i = pl.program_id(0)

    @pl.when(i < nu_ref[0])
    def _():
        @pl.when(tf_ref[i] == 1)
        def _():
            wb1_ref[...] = w1_ref[0].astype(_BF16)
            wb3_ref[...] = w3_ref[0].astype(_BF16)
            wb2_ref[...] = w2_ref[0].astype(_BF16)

        d = wb1_ref.shape[0]
        xs = xs_ref[:, 0:d]
        half = wb1_ref.shape[1] // 2
        acc = None
        for hcols in (slice(0, half), slice(half, 2 * half)):
            a = _silu(_dot(xs, wb1_ref[:, hcols])) * _dot(xs, wb3_ref[:, hcols])
            part = _dot(a.astype(_BF16), wb2_ref[hcols, :])
            acc = part if acc is None else acc + part
        wp = xs_ref[:, d:d + LANES].astype(_F32)
        ys_ref[...] = (acc * (wp[:, 0:1] + wp[:, 1:2] + wp[:, 2:3])).astype(_BF16)

    @pl.when(i >= nu_ref[0])
    def _():
        ys_ref[...] = jnp.zeros(ys_ref.shape, ys_ref.dtype)


def _combine_kernel(nch_ref, cmap_ref, x_ref, ada_ref, gpost_ref, meta_ref, ys_ref, o_ref,
                    loc_ref, sem, *, max_chunks):
    i = pl.program_id(0)
    n_steps = pl.num_programs(0)
    g = x_ref.shape[0]
    lrows, tm = loc_ref.shape[1], x_ref.shape[1]
    buf = lambda step, j: (step % 2) * g + j

    def chunk_copy(tile, slot_, c):
        src_chunk = cmap_ref[tile * max_chunks + c]
        return pltpu.make_async_copy(
            ys_ref.at[pl.ds(pl.multiple_of(src_chunk * CHUNK_ROWS, CHUNK_ROWS), CHUNK_ROWS)],
            loc_ref.at[slot_, pl.ds(pl.multiple_of(c * CHUNK_ROWS, CHUNK_ROWS), CHUNK_ROWS)],
            sem.at[slot_])

    def fetch(step):
        for j in range(g):
            tile = step * g + j
            _chunk_loop(nch_ref[tile], lambda c, j=j, tile=tile: chunk_copy(tile, buf(step, j), c).start())

    @pl.when(i == 0)
    def _():
        loc_ref[...] = jnp.zeros(loc_ref.shape, loc_ref.dtype)
        fetch(0)

    @pl.when(i + 1 < n_steps)
    def _():
        fetch(i + 1)

    for j in range(g):
        _wait_chunks(nch_ref[i * g + j], max_chunks, lambda k, j=j: pltpu.make_async_copy(
            ys_ref.at[pl.ds(0, k * CHUNK_ROWS)], loc_ref.at[buf(i, j), pl.ds(0, k * CHUNK_ROWS)],
            sem.at[buf(i, j)]))

    r = lax.broadcasted_iota(jnp.int32, (tm, lrows), 1)
    gt2 = ada_ref[0][5:6]
    for j in range(g):
        meta = meta_ref[j]
        d1 = meta[:, 0:1].astype(jnp.int32)
        d2 = meta[:, 1:2].astype(jnp.int32)
        sel = jnp.where((r == d1) | (r == d2), 1.0, 0.0).astype(_BF16)
        y = _dot(sel, loc_ref[buf(i, j)])
        o_ref[j] = x_ref[j] + gt2 * (_rms(y) * gpost_ref[...])


def _moe_plan(cnt, max_chunks, n_gmm_tiles):
    per_tile = GMM_ROW_TILE // CHUNK_ROWS
    pc = (cnt + (CHUNK_ROWS - 1)) // CHUNK_ROWS
    local_first = jnp.cumsum(pc, axis=1) - pc
    nch = jnp.sum(pc, axis=1)
    e_chunks = jnp.sum(pc, axis=0)
    e_region = ((e_chunks + per_tile - 1) // per_tile) * per_tile
    e_first = jnp.cumsum(e_region) - e_region
    seg_first = e_first[None, :] + jnp.cumsum(pc, axis=0) - pc
    c = jnp.arange(max_chunks, dtype=jnp.int32)[None, :, None]
    inside = (c >= local_first[:, None, :]) & (c < (local_first + pc)[:, None, :])
    cmap = jnp.sum(jnp.where(inside, seg_first[:, None, :] + c - local_first[:, None, :], 0), axis=-1)
    n_used = jnp.sum(e_region) // per_tile
    t = jnp.arange(n_gmm_tiles, dtype=jnp.int32)
    t_used = jnp.minimum(t, n_used - 1)
    tile_e = jnp.sum(t_used[:, None] * per_tile >= (e_first + e_region)[None, :], axis=1)
    tile_first = jnp.concatenate([jnp.ones((1,), jnp.int32),
                                  (tile_e[1:] != tile_e[:-1]).astype(jnp.int32)])
    i32 = lambda a: a.astype(jnp.int32)
    return dict(nch=i32(nch), cmap=i32(cmap.reshape(-1)), tail_first=i32(e_first + e_chunks),
                n_tail=i32(e_region - e_chunks), tile_e=i32(tile_e), tile_first=tile_first,
                n_used=i32(n_used.reshape(1)))


def _moe(x, h2, meta, metat, cnt, ada, g_post, n_groups, w1, w3, w2):
    b, s, d = x.shape
    n_exp, _, ff = w1.shape
    tm = h2.shape[1]
    assert s % tm == 0 and tm % LANES == 0 and GMM_ROW_TILE % CHUNK_ROWS == 0
    tiles_per_batch = s // tm
    nt = b * tiles_per_batch
    max_chunks = (TOP_K_IN_GROUP * tm + n_exp * (CHUNK_ROWS - 1)) // CHUNK_ROWS
    lrows = -(-max_chunks * CHUNK_ROWS // LANES) * LANES
    per_tile = GMM_ROW_TILE // CHUNK_ROWS
    n_gmm_tiles = -(-(nt * max_chunks + n_exp * (per_tile - 1)) // per_tile)
    n_rows = n_gmm_tiles * GMM_ROW_TILE
    dx = d + LANES

    row = lambda a: a.reshape(1, -1).astype(_F32)
    n_ada = ada.shape[1]
    xt = x.reshape(nt, tm, d)
    g = MOE_TILES_PER_STEP if tiles_per_batch % MOE_TILES_PER_STEP == 0 else 1
    steps_per_batch = tiles_per_batch // g

    plan = _moe_plan(cnt[:, 0, :n_exp], max_chunks, n_gmm_tiles)

    xs = pl.pallas_call(
        functools.partial(_dispatch_kernel, max_chunks=max_chunks),
        grid_spec=pltpu.PrefetchScalarGridSpec(
            num_scalar_prefetch=5,
            grid=(nt // g,),
            in_specs=[
                pl.BlockSpec((g, 8, tm), lambda i, *_: (i, 0, 0)),
                pl.BlockSpec((g, tm, d), lambda i, *_: (i, 0, 0)),
            ],
            out_specs=pl.BlockSpec(memory_space=pl.ANY),
            scratch_shapes=[
                pltpu.VMEM((2 * g, lrows, dx), _BF16),
                pltpu.VMEM((GMM_ROW_TILE, dx), _BF16),
                pltpu.SemaphoreType.DMA((2 * g,)),
                pltpu.SemaphoreType.DMA((2,)),
            ],
        ),
        out_shape=jax.ShapeDtypeStruct((n_rows, dx), _BF16),
        compiler_params=pltpu.CompilerParams(dimension_semantics=("arbitrary",)),
        name="dispatch",
    )(plan["nch"], plan["cmap"], plan["tail_first"], plan["n_tail"], plan["n_used"], metat, h2)

    ys = pl.pallas_call(
        _experts_kernel,
        grid_spec=pltpu.PrefetchScalarGridSpec(
            num_scalar_prefetch=3,
            grid=(n_gmm_tiles,),
            in_specs=[
                pl.BlockSpec((GMM_ROW_TILE, dx), lambda i, te, tf, nu: (jnp.minimum(i, nu[0] - 1), 0)),
                pl.BlockSpec((1, d, ff), lambda i, te, tf, nu: (te[i], 0, 0)),
                pl.BlockSpec((1, d, ff), lambda i, te, tf, nu: (te[i], 0, 0)),
                pl.BlockSpec((1, ff, d), lambda i, te, tf, nu: (te[i], 0, 0)),
            ],
            out_specs=pl.BlockSpec((GMM_ROW_TILE, d), lambda i, te, tf, nu: (i, 0)),
            scratch_shapes=[
                pltpu.VMEM((d, ff), _BF16),
                pltpu.VMEM((d, ff), _BF16),
                pltpu.VMEM((ff, d), _BF16),
            ],
        ),
        out_shape=jax.ShapeDtypeStruct((n_rows, d), _BF16),
        compiler_params=pltpu.CompilerParams(dimension_semantics=("arbitrary",),
                                             vmem_limit_bytes=VMEM_LIMIT_BYTES),
        name="experts",
    )(plan["tile_e"], plan["tile_first"], plan["n_used"], xs, w1, w3, w2)

    out = pl.pallas_call(
        functools.partial(_combine_kernel, max_chunks=max_chunks),
        grid_spec=pltpu.PrefetchScalarGridSpec(
            num_scalar_prefetch=2,
            grid=(nt // g,),
            in_specs=[
                pl.BlockSpec((g, tm, d), lambda i, *_: (i, 0, 0)),
                pl.BlockSpec((1, n_ada, d), lambda i, *_: (i // steps_per_batch, 0, 0)),
                pl.BlockSpec((1, d), lambda i, *_: (0, 0)),
                pl.BlockSpec((g, tm, LANES), lambda i, *_: (i, 0, 0)),
                pl.BlockSpec(memory_space=pl.ANY),
            ],
            out_specs=pl.BlockSpec((g, tm, d), lambda i, *_: (i, 0, 0)),
            scratch_shapes=[
                pltpu.VMEM((2 * g, lrows, d), _BF16),
                pltpu.SemaphoreType.DMA((2 * g,)),
            ],
        ),
        out_shape=jax.ShapeDtypeStruct((nt, tm, d), _F32),
        compiler_params=pltpu.CompilerParams(dimension_semantics=("arbitrary",)),
        name="combine",
    )(plan["nch"], plan["cmap"], xt, ada, row(g_post), meta, ys)
    return out.reshape(b, s, d)


def kernel(x, c, w_ada, b_ada, g_pre_mix, g_post_mix, w_in, ln_v_g, ln_v_b, w_spatial, b_spatial,
           lb_logits, g_hgrn_norm, w_out, g_pre_ffn, g_post_ffn, w_router_group, b_router_group,
           w_router_expert, b_router_expert, w1, w3, w2):
    depth = w_in.shape[0]
    b, s, d = x.shape
    lb_all = _lower_bounds(lb_logits)
    for layer in range(depth):
        ada = _ada(c, w_ada[layer], b_ada[layer]).reshape(b, 6, d)
        x, h2, meta, metat, cnt = _mixer(
            x, ada, g_pre_mix[layer], g_post_mix[layer], w_in[layer], ln_v_g[layer], ln_v_b[layer],
            w_spatial[layer], b_spatial[layer], lb_all[layer], g_hgrn_norm[layer], w_out[layer],
            g_pre_ffn[layer], w_router_group[layer], b_router_group[layer],
            w_router_expert[layer], b_router_expert[layer])
        x = _moe(x, h2, meta, metat, cnt, ada, g_post_ffn[layer], w_router_group.shape[-1],
                 w1[layer], w3[layer], w2[layer])
    return x
```

```python
import functools

import jax
import jax.numpy as jnp
from jax import lax
from jax.experimental import pallas as pl
from jax.experimental.pallas import tpu as pltpu

RMS_EPS = 1e-6
LN_EPS = 1e-5
HGRN_HEAD_DIM = 128
HGRN_CHUNK = 128
HGRN_MAX_SPAN = 60.0
N_IN_SLICES = 8
TOP_K_IN_GROUP = 2
LANES = 128
MIXER_SEQ_TILE = 256
PROJ_COLS = 256
CHUNK_ROWS = 16
GMM_ROW_TILE = 1024
MOE_TILES_PER_STEP = 4
VMEM_LIMIT_BYTES = 56 * 1024 * 1024

_F32 = jnp.float32
_BF16 = jnp.bfloat16


def _sigmoid(x):
    return 0.5 * (jnp.tanh(0.5 * x) + 1.0)


def _silu(x):
    return x * _sigmoid(x)


def _gelu_tanh(x):
    c = 0.7978845608028654
    return 0.5 * x * (1.0 + jnp.tanh(c * (x + 0.044715 * (x * x * x))))


def _rms(x):
    return x * lax.rsqrt(jnp.mean(x * x, axis=-1, keepdims=True) + RMS_EPS)


def _dot(a, b):
    return jnp.dot(a, b, preferred_element_type=_F32)


def _dot_nt(a, b):
    return lax.dot_general(a, b, (((1,), (1,)), ((), ())), preferred_element_type=_F32)


def _dot_tn(a, b):
    return lax.dot_general(a, b, (((0,), (0,)), ((), ())), preferred_element_type=_F32)


def _split3(x):
    hi = x.astype(_BF16)
    r = x - hi.astype(_F32)
    mid = r.astype(_BF16)
    lo = (r - mid.astype(_F32)).astype(_BF16)
    return hi, mid, lo


def _ada_kernel(c_ref, w_ref, b_ref, o_ref):
    s = _silu(c_ref[...])
    w = w_ref[...]
    acc = jnp.zeros(o_ref.shape, _F32)
    for sp in _split3(s):
        for wp in _split3(w)[:2]:
            acc = acc + _dot(sp, wp)
    o_ref[...] = acc + b_ref[...]


def _ada(c, w_ada, b_ada):
    b, d = c.shape
    n = w_ada.shape[1]
    tn = d
    return pl.pallas_call(
        _ada_kernel,
        grid=(n // tn,),
        in_specs=[
            pl.BlockSpec((b, d), lambda j: (0, 0)),
            pl.BlockSpec((d, tn), lambda j: (0, j)),
            pl.BlockSpec((1, tn), lambda j: (0, j)),
        ],
        out_specs=pl.BlockSpec((b, tn), lambda j: (0, j)),
        out_shape=jax.ShapeDtypeStruct((b, n), _F32),
        name="ada",
    )(c, w_ada, b_ada.reshape(1, n))


def _lb_kernel(l_ref, o_ref):
    l = l_ref[...]
    e = jnp.exp(l - jnp.max(l, axis=0, keepdims=True))
    p = e / jnp.sum(e, axis=0, keepdims=True)
    rows = [p[0:1]]
    for i in range(1, p.shape[0]):
        rows.append(rows[-1] + p[i:i + 1])
    for i, r in enumerate(rows):
        o_ref[i:i + 1, :] = r


def _lower_bounds(lb_logits):
    return pl.pallas_call(
        _lb_kernel,
        out_shape=jax.ShapeDtypeStruct(lb_logits.shape, _F32),
        name="lower_bounds",
    )(lb_logits.astype(_F32))


def _mixer_kernel(x_ref, ada_ref, adap_ref, gpre_ref, gpost_ref, win_ref, lng_ref, lnb_ref, wsp_ref,
                  bspt_ref, lb_ref, ghg_ref, wout_ref, gpre2_ref, wrt_ref, brt_ref,
                  o_ref, h2_ref, meta_ref, metat_ref, cnt_ref, span_ref, st_ref, ya_ref, ob_ref,
                  x1p_ref, act_ref, iv_ref, vn_ref,
                  *, n_route_groups, per_group, tiles_per_batch, stepwise):
    ts, d = x_ref.shape[1], x_ref.shape[2]
    w = wout_ref.shape[0]
    n_groups, gchunk = wsp_ref.shape[0], wsp_ref.shape[1]
    gdim = w // n_groups
    n_heads = w // HGRN_HEAD_DIM
    c = HGRN_CHUNK
    t = pl.program_id(0)
    is_tile = t < pl.num_programs(0) - 1

    @pl.when(t % tiles_per_batch == 0)
    def _():
        st_ref[...] = jnp.zeros(st_ref.shape, _F32)

    @pl.when(t == 0)
    def _():
        x1p_ref[...] = jnp.zeros(x1p_ref.shape, _F32)

    x1_prev = x1p_ref[...]
    adap = adap_ref[0]
    _route_tile(x1_prev, adap[3:4], adap[4:5], gpre2_ref[...], wrt_ref[...], brt_ref[...],
                h2_ref, meta_ref, metat_ref, cnt_ref, n_route_groups, per_group)

    x = x_ref[0]
    ada = ada_ref[0]
    sh1, sc1, gt1 = ada[0:1], ada[1:2], ada[2:3]
    h = (_rms(x) * gpre_ref[...]) * (1.0 + sc1) + sh1
    h_bf = h.astype(_BF16)

    lb = lb_ref[...]
    (s_v, s_u, s_q, s_f, s_i, s_og, s_ga, s_gb) = (1, 0, 2, 3, 4, 5, 6, 7)
    (a_u, a_v, a_q, a_lf, a_k, a_og, a_ga, a_gb) = range(8)

    def activate(j, cols, r):
        if j == s_u:
            act_ref[a_u, :, cols] = _gelu_tanh(r)
        elif j == s_v:
            act_ref[a_v, :, cols] = _gelu_tanh(r)
        elif j == s_q:
            act_ref[a_q, :, cols] = _silu(r)
        elif j == s_f:
            f = lb[:, cols] + (1.0 - lb[:, cols]) * _sigmoid(r)
            act_ref[a_lf, :, cols] = jnp.log(f)
            act_ref[a_k, :, cols] = 1.0 - f
        elif j == s_i:
            iv_ref[:, cols] = r.astype(_BF16)
        elif j == s_og:
            act_ref[a_og, :, cols] = _silu(r)
        elif j == s_ga:
            act_ref[a_ga, :, cols] = _sigmoid(r)
        else:
            act_ref[a_gb, :, cols] = _sigmoid(r)

    def layer_norm_v():
        v = act_ref[a_v]
        mu = jnp.mean(v, axis=-1, keepdims=True)
        vc = v - mu
        var = jnp.mean(vc * vc, axis=-1, keepdims=True)
        vn_ref[...] = ((vc * lax.rsqrt(var + LN_EPS)) * lng_ref[...] + lnb_ref[...]).astype(_BF16)

    order = (s_v, s_u, s_q, s_f, s_i, s_og, s_ga, s_gb)
    pending = None
    for pos, j in enumerate(order):
        for cb in range(w // PROJ_COLS):
            cols = slice(cb * PROJ_COLS, (cb + 1) * PROJ_COLS)
            r = _dot(h_bf, win_ref[:, j * w + cb * PROJ_COLS:j * w + (cb + 1) * PROJ_COLS])
            if pending is not None:
                pending()
            pending = functools.partial(activate, j, cols, r)
            if pos == 1 and cb == 0:
                layer_norm_v()
    pending()

    u = act_ref[a_u]
    vn = vn_ref[...]
    tri_g = (lax.broadcasted_iota(jnp.int32, (gchunk, gchunk), 0)
             >= lax.broadcasted_iota(jnp.int32, (gchunk, gchunk), 1))
    bspt = bspt_ref[...]
    n_gchunks = ts // gchunk
    for g in range(n_groups):
        wc = jnp.where(tri_g, wsp_ref[g], 0.0).astype(_BF16)
        cols = slice(g * gdim, (g + 1) * gdim)
        vg = jnp.concatenate([vn[n * gchunk:(n + 1) * gchunk, cols] for n in range(n_gchunks)], axis=1)
        zv = _dot(wc, vg)
        for n in range(n_gchunks):
            rows = slice(n * gchunk, (n + 1) * gchunk)
            ya_ref[rows, cols] = u[rows, cols] * (zv[:, n * gdim:(n + 1) * gdim] + bspt[:, g:g + 1])

    pair = 2 * HGRN_HEAD_DIM
    pr = lax.broadcasted_iota(jnp.int32, (pair, pair), 0)
    pc_i = lax.broadcasted_iota(jnp.int32, (pair, pair), 1)
    same_head = (pr < HGRN_HEAD_DIM) == (pc_i < HGRN_HEAD_DIM)
    if stepwise:
        _hgrn_stepwise(act_ref.at[a_q], act_ref.at[a_k], iv_ref, st_ref, ob_ref, same_head)
        span_ref[0] = jnp.zeros(span_ref.shape[1:], _F32)
    else:
        _hgrn_chunkwise(act_ref.at[a_q], act_ref.at[a_lf], act_ref.at[a_k], iv_ref, st_ref, ob_ref,
                        span_ref, same_head)

    yb = (ob_ref[...] * ghg_ref[...]) * act_ref[a_og]
    y = act_ref[a_ga] * ya_ref[...] + act_ref[a_gb] * yb
    out = _dot(y.astype(_BF16), wout_ref[...])
    x1 = jnp.where(is_tile, x + gt1 * (_rms(out) * gpost_ref[...]), x1_prev)
    o_ref[0] = x1
    x1p_ref[...] = x1


def _hgrn_stepwise(q_ref, k_ref, iv_ref, st_ref, ob_ref, same_head):
    ts, w = q_ref.shape
    pair = 2 * HGRN_HEAD_DIM
    sub = 16

    def group(gi, carry):
        r0 = pl.multiple_of(gi * sub, sub)
        q_g = q_ref[pl.ds(r0, sub), :]
        k_g = k_ref[pl.ds(r0, sub), :]
        v_g = iv_ref[pl.ds(r0, sub), :].astype(_F32)
        o_rows = []
        for i in range(sub):
            q_t, k_t, v_t = q_g[i:i + 1], k_g[i:i + 1], v_g[i:i + 1]
            o_t = []
            for p in range(w // pair):
                pcols = slice(p * pair, (p + 1) * pair)
                upd = _dot_tn(v_t[:, pcols].astype(_BF16), k_t[:, pcols].astype(_BF16))
                st = st_ref[p] * (1.0 - k_t[:, pcols]) + jnp.where(same_head, upd, 0.0)
                st_ref[p] = st
                o_p = _dot_nt(q_t[:, pcols].astype(_BF16), st.astype(_BF16))
                o_t += [_rms(o_p[:, :HGRN_HEAD_DIM]), _rms(o_p[:, HGRN_HEAD_DIM:])]
            o_rows.append(jnp.concatenate(o_t, axis=1))
        ob_ref[pl.ds(r0, sub), :] = jnp.concatenate(o_rows, axis=0)
        return carry

    lax.fori_loop(0, ts // sub, group, 0)


def _hgrn_chunkwise(q_ref, lf_ref, k_ref, iv_ref, st_ref, ob_ref, span_ref, same_head):
    ts, w = q_ref.shape
    n_heads = w // HGRN_HEAD_DIM
    c = HGRN_CHUNK
    q = q_ref[...]
    lf = lf_ref[...]
    k = k_ref[...]
    iv = iv_ref[...]
    r_i = lax.broadcasted_iota(jnp.int32, (ts, ts), 0)
    c_i = lax.broadcasted_iota(jnp.int32, (ts, ts), 1)
    ltri = jnp.where((r_i // c == c_i // c) & (c_i <= r_i), 1.0, 0.0).astype(_BF16)
    bcum = jnp.zeros((ts, w), _F32)
    for part in _split3(lf):
        bcum = bcum + _dot(ltri, part)
    hb = c // 2
    row_h = lax.broadcasted_iota(jnp.int32, (hb, c), 0)
    lane_h = lax.broadcasted_iota(jnp.int32, (hb, c), 1)
    top_mask = lane_h <= row_h
    left = lane_h < hb
    bot_mask = lane_h - hb <= row_h
    pair = 2 * HGRN_HEAD_DIM
    up_rows = lax.broadcasted_iota(jnp.int32, (c, pair), 1) < HGRN_HEAD_DIM
    span = jnp.zeros((1, 1), _F32)
    for n in range(ts // c):
        rows = slice(n * c, (n + 1) * c)
        bc = bcum[rows]
        b_a, b_m, b_b, b_l = (bc[hb // 2 - 1:hb // 2], bc[hb - 1:hb],
                              bc[hb + hb // 2 - 1:hb + hb // 2], bc[c - 1:c])
        stack = lambda ra, rb: jnp.concatenate(
            [jnp.broadcast_to(ra, (hb, w)), jnp.broadcast_to(rb, (hb, w))], axis=0)
        ref = stack(b_a, b_b)
        ends = jnp.maximum(jnp.maximum(bc[0:1] - b_a, b_a - b_m),
                           jnp.maximum(bc[hb:hb + 1] - b_b, b_b - b_l))
        span = jnp.maximum(span, jnp.max(ends, axis=1, keepdims=True))
        qx = q[rows] * jnp.exp(bc - ref)
        kx = k[rows] * jnp.exp(ref - bc)
        qd = (qx * stack(jnp.exp(b_a), jnp.exp(b_b))).astype(_BF16)
        kd = (kx * stack(jnp.exp(b_l - b_a), jnp.exp(b_l - b_b))).astype(_BF16)
        q_off = qx[hb:] * jnp.exp(b_b - b_m)
        k_off = kx[:hb] * jnp.exp(b_m - b_a)
        q3 = jnp.concatenate([qx, q_off], axis=0).astype(_BF16)
        k3 = jnp.concatenate([kx, k_off, jnp.zeros((hb, w), _F32)], axis=0).astype(_BF16)
        dec = jnp.exp(b_l)
        iv_c = iv[rows]
        for p in range(n_heads // 2):
            pcols = slice(p * pair, (p + 1) * pair)
            a_heads = []
            for hd in (2 * p, 2 * p + 1):
                cols = slice(hd * HGRN_HEAD_DIM, (hd + 1) * HGRN_HEAD_DIM)
                sc = _dot_nt(q3[:, cols], k3[:, cols])
                top = jnp.where(top_mask, sc[0:hb, 0:c], 0.0)
                bot = jnp.where(left, sc[2 * hb:3 * hb, c:2 * c],
                                jnp.where(bot_mask, sc[hb:2 * hb, 0:c], 0.0))
                a_heads.append(jnp.concatenate([top, bot], axis=0))
            a_pair = jnp.concatenate(a_heads, axis=1).astype(_BF16)
            iv_p = iv_c[:, pcols]
            iv_blk = jnp.concatenate([jnp.where(up_rows, iv_p, 0.0).astype(_BF16),
                                      jnp.where(up_rows, 0.0, iv_p).astype(_BF16)], axis=0)
            st = st_ref[p]
            o = _dot(a_pair, iv_blk) + _dot_nt(qd[:, pcols], st.astype(_BF16))
            upd = _dot_tn(iv_p, kd[:, pcols])
            st_ref[p] = st * dec[:, pcols] + jnp.where(same_head, upd, 0.0)
            for j in range(2):
                hcols = slice(j * HGRN_HEAD_DIM, (j + 1) * HGRN_HEAD_DIM)
                ob_ref[rows, p * pair + j * HGRN_HEAD_DIM:p * pair + (j + 1) * HGRN_HEAD_DIM] = _rms(o[:, hcols])
    span_ref[0] = jnp.broadcast_to(span, span_ref.shape[1:])


def _mixer(x, ada, g_pre, g_post, w_in, ln_g, ln_b, w_spatial, b_spatial, lb, g_hgrn, w_out,
           g_pre_ffn, w_rg, b_rg, w_re, b_re, stepwise):
    b, s, d = x.shape
    w = w_out.shape[0]
    n_groups, gchunk, _ = w_spatial.shape
    ts = min(MIXER_SEQ_TILE, s)
    assert s % ts == 0 and ts % gchunk == 0 and ts % HGRN_CHUNK == 0 and ts % LANES == 0
    n_rg, n_exp = w_rg.shape[1], w_re.shape[1]
    assert n_rg + n_exp <= LANES
    pad = LANES - n_rg - n_exp
    wrt = jnp.concatenate([w_rg, w_re, jnp.zeros((d, pad), _F32)], axis=1).T.astype(_BF16)
    brt = jnp.concatenate([b_rg, b_re, jnp.zeros((pad,), _F32)]).reshape(LANES, 1).astype(_F32)
    tiles = s // ts
    nt = b * tiles
    cur = lambda t: jnp.minimum(t, nt - 1)
    prev = lambda t: jnp.maximum(t - 1, 0)
    cur3 = lambda t: (cur(t), 0, 0)
    prev3 = lambda t: (prev(t), 0, 0)
    assert w % HGRN_HEAD_DIM == 0 and w // n_groups == LANES and w_in.shape == (d, N_IN_SLICES * w)
    assert w % PROJ_COLS == 0
    n_heads = w // HGRN_HEAD_DIM
    assert n_heads % 2 == 0 and HGRN_HEAD_DIM == LANES
    row = lambda a: a.reshape(1, -1).astype(_F32)
    const2 = lambda t: (0, 0)
    const3 = lambda t: (0, 0, 0)
    single = dict(pipeline_mode=pl.Buffered(1))
    x1, h2, meta, metat, cnt, span = pl.pallas_call(
        functools.partial(_mixer_kernel, n_route_groups=n_rg, per_group=n_exp // n_rg,
                          tiles_per_batch=tiles, stepwise=stepwise),
        grid=(nt + 1,),
        in_specs=[
            pl.BlockSpec((1, ts, d), cur3),
            pl.BlockSpec((1, ada.shape[1], d), lambda t: (cur(t) // tiles, 0, 0)),
            pl.BlockSpec((1, ada.shape[1], d), lambda t: (prev(t) // tiles, 0, 0)),
            pl.BlockSpec((1, d), const2),
            pl.BlockSpec((1, d), const2),
            pl.BlockSpec((d, N_IN_SLICES * w), const2, **single),
            pl.BlockSpec((1, w), const2),
            pl.BlockSpec((1, w), const2),
            pl.BlockSpec((n_groups, gchunk, gchunk), const3, **single),
            pl.BlockSpec((gchunk, n_groups), const2),
            pl.BlockSpec((1, w), const2),
            pl.BlockSpec((1, w), const2),
            pl.BlockSpec((w, d), const2, **single),
            pl.BlockSpec((1, d), const2),
            pl.BlockSpec((LANES, d), const2),
            pl.BlockSpec((LANES, 1), const2),
        ],
        out_specs=[
            pl.BlockSpec((1, ts, d), cur3),
            pl.BlockSpec((1, ts, d), prev3),
            pl.BlockSpec((1, ts, LANES), prev3),
            pl.BlockSpec((1, 8, ts), prev3),
            pl.BlockSpec((1, 8, LANES), prev3),
            pl.BlockSpec((1, 8, LANES), cur3),
        ],
        out_shape=[
            jax.ShapeDtypeStruct((nt, ts, d), _F32),
            jax.ShapeDtypeStruct((nt, ts, d), _BF16),
            jax.ShapeDtypeStruct((nt, ts, LANES), _F32),
            jax.ShapeDtypeStruct((nt, 8, ts), _F32),
            jax.ShapeDtypeStruct((nt, 8, LANES), jnp.int32),
            jax.ShapeDtypeStruct((nt, 8, LANES), _F32),
        ],
        scratch_shapes=[
            pltpu.VMEM((n_heads // 2, 2 * HGRN_HEAD_DIM, 2 * HGRN_HEAD_DIM), _F32),
            pltpu.VMEM((ts, w), _F32),
            pltpu.VMEM((ts, w), _F32),
            pltpu.VMEM((ts, d), _F32),
            pltpu.VMEM((N_IN_SLICES, ts, w), _F32),
            pltpu.VMEM((ts, w), _BF16),
            pltpu.VMEM((ts, w), _BF16),
        ],
        compiler_params=pltpu.CompilerParams(
            dimension_semantics=("arbitrary",),
            vmem_limit_bytes=VMEM_LIMIT_BYTES),
        name="mixer",
    )(x.reshape(nt, ts, d), ada, ada, row(g_pre), row(g_post), w_in.astype(_BF16), row(ln_g),
      row(ln_b), w_spatial.astype(_F32), b_spatial.T.astype(_F32), row(lb), row(g_hgrn),
      w_out.astype(_BF16), row(g_pre_ffn), wrt, brt)
    return x1.reshape(b, s, d), h2, meta, metat, cnt, span


def _first_max(vals):
    m = functools.reduce(jnp.maximum, vals)
    idx = jnp.full(m.shape, len(vals) - 1, jnp.int32)
    for j in range(len(vals) - 2, -1, -1):
        idx = jnp.where(vals[j] == m, j, idx)
    return m, idx


def _route(lt, n_groups, per_group):
    lg = [lt[g:g + 1] for g in range(n_groups)]
    mg, g_idx = _first_max(lg)
    p_top = 1.0 / functools.reduce(lambda a, b: a + b, [jnp.exp(l - mg) for l in lg])
    le = []
    for j in range(per_group):
        v = lt[n_groups + j:n_groups + j + 1]
        for g in range(1, n_groups):
            r0 = n_groups + g * per_group + j
            v = jnp.where(g_idx == g, lt[r0:r0 + 1], v)
        le.append(v)
    m1, j1 = _first_max(le)
    m2, j2 = _first_max([jnp.where(j1 == j, -jnp.inf, v) for j, v in enumerate(le)])
    r = jnp.exp(m2 - m1)
    w1 = p_top / (1.0 + r)
    w2 = w1 * r
    return g_idx * per_group + j1, g_idx * per_group + j2, w1, w2


def _route_tile(x, sh2, sc2, gpre, wrt, brt, h_ref, meta_ref, metat_ref, cnt_ref, n_groups, per_group):
    tm = x.shape[0]
    h = (_rms(x) * gpre) * (1.0 + sc2) + sh2
    hb = h.astype(_BF16)
    h_ref[0] = hb
    lt = _dot_nt(wrt, hb) + brt
    e1, e2, w1, w2 = _route(lt, n_groups, per_group)

    sub = lax.broadcasted_iota(jnp.int32, (LANES, tm), 0)
    ind = jnp.where((sub == e1) | (sub == e2), 1.0, 0.0)
    earlier = (lax.broadcasted_iota(jnp.int32, (tm, tm), 0)
               < lax.broadcasted_iota(jnp.int32, (tm, tm), 1))
    rank = _dot(ind.astype(_BF16), jnp.where(earlier, 1.0, 0.0).astype(_BF16))
    cnt = jnp.sum(ind, axis=1, keepdims=True)
    chunks = jnp.floor((cnt + (CHUNK_ROWS - 1)) * (1.0 / CHUNK_ROWS))
    below = (lax.broadcasted_iota(jnp.int32, (LANES, LANES), 1)
             < lax.broadcasted_iota(jnp.int32, (LANES, LANES), 0))
    first_chunk = _dot(jnp.where(below, 1.0, 0.0).astype(_BF16),
                       jnp.broadcast_to(chunks, (LANES, LANES)).astype(_BF16))
    dest = first_chunk[:, 0:1] * CHUNK_ROWS + rank
    n_rows = -(-n_groups * per_group // 8) * 8
    sub_e = lax.broadcasted_iota(jnp.int32, (n_rows, tm), 0)
    d1 = jnp.sum(jnp.where(sub_e == e1, dest[:n_rows], 0.0), axis=0, keepdims=True)
    d2 = jnp.sum(jnp.where(sub_e == e2, dest[:n_rows], 0.0), axis=0, keepdims=True)
    row8 = lax.broadcasted_iota(jnp.int32, (8, tm), 0)
    metat = jnp.where(row8 == 0, d1, jnp.where(row8 == 1, d2,
                      jnp.where(row8 == 2, w1, jnp.where(row8 == 3, w2, 0.0))))
    metat_ref[0] = metat
    meta_ref[0] = jnp.concatenate([metat, jnp.zeros((LANES - 8, tm), _F32)], axis=0).T
    cnt_ref[0] = jnp.broadcast_to(cnt, (LANES, LANES)).T[0:8].astype(jnp.int32)


def _chunk_loop(n, fn):
    def body(c, carry):
        fn(c)
        return carry
    lax.fori_loop(0, n, body, 0)


def _wait_chunks(n, max_chunks, copy_of_chunks):
    bit = 1
    while bit * 2 <= max_chunks:
        bit *= 2
    while bit >= 1:
        @pl.when((n & bit) != 0)
        def _(bit=bit):
            copy_of_chunks(bit).wait()
        bit //= 2


def _dispatch_kernel(nch_ref, cmap_ref, tbase_ref, ntail_ref, nu_ref, metat_ref, h_ref, xs_ref,
                     loc_ref, zero_ref, sem, zsem, *, max_chunks):
    i = pl.program_id(0)
    n_steps = pl.num_programs(0)
    g = h_ref.shape[0]
    lrows, tm = loc_ref.shape[1], h_ref.shape[1]
    buf = lambda step, j: (step % 2) * g + j

    def chunk_copy(slot_, c, dst_chunk):
        return pltpu.make_async_copy(
            loc_ref.at[slot_, pl.ds(pl.multiple_of(c * CHUNK_ROWS, CHUNK_ROWS), CHUNK_ROWS)],
            xs_ref.at[pl.ds(pl.multiple_of(dst_chunk * CHUNK_ROWS, CHUNK_ROWS), CHUNK_ROWS)],
            sem.at[slot_])

    def zero_copy(dst_chunk):
        return pltpu.make_async_copy(
            zero_ref.at[pl.ds(0, CHUNK_ROWS)],
            xs_ref.at[pl.ds(pl.multiple_of(dst_chunk * CHUNK_ROWS, CHUNK_ROWS), CHUNK_ROWS)],
            zsem.at[0])

    def zero_tile_copy(tile):
        return pltpu.make_async_copy(
            zero_ref,
            xs_ref.at[pl.ds(pl.multiple_of(tile * GMM_ROW_TILE, GMM_ROW_TILE), GMM_ROW_TILE)],
            zsem.at[1])

    @pl.when(i == 0)
    def _():
        zero_ref[...] = jnp.zeros(zero_ref.shape, zero_ref.dtype)
        n_unused = xs_ref.shape[0] // GMM_ROW_TILE - nu_ref[0]
        for e in range(tbase_ref.shape[0]):
            _chunk_loop(ntail_ref[e], lambda j, e=e: zero_copy(tbase_ref[e] + j).start())
        _chunk_loop(n_unused, lambda j: zero_tile_copy(nu_ref[0] + j).start())
        for e in range(tbase_ref.shape[0]):
            _chunk_loop(ntail_ref[e], lambda j: zero_copy(0).wait())
        _chunk_loop(n_unused, lambda j: zero_tile_copy(0).wait())

    def wait_slot(slot_, n_):
        _wait_chunks(n_, max_chunks, lambda k: pltpu.make_async_copy(
            loc_ref.at[slot_, pl.ds(0, k * CHUNK_ROWS)], xs_ref.at[pl.ds(0, k * CHUNK_ROWS)],
            sem.at[slot_]))

    @pl.when(i >= 2)
    def _():
        for j in range(g):
            wait_slot(buf(i, j), nch_ref[(i - 2) * g + j])

    r = lax.broadcasted_iota(jnp.int32, (lrows, tm), 0)
    lane = lax.broadcasted_iota(jnp.int32, (lrows, LANES), 1)
    d = h_ref.shape[2]
    for j in range(g):
        mt = metat_ref[j]
        is1 = r == mt[0:1].astype(jnp.int32)
        is2 = r == mt[1:2].astype(jnp.int32)
        sel = jnp.where(is1 | is2, 1.0, 0.0).astype(_BF16)
        loc_ref[buf(i, j), :, 0:d] = _dot(sel, h_ref[j]).astype(_BF16)
        ws = jnp.sum(jnp.where(is1, mt[2:3], 0.0) + jnp.where(is2, mt[3:4], 0.0),
                     axis=1, keepdims=True)
        hi, mid, lo = (p.astype(_F32) for p in _split3(ws))
        loc_ref[buf(i, j), :, d:d + LANES] = jnp.where(
            lane == 0, hi, jnp.where(lane == 1, mid, jnp.where(lane == 2, lo, 0.0))).astype(_BF16)
    for j in range(g):
        tile = i * g + j
        _chunk_loop(nch_ref[tile], lambda c, j=j, tile=tile: chunk_copy(
            buf(i, j), c, cmap_ref[tile * max_chunks + c]).start())

    @pl.when(i == n_steps - 1)
    def _():
        for j in range(g):
            wait_slot(buf(i, j), nch_ref[i * g + j])

        @pl.when(i >= 1)
        def _():
            for j in range(g):
                wait_slot(buf(i - 1, j), nch_ref[(i - 1) * g + j])


def _experts_kernel(te_ref, tf_ref, nu_ref, xs_ref, w1_ref, w3_ref, w2_ref, ys_ref,
                    wb1_ref, wb3_ref, wb2_ref):
    i = pl.program_id(0)

    @pl.when(i < nu_ref[0])
    def _():
        @pl.when(tf_ref[i] == 1)
        def _():
            wb1_ref[...] = w1_ref[0].astype(_BF16)
            wb3_ref[...] = w3_ref[0].astype(_BF16)
            wb2_ref[...] = w2_ref[0].astype(_BF16)

        d = wb1_ref.shape[0]
        xs = xs_ref[:, 0:d]
        half = wb1_ref.shape[1] // 2
        acc = None
        for hcols in (slice(0, half), slice(half, 2 * half)):
            a = _silu(_dot(xs, wb1_ref[:, hcols])) * _dot(xs, wb3_ref[:, hcols])
            part = _dot(a.astype(_BF16), wb2_ref[hcols, :])
            acc = part if acc is None else acc + part
        wp = xs_ref[:, d:d + LANES].astype(_F32)
        ys_ref[...] = (acc * (wp[:, 0:1] + wp[:, 1:2] + wp[:, 2:3])).astype(_BF16)

    @pl.when(i >= nu_ref[0])
    def _():
        ys_ref[...] = jnp.zeros(ys_ref.shape, ys_ref.dtype)


def _combine_kernel(nch_ref, cmap_ref, x_ref, ada_ref, gpost_ref, meta_ref, ys_ref, o_ref,
                    loc_ref, sem, *, max_chunks):
    i = pl.program_id(0)
    n_steps = pl.num_programs(0)
    g = x_ref.shape[0]
    lrows, tm = loc_ref.shape[1], x_ref.shape[1]
    buf = lambda step, j: (step % 2) * g + j

    def chunk_copy(tile, slot_, c):
        src_chunk = cmap_ref[tile * max_chunks + c]
        return pltpu.make_async_copy(
            ys_ref.at[pl.ds(pl.multiple_of(src_chunk * CHUNK_ROWS, CHUNK_ROWS), CHUNK_ROWS)],
            loc_ref.at[slot_, pl.ds(pl.multiple_of(c * CHUNK_ROWS, CHUNK_ROWS), CHUNK_ROWS)],
            sem.at[slot_])

    def fetch(step):
        for j in range(g):
            tile = step * g + j
            _chunk_loop(nch_ref[tile], lambda c, j=j, tile=tile: chunk_copy(tile, buf(step, j), c).start())

    @pl.when(i == 0)
    def _():
        loc_ref[...] = jnp.zeros(loc_ref.shape, loc_ref.dtype)
        fetch(0)

    @pl.when(i + 1 < n_steps)
    def _():
        fetch(i + 1)

    for j in range(g):
        _wait_chunks(nch_ref[i * g + j], max_chunks, lambda k, j=j: pltpu.make_async_copy(
            ys_ref.at[pl.ds(0, k * CHUNK_ROWS)], loc_ref.at[buf(i, j), pl.ds(0, k * CHUNK_ROWS)],
            sem.at[buf(i, j)]))

    r = lax.broadcasted_iota(jnp.int32, (tm, lrows), 1)
    gt2 = ada_ref[0][5:6]
    for j in range(g):
        meta = meta_ref[j]
        d1 = meta[:, 0:1].astype(jnp.int32)
        d2 = meta[:, 1:2].astype(jnp.int32)
        sel = jnp.where((r == d1) | (r == d2), 1.0, 0.0).astype(_BF16)
        y = _dot(sel, loc_ref[buf(i, j)])
        o_ref[j] = x_ref[j] + gt2 * (_rms(y) * gpost_ref[...])


def _moe_plan(cnt, max_chunks, n_gmm_tiles):
    per_tile = GMM_ROW_TILE // CHUNK_ROWS
    pc = (cnt + (CHUNK_ROWS - 1)) // CHUNK_ROWS
    local_first = jnp.cumsum(pc, axis=1) - pc
    nch = jnp.sum(pc, axis=1)
    e_chunks = jnp.sum(pc, axis=0)
    e_region = ((e_chunks + per_tile - 1) // per_tile) * per_tile
    e_first = jnp.cumsum(e_region) - e_region
    seg_first = e_first[None, :] + jnp.cumsum(pc, axis=0) - pc
    c = jnp.arange(max_chunks, dtype=jnp.int32)[None, :, None]
    inside = (c >= local_first[:, None, :]) & (c < (local_first + pc)[:, None, :])
    cmap = jnp.sum(jnp.where(inside, seg_first[:, None, :] + c - local_first[:, None, :], 0), axis=-1)
    n_used = jnp.sum(e_region) // per_tile
    t = jnp.arange(n_gmm_tiles, dtype=jnp.int32)
    t_used = jnp.minimum(t, n_used - 1)
    tile_e = jnp.sum(t_used[:, None] * per_tile >= (e_first + e_region)[None, :], axis=1)
    tile_first = jnp.concatenate([jnp.ones((1,), jnp.int32),
                                  (tile_e[1:] != tile_e[:-1]).astype(jnp.int32)])
    i32 = lambda a: a.astype(jnp.int32)
    return dict(nch=i32(nch), cmap=i32(cmap.reshape(-1)), tail_first=i32(e_first + e_chunks),
                n_tail=i32(e_region - e_chunks), tile_e=i32(tile_e), tile_first=tile_first,
                n_used=i32(n_used.reshape(1)))


def _moe(x, h2, meta, metat, cnt, ada, g_post, n_groups, w1, w3, w2):
    b, s, d = x.shape
    n_exp, _, ff = w1.shape
    tm = h2.shape[1]
    assert s % tm == 0 and tm % LANES == 0 and GMM_ROW_TILE % CHUNK_ROWS == 0
    tiles_per_batch = s // tm
    nt = b * tiles_per_batch
    max_chunks = (TOP_K_IN_GROUP * tm + n_exp * (CHUNK_ROWS - 1)) // CHUNK_ROWS
    lrows = -(-max_chunks * CHUNK_ROWS // LANES) * LANES
    per_tile = GMM_ROW_TILE // CHUNK_ROWS
    n_gmm_tiles = -(-(nt * max_chunks + n_exp * (per_tile - 1)) // per_tile)
    n_rows = n_gmm_tiles * GMM_ROW_TILE
    dx = d + LANES

    row = lambda a: a.reshape(1, -1).astype(_F32)
    n_ada = ada.shape[1]
    xt = x.reshape(nt, tm, d)
    g = MOE_TILES_PER_STEP if tiles_per_batch % MOE_TILES_PER_STEP == 0 else 1
    steps_per_batch = tiles_per_batch // g

    plan = _moe_plan(cnt[:, 0, :n_exp], max_chunks, n_gmm_tiles)

    xs = pl.pallas_call(
        functools.partial(_dispatch_kernel, max_chunks=max_chunks),
        grid_spec=pltpu.PrefetchScalarGridSpec(
            num_scalar_prefetch=5,
            grid=(nt // g,),
            in_specs=[
                pl.BlockSpec((g, 8, tm), lambda i, *_: (i, 0, 0)),
                pl.BlockSpec((g, tm, d), lambda i, *_: (i, 0, 0)),
            ],
            out_specs=pl.BlockSpec(memory_space=pl.ANY),
            scratch_shapes=[
                pltpu.VMEM((2 * g, lrows, dx), _BF16),
                pltpu.VMEM((GMM_ROW_TILE, dx), _BF16),
                pltpu.SemaphoreType.DMA((2 * g,)),
                pltpu.SemaphoreType.DMA((2,)),
            ],
        ),
        out_shape=jax.ShapeDtypeStruct((n_rows, dx), _BF16),
        compiler_params=pltpu.CompilerParams(dimension_semantics=("arbitrary",)),
        name="dispatch",
    )(plan["nch"], plan["cmap"], plan["tail_first"], plan["n_tail"], plan["n_used"], metat, h2)

    ys = pl.pallas_call(
        _experts_kernel,
        grid_spec=pltpu.PrefetchScalarGridSpec(
            num_scalar_prefetch=3,
            grid=(n_gmm_tiles,),
            in_specs=[
                pl.BlockSpec((GMM_ROW_TILE, dx), lambda i, te, tf, nu: (jnp.minimum(i, nu[0] - 1), 0)),
                pl.BlockSpec((1, d, ff), lambda i, te, tf, nu: (te[i], 0, 0)),
                pl.BlockSpec((1, d, ff), lambda i, te, tf, nu: (te[i], 0, 0)),
                pl.BlockSpec((1, ff, d), lambda i, te, tf, nu: (te[i], 0, 0)),
            ],
            out_specs=pl.BlockSpec((GMM_ROW_TILE, d), lambda i, te, tf, nu: (i, 0)),
            scratch_shapes=[
                pltpu.VMEM((d, ff), _BF16),
                pltpu.VMEM((d, ff), _BF16),
                pltpu.VMEM((ff, d), _BF16),
            ],
        ),
        out_shape=jax.ShapeDtypeStruct((n_rows, d), _BF16),
        compiler_params=pltpu.CompilerParams(dimension_semantics=("arbitrary",),
                                             vmem_limit_bytes=VMEM_LIMIT_BYTES),
        name="experts",
    )(plan["tile_e"], plan["tile_first"], plan["n_used"], xs, w1, w3, w2)

    out = pl.pallas_call(
        functools.partial(_combine_kernel, max_chunks=max_chunks),
        grid_spec=pltpu.PrefetchScalarGridSpec(
            num_scalar_prefetch=2,
            grid=(nt // g,),
            in_specs=[
                pl.BlockSpec((g, tm, d), lambda i, *_: (i, 0, 0)),
                pl.BlockSpec((1, n_ada, d), lambda i, *_: (i // steps_per_batch, 0, 0)),
                pl.BlockSpec((1, d), lambda i, *_: (0, 0)),
                pl.BlockSpec((g, tm, LANES), lambda i, *_: (i, 0, 0)),
                pl.BlockSpec(memory_space=pl.ANY),
            ],
            out_specs=pl.BlockSpec((g, tm, d), lambda i, *_: (i, 0, 0)),
            scratch_shapes=[
                pltpu.VMEM((2 * g, lrows, d), _BF16),
                pltpu.SemaphoreType.DMA((2 * g,)),
            ],
        ),
        out_shape=jax.ShapeDtypeStruct((nt, tm, d), _F32),
        compiler_params=pltpu.CompilerParams(dimension_semantics=("arbitrary",)),
        name="combine",
    )(plan["nch"], plan["cmap"], xt, ada, row(g_post), meta, ys)
    return out.reshape(b, s, d)


def kernel(x, c, w_ada, b_ada, g_pre_mix, g_post_mix, w_in, ln_v_g, ln_v_b, w_spatial, b_spatial,
           lb_logits, g_hgrn_norm, w_out, g_pre_ffn, g_post_ffn, w_router_group, b_router_group,
           w_router_expert, b_router_expert, w1, w3, w2):
    depth = w_in.shape[0]
    b, s, d = x.shape
    lb_all = _lower_bounds(lb_logits)
    for layer in range(depth):
        ada = _ada(c, w_ada[layer], b_ada[layer]).reshape(b, 6, d)
        mix = functools.partial(
            _mixer, x, ada, g_pre_mix[layer], g_post_mix[layer], w_in[layer], ln_v_g[layer],
            ln_v_b[layer], w_spatial[layer], b_spatial[layer], lb_all[layer], g_hgrn_norm[layer],
            w_out[layer], g_pre_ffn[layer], w_router_group[layer], b_router_group[layer],
            w_router_expert[layer], b_router_expert[layer])
        *mixed, span = mix(stepwise=False)
        x, h2, meta, metat, cnt = lax.cond(
            jnp.max(span) > HGRN_MAX_SPAN,
            lambda: tuple(mix(stepwise=True)[:5]),
            lambda: tuple(mixed))
        x = _moe(x, h2, meta, metat, cnt, ada, g_post_ffn[layer], w_router_group.shape[-1],
                 w1[layer], w3[layer], w2[layer])
    return x
```

```python
import functools

import jax
import jax.numpy as jnp
from jax import lax
from jax.experimental import pallas as pl
from jax.experimental.pallas import tpu as pltpu

RMS_EPS = 1e-6
LN_EPS = 1e-5
HGRN_HEAD_DIM = 128
HGRN_CHUNK = 128
HGRN_MAX_SPAN = 60.0
N_IN_SLICES = 8
TOP_K_IN_GROUP = 2
LANES = 128
MIXER_SEQ_TILE = 256
PROJ_COLS = 256
CHUNK_ROWS = 16
GMM_ROW_TILE = 1024
MOE_TILES_PER_STEP = 4
VMEM_LIMIT_BYTES = 56 * 1024 * 1024

_F32 = jnp.float32
_BF16 = jnp.bfloat16


def _sigmoid(x):
    return 0.5 * (jnp.tanh(0.5 * x) + 1.0)


def _silu(x):
    return x * _sigmoid(x)


def _gelu_tanh(x):
    c = 0.7978845608028654
    return 0.5 * x * (1.0 + jnp.tanh(c * (x + 0.044715 * (x * x * x))))


def _rms(x):
    return x * lax.rsqrt(jnp.mean(x * x, axis=-1, keepdims=True) + RMS_EPS)


def _dot(a, b):
    return jnp.dot(a, b, preferred_element_type=_F32)


def _dot_nt(a, b):
    return lax.dot_general(a, b, (((1,), (1,)), ((), ())), preferred_element_type=_F32)


def _dot_tn(a, b):
    return lax.dot_general(a, b, (((0,), (0,)), ((), ())), preferred_element_type=_F32)


def _split3(x):
    hi = x.astype(_BF16)
    r = x - hi.astype(_F32)
    mid = r.astype(_BF16)
    lo = (r - mid.astype(_F32)).astype(_BF16)
    return hi, mid, lo


def _ada_kernel(c_ref, w_ref, b_ref, o_ref):
    s = _silu(c_ref[...])
    w = w_ref[...]
    acc = jnp.zeros(o_ref.shape, _F32)
    for sp in _split3(s):
        for wp in _split3(w)[:2]:
            acc = acc + _dot(sp, wp)
    o_ref[...] = acc + b_ref[...]


def _ada(c, w_ada, b_ada):
    b, d = c.shape
    n = w_ada.shape[1]
    tn = d
    return pl.pallas_call(
        _ada_kernel,
        grid=(n // tn,),
        in_specs=[
            pl.BlockSpec((b, d), lambda j: (0, 0)),
            pl.BlockSpec((d, tn), lambda j: (0, j)),
            pl.BlockSpec((1, tn), lambda j: (0, j)),
        ],
        out_specs=pl.BlockSpec((b, tn), lambda j: (0, j)),
        out_shape=jax.ShapeDtypeStruct((b, n), _F32),
        name="ada",
    )(c, w_ada, b_ada.reshape(1, n))


def _lb_kernel(l_ref, o_ref):
    l = l_ref[...]
    e = jnp.exp(l - jnp.max(l, axis=0, keepdims=True))
    p = e / jnp.sum(e, axis=0, keepdims=True)
    rows = [p[0:1]]
    for i in range(1, p.shape[0]):
        rows.append(rows[-1] + p[i:i + 1])
    for i, r in enumerate(rows):
        o_ref[i:i + 1, :] = r


def _lower_bounds(lb_logits):
    return pl.pallas_call(
        _lb_kernel,
        out_shape=jax.ShapeDtypeStruct(lb_logits.shape, _F32),
        name="lower_bounds",
    )(lb_logits.astype(_F32))


def _mixer_kernel(x_ref, ada_ref, adap_ref, gpre_ref, gpost_ref, win_ref, lng_ref, lnb_ref, wsp_ref,
                  bspt_ref, lb_ref, ghg_ref, wout_ref, gpre2_ref, wrt_ref, brt_ref,
                  o_ref, h2_ref, meta_ref, metat_ref, cnt_ref, span_ref, st_ref, ya_ref, ob_ref,
                  x1p_ref, act_ref, iv_ref, vn_ref,
                  *, n_route_groups, per_group, tiles_per_batch, stepwise):
    ts, d = x_ref.shape[1], x_ref.shape[2]
    w = wout_ref.shape[0]
    n_groups, gchunk = wsp_ref.shape[0], wsp_ref.shape[1]
    gdim = w // n_groups
    n_heads = w // HGRN_HEAD_DIM
    c = HGRN_CHUNK
    t = pl.program_id(0)
    is_tile = t < pl.num_programs(0) - 1

    @pl.when(t % tiles_per_batch == 0)
    def _():
        st_ref[...] = jnp.zeros(st_ref.shape, _F32)

    @pl.when(t == 0)
    def _():
        x1p_ref[...] = jnp.zeros(x1p_ref.shape, _F32)

    x1_prev = x1p_ref[...]
    adap = adap_ref[0]
    _route_tile(x1_prev, adap[3:4], adap[4:5], gpre2_ref[...], wrt_ref[...], brt_ref[...],
                h2_ref, meta_ref, metat_ref, cnt_ref, n_route_groups, per_group)

    x = x_ref[0]
    ada = ada_ref[0]
    sh1, sc1, gt1 = ada[0:1], ada[1:2], ada[2:3]
    h = (_rms(x) * gpre_ref[...]) * (1.0 + sc1) + sh1
    h_bf = h.astype(_BF16)

    lb = lb_ref[...]
    (s_v, s_u, s_q, s_f, s_i, s_og, s_ga, s_gb) = (1, 0, 2, 3, 4, 5, 6, 7)
    (a_u, a_v, a_q, a_lf, a_k, a_og, a_ga, a_gb) = range(8)

    def activate(j, cols, r):
        if j == s_u:
            act_ref[a_u, :, cols] = _gelu_tanh(r)
        elif j == s_v:
            act_ref[a_v, :, cols] = _gelu_tanh(r)
        elif j == s_q:
            act_ref[a_q, :, cols] = _silu(r)
        elif j == s_f:
            f = lb[:, cols] + (1.0 - lb[:, cols]) * _sigmoid(r)
            act_ref[a_lf, :, cols] = jnp.log(f)
            act_ref[a_k, :, cols] = 1.0 - f
        elif j == s_i:
            iv_ref[:, cols] = r.astype(_BF16)
        elif j == s_og:
            act_ref[a_og, :, cols] = _silu(r)
        elif j == s_ga:
            act_ref[a_ga, :, cols] = _sigmoid(r)
        else:
            act_ref[a_gb, :, cols] = _sigmoid(r)

    def layer_norm_v():
        v = act_ref[a_v]
        mu = jnp.mean(v, axis=-1, keepdims=True)
        vc = v - mu
        var = jnp.mean(vc * vc, axis=-1, keepdims=True)
        vn_ref[...] = ((vc * lax.rsqrt(var + LN_EPS)) * lng_ref[...] + lnb_ref[...]).astype(_BF16)

    order = (s_v, s_u, s_q, s_f, s_i, s_og, s_ga, s_gb)
    pending = None
    for pos, j in enumerate(order):
        for cb in range(w // PROJ_COLS):
            cols = slice(cb * PROJ_COLS, (cb + 1) * PROJ_COLS)
            r = _dot(h_bf, win_ref[:, j * w + cb * PROJ_COLS:j * w + (cb + 1) * PROJ_COLS])
            if pending is not None:
                pending()
            pending = functools.partial(activate, j, cols, r)
            if pos == 1 and cb == 0:
                layer_norm_v()
    pending()

    u = act_ref[a_u]
    vn = vn_ref[...]
    tri_g = (lax.broadcasted_iota(jnp.int32, (gchunk, gchunk), 0)
             >= lax.broadcasted_iota(jnp.int32, (gchunk, gchunk), 1))
    bspt = bspt_ref[...]
    n_gchunks = ts // gchunk
    for g in range(n_groups):
        wc = jnp.where(tri_g, wsp_ref[g], 0.0).astype(_BF16)
        cols = slice(g * gdim, (g + 1) * gdim)
        vg = jnp.concatenate([vn[n * gchunk:(n + 1) * gchunk, cols] for n in range(n_gchunks)], axis=1)
        zv = _dot(wc, vg)
        for n in range(n_gchunks):
            rows = slice(n * gchunk, (n + 1) * gchunk)
            ya_ref[rows, cols] = u[rows, cols] * (zv[:, n * gdim:(n + 1) * gdim] + bspt[:, g:g + 1])

    pair = 2 * HGRN_HEAD_DIM
    pr = lax.broadcasted_iota(jnp.int32, (pair, pair), 0)
    pc_i = lax.broadcasted_iota(jnp.int32, (pair, pair), 1)
    same_head = (pr < HGRN_HEAD_DIM) == (pc_i < HGRN_HEAD_DIM)
    if stepwise:
        _hgrn_stepwise(act_ref.at[a_q], act_ref.at[a_k], iv_ref, st_ref, ob_ref, same_head)
        span_ref[0] = jnp.zeros(span_ref.shape[1:], _F32)
    else:
        _hgrn_chunkwise(act_ref.at[a_q], act_ref.at[a_lf], act_ref.at[a_k], iv_ref, st_ref, ob_ref,
                        span_ref, same_head)

    yb = (ob_ref[...] * ghg_ref[...]) * act_ref[a_og]
    y = act_ref[a_ga] * ya_ref[...] + act_ref[a_gb] * yb
    out = _dot(y.astype(_BF16), wout_ref[...])
    x1 = jnp.where(is_tile, x + gt1 * (_rms(out) * gpost_ref[...]), x1_prev)
    o_ref[0] = x1
    x1p_ref[...] = x1


def _hgrn_stepwise(q_ref, k_ref, iv_ref, st_ref, ob_ref, same_head):
    ts, w = q_ref.shape
    pair = 2 * HGRN_HEAD_DIM
    sub = 16

    def group(gi, carry):
        r0 = pl.multiple_of(gi * sub, sub)
        q_g = q_ref[pl.ds(r0, sub), :]
        k_g = k_ref[pl.ds(r0, sub), :]
        v_g = iv_ref[pl.ds(r0, sub), :].astype(_F32)
        o_rows = []
        for i in range(sub):
            q_t, k_t, v_t = q_g[i:i + 1], k_g[i:i + 1], v_g[i:i + 1]
            o_t = []
            for p in range(w // pair):
                pcols = slice(p * pair, (p + 1) * pair)
                upd = _dot_tn(v_t[:, pcols].astype(_BF16), k_t[:, pcols].astype(_BF16))
                st = st_ref[p] * (1.0 - k_t[:, pcols]) + jnp.where(same_head, upd, 0.0)
                st_ref[p] = st
                o_p = _dot_nt(q_t[:, pcols].astype(_BF16), st.astype(_BF16))
                o_t += [_rms(o_p[:, :HGRN_HEAD_DIM]), _rms(o_p[:, HGRN_HEAD_DIM:])]
            o_rows.append(jnp.concatenate(o_t, axis=1))
        ob_ref[pl.ds(r0, sub), :] = jnp.concatenate(o_rows, axis=0)
        return carry

    lax.fori_loop(0, ts // sub, group, 0)


def _hgrn_chunkwise(q_ref, lf_ref, k_ref, iv_ref, st_ref, ob_ref, span_ref, same_head):
    ts, w = q_ref.shape
    n_heads = w // HGRN_HEAD_DIM
    c = HGRN_CHUNK
    q = q_ref[...]
    lf = lf_ref[...]
    k = k_ref[...]
    iv = iv_ref[...]
    r_i = lax.broadcasted_iota(jnp.int32, (ts, ts), 0)
    c_i = lax.broadcasted_iota(jnp.int32, (ts, ts), 1)
    ltri = jnp.where((r_i // c == c_i // c) & (c_i <= r_i), 1.0, 0.0).astype(_BF16)
    bcum = jnp.zeros((ts, w), _F32)
    for part in _split3(lf)[:2]:
        bcum = bcum + _dot(ltri, part)
    hb = c // 2
    row_h = lax.broadcasted_iota(jnp.int32, (hb, c), 0)
    lane_h = lax.broadcasted_iota(jnp.int32, (hb, c), 1)
    top_mask = lane_h <= row_h
    left = lane_h < hb
    bot_mask = lane_h - hb <= row_h
    pair = 2 * HGRN_HEAD_DIM
    up_rows = lax.broadcasted_iota(jnp.int32, (c, pair), 1) < HGRN_HEAD_DIM
    span = jnp.zeros((1, 1), _F32)
    for n in range(ts // c):
        rows = slice(n * c, (n + 1) * c)
        bc = bcum[rows]
        b_a, b_m, b_b, b_l = (bc[hb // 2 - 1:hb // 2], bc[hb - 1:hb],
                              bc[hb + hb // 2 - 1:hb + hb // 2], bc[c - 1:c])
        stack = lambda ra, rb: jnp.concatenate(
            [jnp.broadcast_to(ra, (hb, w)), jnp.broadcast_to(rb, (hb, w))], axis=0)
        ref = stack(b_a, b_b)
        ends = jnp.maximum(jnp.maximum(bc[0:1] - b_a, b_a - b_m),
                           jnp.maximum(bc[hb:hb + 1] - b_b, b_b - b_l))
        span = jnp.maximum(span, jnp.max(ends, axis=1, keepdims=True))
        qx = q[rows] * jnp.exp(bc - ref)
        kx = k[rows] * jnp.exp(ref - bc)
        qd = (qx * stack(jnp.exp(b_a), jnp.exp(b_b))).astype(_BF16)
        kd = (kx * stack(jnp.exp(b_l - b_a), jnp.exp(b_l - b_b))).astype(_BF16)
        q_off = qx[hb:] * jnp.exp(b_b - b_m)
        k_off = kx[:hb] * jnp.exp(b_m - b_a)
        q3 = jnp.concatenate([qx, q_off], axis=0).astype(_BF16)
        k3 = jnp.concatenate([kx, k_off, jnp.zeros((hb, w), _F32)], axis=0).astype(_BF16)
        dec = jnp.exp(b_l)
        iv_c = iv[rows]
        for p in range(n_heads // 2):
            pcols = slice(p * pair, (p + 1) * pair)
            a_heads = []
            for hd in (2 * p, 2 * p + 1):
                cols = slice(hd * HGRN_HEAD_DIM, (hd + 1) * HGRN_HEAD_DIM)
                sc = _dot_nt(q3[:, cols], k3[:, cols])
                top = jnp.where(top_mask, sc[0:hb, 0:c], 0.0)
                bot = jnp.where(left, sc[2 * hb:3 * hb, c:2 * c],
                                jnp.where(bot_mask, sc[hb:2 * hb, 0:c], 0.0))
                a_heads.append(jnp.concatenate([top, bot], axis=0))
            a_pair = jnp.concatenate(a_heads, axis=1).astype(_BF16)
            iv_p = iv_c[:, pcols]
            iv_blk = jnp.concatenate([jnp.where(up_rows, iv_p, 0.0).astype(_BF16),
                                      jnp.where(up_rows, 0.0, iv_p).astype(_BF16)], axis=0)
            st = st_ref[p]
            o = _dot(a_pair, iv_blk) + _dot_nt(qd[:, pcols], st.astype(_BF16))
            upd = _dot_tn(iv_p, kd[:, pcols])
            st_ref[p] = st * dec[:, pcols] + jnp.where(same_head, upd, 0.0)
            for j in range(2):
                hcols = slice(j * HGRN_HEAD_DIM, (j + 1) * HGRN_HEAD_DIM)
                ob_ref[rows, p * pair + j * HGRN_HEAD_DIM:p * pair + (j + 1) * HGRN_HEAD_DIM] = _rms(o[:, hcols])
    span_ref[0] = jnp.broadcast_to(span, span_ref.shape[1:])


def _mixer(x, ada, g_pre, g_post, w_in, ln_g, ln_b, w_spatial, b_spatial, lb, g_hgrn, w_out,
           g_pre_ffn, w_rg, b_rg, w_re, b_re, stepwise):
    b, s, d = x.shape
    w = w_out.shape[0]
    n_groups, gchunk, _ = w_spatial.shape
    ts = min(MIXER_SEQ_TILE, s)
    assert s % ts == 0 and ts % gchunk == 0 and ts % HGRN_CHUNK == 0 and ts % LANES == 0
    n_rg, n_exp = w_rg.shape[1], w_re.shape[1]
    assert n_rg + n_exp <= LANES
    pad = LANES - n_rg - n_exp
    wrt = jnp.concatenate([w_rg, w_re, jnp.zeros((d, pad), _F32)], axis=1).T.astype(_BF16)
    brt = jnp.concatenate([b_rg, b_re, jnp.zeros((pad,), _F32)]).reshape(LANES, 1).astype(_F32)
    tiles = s // ts
    nt = b * tiles
    cur = lambda t: jnp.minimum(t, nt - 1)
    prev = lambda t: jnp.maximum(t - 1, 0)
    cur3 = lambda t: (cur(t), 0, 0)
    prev3 = lambda t: (prev(t), 0, 0)
    assert w % HGRN_HEAD_DIM == 0 and w // n_groups == LANES and w_in.shape == (d, N_IN_SLICES * w)
    assert w % PROJ_COLS == 0
    n_heads = w // HGRN_HEAD_DIM
    assert n_heads % 2 == 0 and HGRN_HEAD_DIM == LANES
    row = lambda a: a.reshape(1, -1).astype(_F32)
    const2 = lambda t: (0, 0)
    const3 = lambda t: (0, 0, 0)
    single = dict(pipeline_mode=pl.Buffered(1))
    x1, h2, meta, metat, cnt, span = pl.pallas_call(
        functools.partial(_mixer_kernel, n_route_groups=n_rg, per_group=n_exp // n_rg,
                          tiles_per_batch=tiles, stepwise=stepwise),
        grid=(nt + 1,),
        in_specs=[
            pl.BlockSpec((1, ts, d), cur3),
            pl.BlockSpec((1, ada.shape[1], d), lambda t: (cur(t) // tiles, 0, 0)),
            pl.BlockSpec((1, ada.shape[1], d), lambda t: (prev(t) // tiles, 0, 0)),
            pl.BlockSpec((1, d), const2),
            pl.BlockSpec((1, d), const2),
            pl.BlockSpec((d, N_IN_SLICES * w), const2, **single),
            pl.BlockSpec((1, w), const2),
            pl.BlockSpec((1, w), const2),
            pl.BlockSpec((n_groups, gchunk, gchunk), const3, **single),
            pl.BlockSpec((gchunk, n_groups), const2),
            pl.BlockSpec((1, w), const2),
            pl.BlockSpec((1, w), const2),
            pl.BlockSpec((w, d), const2, **single),
            pl.BlockSpec((1, d), const2),
            pl.BlockSpec((LANES, d), const2),
            pl.BlockSpec((LANES, 1), const2),
        ],
        out_specs=[
            pl.BlockSpec((1, ts, d), cur3),
            pl.BlockSpec((1, ts, d), prev3),
            pl.BlockSpec((1, ts, LANES), prev3),
            pl.BlockSpec((1, 8, ts), prev3),
            pl.BlockSpec((1, 8, LANES), prev3),
            pl.BlockSpec((1, 8, LANES), cur3),
        ],
        out_shape=[
            jax.ShapeDtypeStruct((nt, ts, d), _F32),
            jax.ShapeDtypeStruct((nt, ts, d), _BF16),
            jax.ShapeDtypeStruct((nt, ts, LANES), _F32),
            jax.ShapeDtypeStruct((nt, 8, ts), _F32),
            jax.ShapeDtypeStruct((nt, 8, LANES), jnp.int32),
            jax.ShapeDtypeStruct((nt, 8, LANES), _F32),
        ],
        scratch_shapes=[
            pltpu.VMEM((n_heads // 2, 2 * HGRN_HEAD_DIM, 2 * HGRN_HEAD_DIM), _F32),
            pltpu.VMEM((ts, w), _F32),
            pltpu.VMEM((ts, w), _F32),
            pltpu.VMEM((ts, d), _F32),
            pltpu.VMEM((N_IN_SLICES, ts, w), _F32),
            pltpu.VMEM((ts, w), _BF16),
            pltpu.VMEM((ts, w), _BF16),
        ],
        compiler_params=pltpu.CompilerParams(
            dimension_semantics=("arbitrary",),
            vmem_limit_bytes=VMEM_LIMIT_BYTES),
        name="mixer",
    )(x.reshape(nt, ts, d), ada, ada, row(g_pre), row(g_post), w_in.astype(_BF16), row(ln_g),
      row(ln_b), w_spatial.astype(_F32), b_spatial.T.astype(_F32), row(lb), row(g_hgrn),
      w_out.astype(_BF16), row(g_pre_ffn), wrt, brt)
    return x1.reshape(b, s, d), h2, meta, metat, cnt, span


def _first_max(vals):
    m = functools.reduce(jnp.maximum, vals)
    idx = jnp.full(m.shape, len(vals) - 1, jnp.int32)
    for j in range(len(vals) - 2, -1, -1):
        idx = jnp.where(vals[j] == m, j, idx)
    return m, idx


def _route(lt, n_groups, per_group):
    lg = [lt[g:g + 1] for g in range(n_groups)]
    mg, g_idx = _first_max(lg)
    p_top = 1.0 / functools.reduce(lambda a, b: a + b, [jnp.exp(l - mg) for l in lg])
    le = []
    for j in range(per_group):
        v = lt[n_groups + j:n_groups + j + 1]
        for g in range(1, n_groups):
            r0 = n_groups + g * per_group + j
            v = jnp.where(g_idx == g, lt[r0:r0 + 1], v)
        le.append(v)
    m1, j1 = _first_max(le)
    m2, j2 = _first_max([jnp.where(j1 == j, -jnp.inf, v) for j, v in enumerate(le)])
    r = jnp.exp(m2 - m1)
    w1 = p_top / (1.0 + r)
    w2 = w1 * r
    return g_idx * per_group + j1, g_idx * per_group + j2, w1, w2


def _route_tile(x, sh2, sc2, gpre, wrt, brt, h_ref, meta_ref, metat_ref, cnt_ref, n_groups, per_group):
    tm = x.shape[0]
    h = (_rms(x) * gpre) * (1.0 + sc2) + sh2
    hb = h.astype(_BF16)
    h_ref[0] = hb
    lt = _dot_nt(wrt, hb) + brt
    e1, e2, w1, w2 = _route(lt, n_groups, per_group)

    sub = lax.broadcasted_iota(jnp.int32, (LANES, tm), 0)
    ind = jnp.where((sub == e1) | (sub == e2), 1.0, 0.0)
    earlier = (lax.broadcasted_iota(jnp.int32, (tm, tm), 0)
               < lax.broadcasted_iota(jnp.int32, (tm, tm), 1))
    rank = _dot(ind.astype(_BF16), jnp.where(earlier, 1.0, 0.0).astype(_BF16))
    cnt = jnp.sum(ind, axis=1, keepdims=True)
    chunks = jnp.floor((cnt + (CHUNK_ROWS - 1)) * (1.0 / CHUNK_ROWS))
    below = (lax.broadcasted_iota(jnp.int32, (LANES, LANES), 1)
             < lax.broadcasted_iota(jnp.int32, (LANES, LANES), 0))
    first_chunk = _dot(jnp.where(below, 1.0, 0.0).astype(_BF16),
                       jnp.broadcast_to(chunks, (LANES, LANES)).astype(_BF16))
    dest = first_chunk[:, 0:1] * CHUNK_ROWS + rank
    n_rows = -(-n_groups * per_group // 8) * 8
    sub_e = lax.broadcasted_iota(jnp.int32, (n_rows, tm), 0)
    d1 = jnp.sum(jnp.where(sub_e == e1, dest[:n_rows], 0.0), axis=0, keepdims=True)
    d2 = jnp.sum(jnp.where(sub_e == e2, dest[:n_rows], 0.0), axis=0, keepdims=True)
    row8 = lax.broadcasted_iota(jnp.int32, (8, tm), 0)
    metat = jnp.where(row8 == 0, d1, jnp.where(row8 == 1, d2,
                      jnp.where(row8 == 2, w1, jnp.where(row8 == 3, w2, 0.0))))
    metat_ref[0] = metat
    meta_ref[0] = jnp.concatenate([metat, jnp.zeros((LANES - 8, tm), _F32)], axis=0).T
    cnt_ref[0] = jnp.broadcast_to(cnt, (LANES, LANES)).T[0:8].astype(jnp.int32)


def _chunk_loop(n, fn):
    def body(c, carry):
        fn(c)
        return carry
    lax.fori_loop(0, n, body, 0)


def _wait_chunks(n, max_chunks, copy_of_chunks):
    bit = 1
    while bit * 2 <= max_chunks:
        bit *= 2
    while bit >= 1:
        @pl.when((n & bit) != 0)
        def _(bit=bit):
            copy_of_chunks(bit).wait()
        bit //= 2


def _dispatch_kernel(nch_ref, cmap_ref, tbase_ref, ntail_ref, nu_ref, metat_ref, h_ref, xs_ref,
                     loc_ref, zero_ref, sem, zsem, *, max_chunks):
    i = pl.program_id(0)
    n_steps = pl.num_programs(0)
    g = h_ref.shape[0]
    lrows, tm = loc_ref.shape[1], h_ref.shape[1]
    buf = lambda step, j: (step % 2) * g + j

    def chunk_copy(slot_, c, dst_chunk):
        return pltpu.make_async_copy(
            loc_ref.at[slot_, pl.ds(pl.multiple_of(c * CHUNK_ROWS, CHUNK_ROWS), CHUNK_ROWS)],
            xs_ref.at[pl.ds(pl.multiple_of(dst_chunk * CHUNK_ROWS, CHUNK_ROWS), CHUNK_ROWS)],
            sem.at[slot_])

    def zero_copy(dst_chunk):
        return pltpu.make_async_copy(
            zero_ref.at[pl.ds(0, CHUNK_ROWS)],
            xs_ref.at[pl.ds(pl.multiple_of(dst_chunk * CHUNK_ROWS, CHUNK_ROWS), CHUNK_ROWS)],
            zsem.at[0])

    def zero_tile_copy(tile):
        return pltpu.make_async_copy(
            zero_ref,
            xs_ref.at[pl.ds(pl.multiple_of(tile * GMM_ROW_TILE, GMM_ROW_TILE), GMM_ROW_TILE)],
            zsem.at[1])

    @pl.when(i == 0)
    def _():
        zero_ref[...] = jnp.zeros(zero_ref.shape, zero_ref.dtype)
        n_unused = xs_ref.shape[0] // GMM_ROW_TILE - nu_ref[0]
        for e in range(tbase_ref.shape[0]):
            _chunk_loop(ntail_ref[e], lambda j, e=e: zero_copy(tbase_ref[e] + j).start())
        _chunk_loop(n_unused, lambda j: zero_tile_copy(nu_ref[0] + j).start())
        for e in range(tbase_ref.shape[0]):
            _chunk_loop(ntail_ref[e], lambda j: zero_copy(0).wait())
        _chunk_loop(n_unused, lambda j: zero_tile_copy(0).wait())

    def wait_slot(slot_, n_):
        _wait_chunks(n_, max_chunks, lambda k: pltpu.make_async_copy(
            loc_ref.at[slot_, pl.ds(0, k * CHUNK_ROWS)], xs_ref.at[pl.ds(0, k * CHUNK_ROWS)],
            sem.at[slot_]))

    @pl.when(i >= 2)
    def _():
        for j in range(g):
            wait_slot(buf(i, j), nch_ref[(i - 2) * g + j])

    r = lax.broadcasted_iota(jnp.int32, (lrows, tm), 0)
    lane = lax.broadcasted_iota(jnp.int32, (lrows, LANES), 1)
    d = h_ref.shape[2]
    for j in range(g):
        mt = metat_ref[j]
        is1 = r == mt[0:1].astype(jnp.int32)
        is2 = r == mt[1:2].astype(jnp.int32)
        sel = jnp.where(is1 | is2, 1.0, 0.0).astype(_BF16)
        loc_ref[buf(i, j), :, 0:d] = _dot(sel, h_ref[j]).astype(_BF16)
        ws = jnp.sum(jnp.where(is1, mt[2:3], 0.0) + jnp.where(is2, mt[3:4], 0.0),
                     axis=1, keepdims=True)
        hi, mid, lo = (p.astype(_F32) for p in _split3(ws))
        loc_ref[buf(i, j), :, d:d + LANES] = jnp.where(
            lane == 0, hi, jnp.where(lane == 1, mid, jnp.where(lane == 2, lo, 0.0))).astype(_BF16)
    for j in range(g):
        tile = i * g + j
        _chunk_loop(nch_ref[tile], lambda c, j=j, tile=tile: chunk_copy(
            buf(i, j), c, cmap_ref[tile * max_chunks + c]).start())

    @pl.when(i == n_steps - 1)
    def _():
        for j in range(g):
            wait_slot(buf(i, j), nch_ref[i * g + j])

        @pl.when(i >= 1)
        def _():
            for j in range(g):
                wait_slot(buf(i - 1, j), nch_ref[(i - 1) * g + j])


def _experts_kernel(te_ref, tf_ref, nu_ref, xs_ref, w1_ref, w3_ref, w2_ref, ys_ref,
                    wb1_ref, wb3_ref, wb2_ref):
    i = pl.program_id(0)

    @pl.when(i < nu_ref[0])
    def _():
        @pl.when(tf_ref[i] == 1)
        def _():
            wb1_ref[...] = w1_ref[0].astype(_BF16)
            wb3_ref[...] = w3_ref[0].astype(_BF16)
            wb2_ref[...] = w2_ref[0].astype(_BF16)

        d = wb1_ref.shape[0]
        xs = xs_ref[:, 0:d]
        half = wb1_ref.shape[1] // 2
        acc = None
        for hcols in (slice(0, half), slice(half, 2 * half)):
            a = _silu(_dot(xs, wb1_ref[:, hcols])) * _dot(xs, wb3_ref[:, hcols])
            part = _dot(a.astype(_BF16), wb2_ref[hcols, :])
            acc = part if acc is None else acc + part
        wp = xs_ref[:, d:d + LANES].astype(_F32)
        ys_ref[...] = (acc * (wp[:, 0:1] + wp[:, 1:2] + wp[:, 2:3])).astype(_BF16)

    @pl.when(i >= nu_ref[0])
    def _():
        ys_ref[...] = jnp.zeros(ys_ref.shape, ys_ref.dtype)


def _combine_kernel(nch_ref, cmap_ref, x_ref, ada_ref, gpost_ref, meta_ref, ys_ref, o_ref,
                    loc_ref, sem, *, max_chunks):
    i = pl.program_id(0)
    n_steps = pl.num_programs(0)
    g = x_ref.shape[0]
    lrows, tm = loc_ref.shape[1], x_ref.shape[1]
    buf = lambda step, j: (step % 2) * g + j

    def chunk_copy(tile, slot_, c):
        src_chunk = cmap_ref[tile * max_chunks + c]
        return pltpu.make_async_copy(
            ys_ref.at[pl.ds(pl.multiple_of(src_chunk * CHUNK_ROWS, CHUNK_ROWS), CHUNK_ROWS)],
            loc_ref.at[slot_, pl.ds(pl.multiple_of(c * CHUNK_ROWS, CHUNK_ROWS), CHUNK_ROWS)],
            sem.at[slot_])

    def fetch(step):
        for j in range(g):
            tile = step * g + j
            _chunk_loop(nch_ref[tile], lambda c, j=j, tile=tile: chunk_copy(tile, buf(step, j), c).start())

    @pl.when(i == 0)
    def _():
        loc_ref[...] = jnp.zeros(loc_ref.shape, loc_ref.dtype)
        fetch(0)

    @pl.when(i + 1 < n_steps)
    def _():
        fetch(i + 1)

    for j in range(g):
        _wait_chunks(nch_ref[i * g + j], max_chunks, lambda k, j=j: pltpu.make_async_copy(
            ys_ref.at[pl.ds(0, k * CHUNK_ROWS)], loc_ref.at[buf(i, j), pl.ds(0, k * CHUNK_ROWS)],
            sem.at[buf(i, j)]))

    r = lax.broadcasted_iota(jnp.int32, (tm, lrows), 1)
    gt2 = ada_ref[0][5:6]
    for j in range(g):
        meta = meta_ref[j]
        d1 = meta[:, 0:1].astype(jnp.int32)
        d2 = meta[:, 1:2].astype(jnp.int32)
        sel = jnp.where((r == d1) | (r == d2), 1.0, 0.0).astype(_BF16)
        y = _dot(sel, loc_ref[buf(i, j)])
        o_ref[j] = x_ref[j] + gt2 * (_rms(y) * gpost_ref[...])


def _moe_plan(cnt, max_chunks, n_gmm_tiles):
    per_tile = GMM_ROW_TILE // CHUNK_ROWS
    pc = (cnt + (CHUNK_ROWS - 1)) // CHUNK_ROWS
    local_first = jnp.cumsum(pc, axis=1) - pc
    nch = jnp.sum(pc, axis=1)
    e_chunks = jnp.sum(pc, axis=0)
    e_region = ((e_chunks + per_tile - 1) // per_tile) * per_tile
    e_first = jnp.cumsum(e_region) - e_region
    seg_first = e_first[None, :] + jnp.cumsum(pc, axis=0) - pc
    c = jnp.arange(max_chunks, dtype=jnp.int32)[None, :, None]
    inside = (c >= local_first[:, None, :]) & (c < (local_first + pc)[:, None, :])
    cmap = jnp.sum(jnp.where(inside, seg_first[:, None, :] + c - local_first[:, None, :], 0), axis=-1)
    n_used = jnp.sum(e_region) // per_tile
    t = jnp.arange(n_gmm_tiles, dtype=jnp.int32)
    t_used = jnp.minimum(t, n_used - 1)
    tile_e = jnp.sum(t_used[:, None] * per_tile >= (e_first + e_region)[None, :], axis=1)
    tile_first = jnp.concatenate([jnp.ones((1,), jnp.int32),
                                  (tile_e[1:] != tile_e[:-1]).astype(jnp.int32)])
    i32 = lambda a: a.astype(jnp.int32)
    return dict(nch=i32(nch), cmap=i32(cmap.reshape(-1)), tail_first=i32(e_first + e_chunks),
                n_tail=i32(e_region - e_chunks), tile_e=i32(tile_e), tile_first=tile_first,
                n_used=i32(n_used.reshape(1)))


def _moe(x, h2, meta, metat, cnt, ada, g_post, n_groups, w1, w3, w2):
    b, s, d = x.shape
    n_exp, _, ff = w1.shape
    tm = h2.shape[1]
    assert s % tm == 0 and tm % LANES == 0 and GMM_ROW_TILE % CHUNK_ROWS == 0
    tiles_per_batch = s // tm
    nt = b * tiles_per_batch
    max_chunks = (TOP_K_IN_GROUP * tm + n_exp * (CHUNK_ROWS - 1)) // CHUNK_ROWS
    lrows = -(-max_chunks * CHUNK_ROWS // LANES) * LANES
    per_tile = GMM_ROW_TILE // CHUNK_ROWS
    n_gmm_tiles = -(-(nt * max_chunks + n_exp * (per_tile - 1)) // per_tile)
    n_rows = n_gmm_tiles * GMM_ROW_TILE
    dx = d + LANES

    row = lambda a: a.reshape(1, -1).astype(_F32)
    n_ada = ada.shape[1]
    xt = x.reshape(nt, tm, d)
    g = MOE_TILES_PER_STEP if tiles_per_batch % MOE_TILES_PER_STEP == 0 else 1
    steps_per_batch = tiles_per_batch // g

    plan = _moe_plan(cnt[:, 0, :n_exp], max_chunks, n_gmm_tiles)

    xs = pl.pallas_call(
        functools.partial(_dispatch_kernel, max_chunks=max_chunks),
        grid_spec=pltpu.PrefetchScalarGridSpec(
            num_scalar_prefetch=5,
            grid=(nt // g,),
            in_specs=[
                pl.BlockSpec((g, 8, tm), lambda i, *_: (i, 0, 0)),
                pl.BlockSpec((g, tm, d), lambda i, *_: (i, 0, 0)),
            ],
            out_specs=pl.BlockSpec(memory_space=pl.ANY),
            scratch_shapes=[
                pltpu.VMEM((2 * g, lrows, dx), _BF16),
                pltpu.VMEM((GMM_ROW_TILE, dx), _BF16),
                pltpu.SemaphoreType.DMA((2 * g,)),
                pltpu.SemaphoreType.DMA((2,)),
            ],
        ),
        out_shape=jax.ShapeDtypeStruct((n_rows, dx), _BF16),
        compiler_params=pltpu.CompilerParams(dimension_semantics=("arbitrary",)),
        name="dispatch",
    )(plan["nch"], plan["cmap"], plan["tail_first"], plan["n_tail"], plan["n_used"], metat, h2)

    ys = pl.pallas_call(
        _experts_kernel,
        grid_spec=pltpu.PrefetchScalarGridSpec(
            num_scalar_prefetch=3,
            grid=(n_gmm_tiles,),
            in_specs=[
                pl.BlockSpec((GMM_ROW_TILE, dx), lambda i, te, tf, nu: (jnp.minimum(i, nu[0] - 1), 0)),
                pl.BlockSpec((1, d, ff), lambda i, te, tf, nu: (te[i], 0, 0)),
                pl.BlockSpec((1, d, ff), lambda i, te, tf, nu: (te[i], 0, 0)),
                pl.BlockSpec((1, ff, d), lambda i, te, tf, nu: (te[i], 0, 0)),
            ],
            out_specs=pl.BlockSpec((GMM_ROW_TILE, d), lambda i, te, tf, nu: (i, 0)),
            scratch_shapes=[
                pltpu.VMEM((d, ff), _BF16),
                pltpu.VMEM((d, ff), _BF16),
                pltpu.VMEM((ff, d), _BF16),
            ],
        ),
        out_shape=jax.ShapeDtypeStruct((n_rows, d), _BF16),
        compiler_params=pltpu.CompilerParams(dimension_semantics=("arbitrary",),
                                             vmem_limit_bytes=VMEM_LIMIT_BYTES),
        name="experts",
    )(plan["tile_e"], plan["tile_first"], plan["n_used"], xs, w1, w3, w2)

    out = pl.pallas_call(
        functools.partial(_combine_kernel, max_chunks=max_chunks),
        grid_spec=pltpu.PrefetchScalarGridSpec(
            num_scalar_prefetch=2,
            grid=(nt // g,),
            in_specs=[
                pl.BlockSpec((g, tm, d), lambda i, *_: (i, 0, 0)),
                pl.BlockSpec((1, n_ada, d), lambda i, *_: (i // steps_per_batch, 0, 0)),
                pl.BlockSpec((1, d), lambda i, *_: (0, 0)),
                pl.BlockSpec((g, tm, LANES), lambda i, *_: (i, 0, 0)),
                pl.BlockSpec(memory_space=pl.ANY),
            ],
            out_specs=pl.BlockSpec((g, tm, d), lambda i, *_: (i, 0, 0)),
            scratch_shapes=[
                pltpu.VMEM((2 * g, lrows, d), _BF16),
                pltpu.SemaphoreType.DMA((2 * g,)),
            ],
        ),
        out_shape=jax.ShapeDtypeStruct((nt, tm, d), _F32),
        compiler_params=pltpu.CompilerParams(dimension_semantics=("arbitrary",)),
        name="combine",
    )(plan["nch"], plan["cmap"], xt, ada, row(g_post), meta, ys)
    return out.reshape(b, s, d)


def kernel(x, c, w_ada, b_ada, g_pre_mix, g_post_mix, w_in, ln_v_g, ln_v_b, w_spatial, b_spatial,
           lb_logits, g_hgrn_norm, w_out, g_pre_ffn, g_post_ffn, w_router_group, b_router_group,
           w_router_expert, b_router_expert, w1, w3, w2):
    depth = w_in.shape[0]
    b, s, d = x.shape
    lb_all = _lower_bounds(lb_logits)
    for layer in range(depth):
        ada = _ada(c, w_ada[layer], b_ada[layer]).reshape(b, 6, d)
        mix = functools.partial(
            _mixer, x, ada, g_pre_mix[layer], g_post_mix[layer], w_in[layer], ln_v_g[layer],
            ln_v_b[layer], w_spatial[layer], b_spatial[layer], lb_all[layer], g_hgrn_norm[layer],
            w_out[layer], g_pre_ffn[layer], w_router_group[layer], b_router_group[layer],
            w_router_expert[layer], b_router_expert[layer])
        *mixed, span = mix(stepwise=False)
        x, h2, meta, metat, cnt = lax.cond(
            jnp.max(span) > HGRN_MAX_SPAN,
            lambda: tuple(mix(stepwise=True)[:5]),
            lambda: tuple(mixed))
        x = _moe(x, h2, meta, metat, cnt, ada, g_post_ffn[layer], w_router_group.shape[-1],
                 w1[layer], w3[layer], w2[layer])
    return x
```

```python
import functools

import jax
import jax.numpy as jnp
from jax import lax
from jax.experimental import pallas as pl
from jax.experimental.pallas import tpu as pltpu

RMS_EPS = 1e-6
LN_EPS = 1e-5
HGRN_HEAD_DIM = 128
HGRN_CHUNK = 128
HGRN_MAX_SPAN = 60.0
N_IN_SLICES = 8
TOP_K_IN_GROUP = 2
LANES = 128
MIXER_SEQ_TILE = 256
PROJ_COLS = 256
CHUNK_ROWS = 16
GMM_ROW_TILE = 1024
MOE_TILES_PER_STEP = 4
VMEM_LIMIT_BYTES = 56 * 1024 * 1024

_F32 = jnp.float32
_BF16 = jnp.bfloat16


def _sigmoid(x):
    return 0.5 * (jnp.tanh(0.5 * x) + 1.0)


def _silu(x):
    return x * _sigmoid(x)


def _gelu_tanh(x):
    c = 0.7978845608028654
    return 0.5 * x * (1.0 + jnp.tanh(c * (x + 0.044715 * (x * x * x))))


def _rms(x):
    return x * lax.rsqrt(jnp.mean(x * x, axis=-1, keepdims=True) + RMS_EPS)


def _dot(a, b):
    return jnp.dot(a, b, preferred_element_type=_F32)


def _dot_nt(a, b):
    return lax.dot_general(a, b, (((1,), (1,)), ((), ())), preferred_element_type=_F32)


def _dot_tn(a, b):
    return lax.dot_general(a, b, (((0,), (0,)), ((), ())), preferred_element_type=_F32)


def _split3(x):
    hi = x.astype(_BF16)
    r = x - hi.astype(_F32)
    mid = r.astype(_BF16)
    lo = (r - mid.astype(_F32)).astype(_BF16)
    return hi, mid, lo


def _ada_kernel(c_ref, w_ref, b_ref, o_ref):
    s = _silu(c_ref[...])
    w = w_ref[...]
    acc = jnp.zeros(o_ref.shape, _F32)
    for sp in _split3(s):
        for wp in _split3(w)[:2]:
            acc = acc + _dot(sp, wp)
    o_ref[...] = acc + b_ref[...]


def _ada(c, w_ada, b_ada):
    b, d = c.shape
    n = w_ada.shape[1]
    tn = d
    return pl.pallas_call(
        _ada_kernel,
        grid=(n // tn,),
        in_specs=[
            pl.BlockSpec((b, d), lambda j: (0, 0)),
            pl.BlockSpec((d, tn), lambda j: (0, j)),
            pl.BlockSpec((1, tn), lambda j: (0, j)),
        ],
        out_specs=pl.BlockSpec((b, tn), lambda j: (0, j)),
        out_shape=jax.ShapeDtypeStruct((b, n), _F32),
        name="ada",
    )(c, w_ada, b_ada.reshape(1, n))


def _lb_kernel(l_ref, o_ref):
    l = l_ref[...]
    e = jnp.exp(l - jnp.max(l, axis=0, keepdims=True))
    p = e / jnp.sum(e, axis=0, keepdims=True)
    rows = [p[0:1]]
    for i in range(1, p.shape[0]):
        rows.append(rows[-1] + p[i:i + 1])
    for i, r in enumerate(rows):
        o_ref[i:i + 1, :] = r


def _lower_bounds(lb_logits):
    return pl.pallas_call(
        _lb_kernel,
        out_shape=jax.ShapeDtypeStruct(lb_logits.shape, _F32),
        name="lower_bounds",
    )(lb_logits.astype(_F32))


def _mixer_kernel(x_ref, ada_ref, adap_ref, gpre_ref, gpost_ref, win_ref, lng_ref, lnb_ref, wsp_ref,
                  bspt_ref, lb_ref, ghg_ref, wout_ref, gpre2_ref, wrt_ref, brt_ref,
                  o_ref, h2_ref, meta_ref, metat_ref, cnt_ref, span_ref, st_ref, ya_ref, ob_ref,
                  x1p_ref, act_ref, iv_ref, vn_ref,
                  *, n_route_groups, per_group, tiles_per_batch, stepwise):
    ts, d = x_ref.shape[1], x_ref.shape[2]
    w = wout_ref.shape[0]
    n_groups, gchunk = wsp_ref.shape[0], wsp_ref.shape[1]
    gdim = w // n_groups
    t = pl.program_id(0)
    is_tile = t < pl.num_programs(0) - 1

    @pl.when(t % tiles_per_batch == 0)
    def _():
        st_ref[...] = jnp.zeros(st_ref.shape, _F32)

    @pl.when(t == 0)
    def _():
        x1p_ref[...] = jnp.zeros(x1p_ref.shape, _F32)

    x1_prev = x1p_ref[...]
    adap = adap_ref[0]
    _route_tile(x1_prev, adap[3:4], adap[4:5], gpre2_ref[...], wrt_ref[...], brt_ref[...],
                h2_ref, meta_ref, metat_ref, cnt_ref, n_route_groups, per_group)

    x = x_ref[0]
    ada = ada_ref[0]
    sh1, sc1, gt1 = ada[0:1], ada[1:2], ada[2:3]
    h = (_rms(x) * gpre_ref[...]) * (1.0 + sc1) + sh1
    h_bf = h.astype(_BF16)

    lb = lb_ref[...]
    (s_v, s_u, s_q, s_f, s_i, s_og, s_ga, s_gb) = (1, 0, 2, 3, 4, 5, 6, 7)
    (a_u, a_v, a_q, a_lf, a_k, a_og, a_ga, a_gb) = range(8)

    def activate(j, cols, r):
        if j == s_u:
            act_ref[a_u, :, cols] = _gelu_tanh(r)
        elif j == s_v:
            act_ref[a_v, :, cols] = _gelu_tanh(r)
        elif j == s_q:
            act_ref[a_q, :, cols] = _silu(r)
        elif j == s_f:
            f = lb[:, cols] + (1.0 - lb[:, cols]) * _sigmoid(r)
            act_ref[a_lf, :, cols] = jnp.log(f)
            act_ref[a_k, :, cols] = 1.0 - f
        elif j == s_i:
            iv_ref[:, cols] = r.astype(_BF16)
        elif j == s_og:
            act_ref[a_og, :, cols] = _silu(r)
        elif j == s_ga:
            act_ref[a_ga, :, cols] = _sigmoid(r)
        else:
            act_ref[a_gb, :, cols] = _sigmoid(r)

    def layer_norm_v():
        v = act_ref[a_v]
        mu = jnp.mean(v, axis=-1, keepdims=True)
        vc = v - mu
        var = jnp.mean(vc * vc, axis=-1, keepdims=True)
        vn_ref[...] = ((vc * lax.rsqrt(var + LN_EPS)) * lng_ref[...] + lnb_ref[...]).astype(_BF16)

    order = (s_v, s_u, s_q, s_f, s_i, s_og, s_ga, s_gb)
    pending = None
    for pos, j in enumerate(order):
        for cb in range(w // PROJ_COLS):
            cols = slice(cb * PROJ_COLS, (cb + 1) * PROJ_COLS)
            r = _dot(h_bf, win_ref[:, j * w + cb * PROJ_COLS:j * w + (cb + 1) * PROJ_COLS])
            if pending is not None:
                pending()
            pending = functools.partial(activate, j, cols, r)
            if pos == 1 and cb == 0:
                layer_norm_v()
    pending()

    u = act_ref[a_u]
    vn = vn_ref[...]
    tri_g = (lax.broadcasted_iota(jnp.int32, (gchunk, gchunk), 0)
             >= lax.broadcasted_iota(jnp.int32, (gchunk, gchunk), 1))
    bspt = bspt_ref[...]
    n_gchunks = ts // gchunk
    for g in range(n_groups):
        wc = jnp.where(tri_g, wsp_ref[g], 0.0).astype(_BF16)
        cols = slice(g * gdim, (g + 1) * gdim)
        vg = jnp.concatenate([vn[n * gchunk:(n + 1) * gchunk, cols] for n in range(n_gchunks)], axis=1)
        zv = _dot(wc, vg)
        for n in range(n_gchunks):
            rows = slice(n * gchunk, (n + 1) * gchunk)
            ya_ref[rows, cols] = u[rows, cols] * (zv[:, n * gdim:(n + 1) * gdim] + bspt[:, g:g + 1])

    pair = 2 * HGRN_HEAD_DIM
    pr = lax.broadcasted_iota(jnp.int32, (pair, pair), 0)
    pc_i = lax.broadcasted_iota(jnp.int32, (pair, pair), 1)
    same_head = (pr < HGRN_HEAD_DIM) == (pc_i < HGRN_HEAD_DIM)
    if stepwise:
        _hgrn_stepwise(act_ref.at[a_q], act_ref.at[a_k], iv_ref, st_ref, ob_ref, same_head)
        span_ref[0] = jnp.zeros(span_ref.shape[1:], _F32)
    else:
        _hgrn_chunkwise(act_ref.at[a_q], act_ref.at[a_lf], act_ref.at[a_k], iv_ref, st_ref, ob_ref,
                        span_ref, same_head)

    yb = (ob_ref[...] * ghg_ref[...]) * act_ref[a_og]
    y = act_ref[a_ga] * ya_ref[...] + act_ref[a_gb] * yb
    out = _dot(y.astype(_BF16), wout_ref[...])
    x1 = jnp.where(is_tile, x + gt1 * (_rms(out) * gpost_ref[...]), x1_prev)
    o_ref[0] = x1
    x1p_ref[...] = x1


def _hgrn_stepwise(q_ref, k_ref, iv_ref, st_ref, ob_ref, same_head):
    ts, w = q_ref.shape
    pair = 2 * HGRN_HEAD_DIM
    sub = 16

    def group(gi, carry):
        r0 = pl.multiple_of(gi * sub, sub)
        q_g = q_ref[pl.ds(r0, sub), :]
        k_g = k_ref[pl.ds(r0, sub), :]
        v_g = iv_ref[pl.ds(r0, sub), :].astype(_F32)
        o_rows = []
        for i in range(sub):
            q_t, k_t, v_t = q_g[i:i + 1], k_g[i:i + 1], v_g[i:i + 1]
            o_t = []
            for p in range(w // pair):
                pcols = slice(p * pair, (p + 1) * pair)
                upd = _dot_tn(v_t[:, pcols].astype(_BF16), k_t[:, pcols].astype(_BF16))
                st = st_ref[p] * (1.0 - k_t[:, pcols]) + jnp.where(same_head, upd, 0.0)
                st_ref[p] = st
                o_p = _dot_nt(q_t[:, pcols].astype(_BF16), st.astype(_BF16))
                o_t += [_rms(o_p[:, :HGRN_HEAD_DIM]), _rms(o_p[:, HGRN_HEAD_DIM:])]
            o_rows.append(jnp.concatenate(o_t, axis=1))
        ob_ref[pl.ds(r0, sub), :] = jnp.concatenate(o_rows, axis=0)
        return carry

    lax.fori_loop(0, ts // sub, group, 0)


def _hgrn_chunkwise(q_ref, lf_ref, k_ref, iv_ref, st_ref, ob_ref, span_ref, same_head):
    ts, w = q_ref.shape
    n_heads = w // HGRN_HEAD_DIM
    c = HGRN_CHUNK
    q = q_ref[...]
    lf = lf_ref[...]
    k = k_ref[...]
    iv = iv_ref[...]
    r_i = lax.broadcasted_iota(jnp.int32, (ts, ts), 0)
    c_i = lax.broadcasted_iota(jnp.int32, (ts, ts), 1)
    ltri = jnp.where((r_i // c == c_i // c) & (c_i <= r_i), 1.0, 0.0).astype(_BF16)
    bcum = jnp.zeros((ts, w), _F32)
    for part in _split3(lf)[:2]:
        bcum = bcum + _dot(ltri, part)
    hb = c // 2
    row_h = lax.broadcasted_iota(jnp.int32, (hb, c), 0)
    lane_h = lax.broadcasted_iota(jnp.int32, (hb, c), 1)
    top_mask = lane_h <= row_h
    left = lane_h < hb
    bot_mask = lane_h - hb <= row_h
    pair = 2 * HGRN_HEAD_DIM
    up_rows = lax.broadcasted_iota(jnp.int32, (c, pair), 1) < HGRN_HEAD_DIM
    span = jnp.zeros((1, 1), _F32)
    for n in range(ts // c):
        rows = slice(n * c, (n + 1) * c)
        bc = bcum[rows]
        b_a, b_m, b_b, b_l = (bc[hb // 2 - 1:hb // 2], bc[hb - 1:hb],
                              bc[hb + hb // 2 - 1:hb + hb // 2], bc[c - 1:c])
        stack = lambda ra, rb: jnp.concatenate(
            [jnp.broadcast_to(ra, (hb, w)), jnp.broadcast_to(rb, (hb, w))], axis=0)
        ref = stack(b_a, b_b)
        ends = jnp.maximum(jnp.maximum(bc[0:1] - b_a, b_a - b_m),
                           jnp.maximum(bc[hb:hb + 1] - b_b, b_b - b_l))
        span = jnp.maximum(span, jnp.max(ends, axis=1, keepdims=True))
        qx = q[rows] * jnp.exp(bc - ref)
        kx = k[rows] * jnp.exp(ref - bc)
        qd = (qx * stack(jnp.exp(b_a), jnp.exp(b_b))).astype(_BF16)
        kd = (kx * stack(jnp.exp(b_l - b_a), jnp.exp(b_l - b_b))).astype(_BF16)
        q_off = qx[hb:] * jnp.exp(b_b - b_m)
        k_off = kx[:hb] * jnp.exp(b_m - b_a)
        q3 = jnp.concatenate([qx, q_off], axis=0).astype(_BF16)
        k3 = jnp.concatenate([kx, k_off, jnp.zeros((hb, w), _F32)], axis=0).astype(_BF16)
        dec = jnp.exp(b_l)
        iv_c = iv[rows]
        for p in range(n_heads // 2):
            pcols = slice(p * pair, (p + 1) * pair)
            a_heads = []
            for hd in (2 * p, 2 * p + 1):
                cols = slice(hd * HGRN_HEAD_DIM, (hd + 1) * HGRN_HEAD_DIM)
                sc = _dot_nt(q3[:, cols], k3[:, cols])
                top = jnp.where(top_mask, sc[0:hb, 0:c], 0.0)
                bot = jnp.where(left, sc[2 * hb:3 * hb, c:2 * c],
                                jnp.where(bot_mask, sc[hb:2 * hb, 0:c], 0.0))
                a_heads.append(jnp.concatenate([top, bot], axis=0))
            a_pair = jnp.concatenate(a_heads, axis=1).astype(_BF16)
            iv_p = iv_c[:, pcols]
            iv_blk = jnp.concatenate([jnp.where(up_rows, iv_p, 0.0).astype(_BF16),
                                      jnp.where(up_rows, 0.0, iv_p).astype(_BF16)], axis=0)
            st = st_ref[p]
            o = _dot(a_pair, iv_blk) + _dot_nt(qd[:, pcols], st.astype(_BF16))
            upd = _dot_tn(iv_p, kd[:, pcols])
            st_ref[p] = st * dec[:, pcols] + jnp.where(same_head, upd, 0.0)
            for j in range(2):
                hcols = slice(j * HGRN_HEAD_DIM, (j + 1) * HGRN_HEAD_DIM)
                ob_ref[rows, p * pair + j * HGRN_HEAD_DIM:p * pair + (j + 1) * HGRN_HEAD_DIM] = _rms(o[:, hcols])
    span_ref[0] = jnp.broadcast_to(span, span_ref.shape[1:])


def _mixer(x, ada, g_pre, g_post, w_in, ln_g, ln_b, w_spatial, b_spatial, lb, g_hgrn, w_out,
           g_pre_ffn, w_rg, b_rg, w_re, b_re, stepwise):
    b, s, d = x.shape
    w = w_out.shape[0]
    n_groups, gchunk, _ = w_spatial.shape
    ts = min(MIXER_SEQ_TILE, s)
    assert s % ts == 0 and ts % gchunk == 0 and ts % HGRN_CHUNK == 0 and ts % LANES == 0
    n_rg, n_exp = w_rg.shape[1], w_re.shape[1]
    assert n_rg + n_exp <= LANES
    pad = LANES - n_rg - n_exp
    wrt = jnp.concatenate([w_rg, w_re, jnp.zeros((d, pad), _F32)], axis=1).T.astype(_BF16)
    brt = jnp.concatenate([b_rg, b_re, jnp.zeros((pad,), _F32)]).reshape(LANES, 1).astype(_F32)
    tiles = s // ts
    nt = b * tiles
    cur = lambda t: jnp.minimum(t, nt - 1)
    prev = lambda t: jnp.maximum(t - 1, 0)
    cur3 = lambda t: (cur(t), 0, 0)
    prev3 = lambda t: (prev(t), 0, 0)
    assert w % HGRN_HEAD_DIM == 0 and w // n_groups == LANES and w_in.shape == (d, N_IN_SLICES * w)
    assert w % PROJ_COLS == 0
    n_heads = w // HGRN_HEAD_DIM
    assert n_heads % 2 == 0 and HGRN_HEAD_DIM == LANES
    row = lambda a: a.reshape(1, -1).astype(_F32)
    const2 = lambda t: (0, 0)
    const3 = lambda t: (0, 0, 0)
    single = dict(pipeline_mode=pl.Buffered(1))
    x1, h2, meta, metat, cnt, span = pl.pallas_call(
        functools.partial(_mixer_kernel, n_route_groups=n_rg, per_group=n_exp // n_rg,
                          tiles_per_batch=tiles, stepwise=stepwise),
        grid=(nt + 1,),
        in_specs=[
            pl.BlockSpec((1, ts, d), cur3),
            pl.BlockSpec((1, ada.shape[1], d), lambda t: (cur(t) // tiles, 0, 0)),
            pl.BlockSpec((1, ada.shape[1], d), lambda t: (prev(t) // tiles, 0, 0)),
            pl.BlockSpec((1, d), const2),
            pl.BlockSpec((1, d), const2),
            pl.BlockSpec((d, N_IN_SLICES * w), const2, **single),
            pl.BlockSpec((1, w), const2),
            pl.BlockSpec((1, w), const2),
            pl.BlockSpec((n_groups, gchunk, gchunk), const3, **single),
            pl.BlockSpec((gchunk, n_groups), const2),
            pl.BlockSpec((1, w), const2),
            pl.BlockSpec((1, w), const2),
            pl.BlockSpec((w, d), const2, **single),
            pl.BlockSpec((1, d), const2),
            pl.BlockSpec((LANES, d), const2),
            pl.BlockSpec((LANES, 1), const2),
        ],
        out_specs=[
            pl.BlockSpec((1, ts, d), cur3),
            pl.BlockSpec((1, ts, d), prev3),
            pl.BlockSpec((1, ts, LANES), prev3),
            pl.BlockSpec((1, 8, ts), prev3),
            pl.BlockSpec((1, 8, LANES), prev3),
            pl.BlockSpec((1, 8, LANES), cur3),
        ],
        out_shape=[
            jax.ShapeDtypeStruct((nt, ts, d), _F32),
            jax.ShapeDtypeStruct((nt, ts, d), _BF16),
            jax.ShapeDtypeStruct((nt, ts, LANES), _F32),
            jax.ShapeDtypeStruct((nt, 8, ts), _F32),
            jax.ShapeDtypeStruct((nt, 8, LANES), jnp.int32),
            jax.ShapeDtypeStruct((nt, 8, LANES), _F32),
        ],
        scratch_shapes=[
            pltpu.VMEM((n_heads // 2, 2 * HGRN_HEAD_DIM, 2 * HGRN_HEAD_DIM), _F32),
            pltpu.VMEM((ts, w), _F32),
            pltpu.VMEM((ts, w), _F32),
            pltpu.VMEM((ts, d), _F32),
            pltpu.VMEM((N_IN_SLICES, ts, w), _F32),
            pltpu.VMEM((ts, w), _BF16),
            pltpu.VMEM((ts, w), _BF16),
        ],
        compiler_params=pltpu.CompilerParams(
            dimension_semantics=("arbitrary",),
            vmem_limit_bytes=VMEM_LIMIT_BYTES),
        name="mixer",
    )(x.reshape(nt, ts, d), ada, ada, row(g_pre), row(g_post), w_in.astype(_BF16), row(ln_g),
      row(ln_b), w_spatial.astype(_F32), b_spatial.T.astype(_F32), row(lb), row(g_hgrn),
      w_out.astype(_BF16), row(g_pre_ffn), wrt, brt)
    return x1.reshape(b, s, d), h2, meta, metat, cnt, span


def _first_max(vals):
    m = functools.reduce(jnp.maximum, vals)
    idx = jnp.full(m.shape, len(vals) - 1, jnp.int32)
    for j in range(len(vals) - 2, -1, -1):
        idx = jnp.where(vals[j] == m, j, idx)
    return m, idx


def _route(lt, n_groups, per_group):
    lg = [lt[g:g + 1] for g in range(n_groups)]
    mg, g_idx = _first_max(lg)
    p_top = 1.0 / functools.reduce(lambda a, b: a + b, [jnp.exp(l - mg) for l in lg])
    le = []
    for j in range(per_group):
        v = lt[n_groups + j:n_groups + j + 1]
        for g in range(1, n_groups):
            r0 = n_groups + g * per_group + j
            v = jnp.where(g_idx == g, lt[r0:r0 + 1], v)
        le.append(v)
    m1, j1 = _first_max(le)
    m2, j2 = _first_max([jnp.where(j1 == j, -jnp.inf, v) for j, v in enumerate(le)])
    r = jnp.exp(m2 - m1)
    w1 = p_top / (1.0 + r)
    w2 = w1 * r
    return g_idx * per_group + j1, g_idx * per_group + j2, w1, w2


def _route_tile(x, sh2, sc2, gpre, wrt, brt, h_ref, meta_ref, metat_ref, cnt_ref, n_groups, per_group):
    tm = x.shape[0]
    h = (_rms(x) * gpre) * (1.0 + sc2) + sh2
    hb = h.astype(_BF16)
    h_ref[0] = hb
    lt = _dot_nt(wrt, hb) + brt
    e1, e2, w1, w2 = _route(lt, n_groups, per_group)

    sub = lax.broadcasted_iota(jnp.int32, (LANES, tm), 0)
    ind = jnp.where((sub == e1) | (sub == e2), 1.0, 0.0)
    earlier = (lax.broadcasted_iota(jnp.int32, (tm, tm), 0)
               < lax.broadcasted_iota(jnp.int32, (tm, tm), 1))
    rank = _dot(ind.astype(_BF16), jnp.where(earlier, 1.0, 0.0).astype(_BF16))
    cnt = jnp.sum(ind, axis=1, keepdims=True)
    chunks = jnp.floor((cnt + (CHUNK_ROWS - 1)) * (1.0 / CHUNK_ROWS))
    below = (lax.broadcasted_iota(jnp.int32, (LANES, LANES), 1)
             < lax.broadcasted_iota(jnp.int32, (LANES, LANES), 0))
    first_chunk = _dot(jnp.where(below, 1.0, 0.0).astype(_BF16),
                       jnp.broadcast_to(chunks, (LANES, LANES)).astype(_BF16))
    dest = first_chunk[:, 0:1] * CHUNK_ROWS + rank
    n_rows = -(-n_groups * per_group // 8) * 8
    sub_e = lax.broadcasted_iota(jnp.int32, (n_rows, tm), 0)
    d1 = jnp.sum(jnp.where(sub_e == e1, dest[:n_rows], 0.0), axis=0, keepdims=True)
    d2 = jnp.sum(jnp.where(sub_e == e2, dest[:n_rows], 0.0), axis=0, keepdims=True)
    row8 = lax.broadcasted_iota(jnp.int32, (8, tm), 0)
    metat = jnp.where(row8 == 0, d1, jnp.where(row8 == 1, d2,
                      jnp.where(row8 == 2, w1, jnp.where(row8 == 3, w2, 0.0))))
    metat_ref[0] = metat
    meta_ref[0] = jnp.concatenate([metat, jnp.zeros((LANES - 8, tm), _F32)], axis=0).T
    cnt_ref[0] = jnp.broadcast_to(cnt, (LANES, LANES)).T[0:8].astype(jnp.int32)


def _chunk_loop(n, fn):
    def body(c, carry):
        fn(c)
        return carry
    lax.fori_loop(0, n, body, 0)


def _wait_chunks(n, max_chunks, copy_of_chunks):
    bit = 1
    while bit * 2 <= max_chunks:
        bit *= 2
    while bit >= 1:
        @pl.when((n & bit) != 0)
        def _(bit=bit):
            copy_of_chunks(bit).wait()
        bit //= 2


def _dispatch_kernel(nch_ref, cmap_ref, tbase_ref, ntail_ref, nu_ref, metat_ref, h_ref, xs_ref,
                     loc_ref, zero_ref, sem, zsem, *, max_chunks):
    i = pl.program_id(0)
    n_steps = pl.num_programs(0)
    g = h_ref.shape[0]
    lrows, tm = loc_ref.shape[1], h_ref.shape[1]
    buf = lambda step, j: (step % 2) * g + j

    def chunk_copy(slot_, c, dst_chunk):
        return pltpu.make_async_copy(
            loc_ref.at[slot_, pl.ds(pl.multiple_of(c * CHUNK_ROWS, CHUNK_ROWS), CHUNK_ROWS)],
            xs_ref.at[pl.ds(pl.multiple_of(dst_chunk * CHUNK_ROWS, CHUNK_ROWS), CHUNK_ROWS)],
            sem.at[slot_])

    def zero_copy(dst_chunk):
        return pltpu.make_async_copy(
            zero_ref.at[pl.ds(0, CHUNK_ROWS)],
            xs_ref.at[pl.ds(pl.multiple_of(dst_chunk * CHUNK_ROWS, CHUNK_ROWS), CHUNK_ROWS)],
            zsem.at[0])

    def zero_tile_copy(tile):
        return pltpu.make_async_copy(
            zero_ref,
            xs_ref.at[pl.ds(pl.multiple_of(tile * GMM_ROW_TILE, GMM_ROW_TILE), GMM_ROW_TILE)],
            zsem.at[1])

    @pl.when(i == 0)
    def _():
        zero_ref[...] = jnp.zeros(zero_ref.shape, zero_ref.dtype)
        n_unused = xs_ref.shape[0] // GMM_ROW_TILE - nu_ref[0]
        for e in range(tbase_ref.shape[0]):
            _chunk_loop(ntail_ref[e], lambda j, e=e: zero_copy(tbase_ref[e] + j).start())
        _chunk_loop(n_unused, lambda j: zero_tile_copy(nu_ref[0] + j).start())
        for e in range(tbase_ref.shape[0]):
            _chunk_loop(ntail_ref[e], lambda j: zero_copy(0).wait())
        _chunk_loop(n_unused, lambda j: zero_tile_copy(0).wait())

    def wait_slot(slot_, n_):
        _wait_chunks(n_, max_chunks, lambda k: pltpu.make_async_copy(
            loc_ref.at[slot_, pl.ds(0, k * CHUNK_ROWS)], xs_ref.at[pl.ds(0, k * CHUNK_ROWS)],
            sem.at[slot_]))

    @pl.when(i >= 2)
    def _():
        for j in range(g):
            wait_slot(buf(i, j), nch_ref[(i - 2) * g + j])

    r = lax.broadcasted_iota(jnp.int32, (lrows, tm), 0)
    lane = lax.broadcasted_iota(jnp.int32, (lrows, LANES), 1)
    d = h_ref.shape[2]
    for j in range(g):
        mt = metat_ref[j]
        is1 = r == mt[0:1].astype(jnp.int32)
        is2 = r == mt[1:2].astype(jnp.int32)
        sel = jnp.where(is1 | is2, 1.0, 0.0).astype(_BF16)
        loc_ref[buf(i, j), :, 0:d] = _dot(sel, h_ref[j]).astype(_BF16)
        ws = jnp.sum(jnp.where(is1, mt[2:3], 0.0) + jnp.where(is2, mt[3:4], 0.0),
                     axis=1, keepdims=True)
        hi, mid, lo = (p.astype(_F32) for p in _split3(ws))
        loc_ref[buf(i, j), :, d:d + LANES] = jnp.where(
            lane == 0, hi, jnp.where(lane == 1, mid, jnp.where(lane == 2, lo, 0.0))).astype(_BF16)
    for j in range(g):
        tile = i * g + j
        _chunk_loop(nch_ref[tile], lambda c, j=j, tile=tile: chunk_copy(
            buf(i, j), c, cmap_ref[tile * max_chunks + c]).start())

    @pl.when(i == n_steps - 1)
    def _():
        for j in range(g):
            wait_slot(buf(i, j), nch_ref[i * g + j])

        @pl.when(i >= 1)
        def _():
            for j in range(g):
                wait_slot(buf(i - 1, j), nch_ref[(i - 1) * g + j])


def _experts_kernel(te_ref, tf_ref, nu_ref, xs_ref, w1_ref, w3_ref, w2_ref, ys_ref,
                    wb1_ref, wb3_ref, wb2_ref):
    i = pl.program_id(0)

    @pl.when(i < nu_ref[0])
    def _():
        @pl.when(tf_ref[i] == 1)
        def _():
            wb1_ref[...] = w1_ref[0].astype(_BF16)
            wb3_ref[...] = w3_ref[0].astype(_BF16)
            wb2_ref[...] = w2_ref[0].astype(_BF16)

        d = wb1_ref.shape[0]
        xs = xs_ref[:, 0:d]
        half = wb1_ref.shape[1] // 2
        acc = None
        for hcols in (slice(0, half), slice(half, 2 * half)):
            a = _silu(_dot(xs, wb1_ref[:, hcols])) * _dot(xs, wb3_ref[:, hcols])
            part = _dot(a.astype(_BF16), wb2_ref[hcols, :])
            acc = part if acc is None else acc + part
        wp = xs_ref[:, d:d + LANES].astype(_F32)
        ys_ref[...] = (acc * (wp[:, 0:1] + wp[:, 1:2] + wp[:, 2:3])).astype(_BF16)

    @pl.when(i >= nu_ref[0])
    def _():
        ys_ref[...] = jnp.zeros(ys_ref.shape, ys_ref.dtype)


def _combine_kernel(nch_ref, cmap_ref, x_ref, ada_ref, gpost_ref, meta_ref, ys_ref, o_ref,
                    loc_ref, sem, *, max_chunks):
    i = pl.program_id(0)
    n_steps = pl.num_programs(0)
    g = x_ref.shape[0]
    lrows, tm = loc_ref.shape[1], x_ref.shape[1]
    buf = lambda step, j: (step % 2) * g + j

    def chunk_copy(tile, slot_, c):
        src_chunk = cmap_ref[tile * max_chunks + c]
        return pltpu.make_async_copy(
            ys_ref.at[pl.ds(pl.multiple_of(src_chunk * CHUNK_ROWS, CHUNK_ROWS), CHUNK_ROWS)],
            loc_ref.at[slot_, pl.ds(pl.multiple_of(c * CHUNK_ROWS, CHUNK_ROWS), CHUNK_ROWS)],
            sem.at[slot_])

    def fetch(step):
        for j in range(g):
            tile = step * g + j
            _chunk_loop(nch_ref[tile], lambda c, j=j, tile=tile: chunk_copy(tile, buf(step, j), c).start())

    @pl.when(i == 0)
    def _():
        loc_ref[...] = jnp.zeros(loc_ref.shape, loc_ref.dtype)
        fetch(0)

    @pl.when(i + 1 < n_steps)
    def _():
        fetch(i + 1)

    for j in range(g):
        _wait_chunks(nch_ref[i * g + j], max_chunks, lambda k, j=j: pltpu.make_async_copy(
            ys_ref.at[pl.ds(0, k * CHUNK_ROWS)], loc_ref.at[buf(i, j), pl.ds(0, k * CHUNK_ROWS)],
            sem.at[buf(i, j)]))

    r = lax.broadcasted_iota(jnp.int32, (tm, lrows), 1)
    gt2 = ada_ref[0][5:6]
    for j in range(g):
        meta = meta_ref[j]
        d1 = meta[:, 0:1].astype(jnp.int32)
        d2 = meta[:, 1:2].astype(jnp.int32)
        sel = jnp.where((r == d1) | (r == d2), 1.0, 0.0).astype(_BF16)
        y = _dot(sel, loc_ref[buf(i, j)])
        o_ref[j] = x_ref[j] + gt2 * (_rms(y) * gpost_ref[...])


def _moe_plan(cnt, max_chunks, n_gmm_tiles):
    per_tile = GMM_ROW_TILE // CHUNK_ROWS
    pc = (cnt + (CHUNK_ROWS - 1)) // CHUNK_ROWS
    local_first = jnp.cumsum(pc, axis=1) - pc
    nch = jnp.sum(pc, axis=1)
    e_chunks = jnp.sum(pc, axis=0)
    e_region = ((e_chunks + per_tile - 1) // per_tile) * per_tile
    e_first = jnp.cumsum(e_region) - e_region
    seg_first = e_first[None, :] + jnp.cumsum(pc, axis=0) - pc
    c = jnp.arange(max_chunks, dtype=jnp.int32)[None, :, None]
    inside = (c >= local_first[:, None, :]) & (c < (local_first + pc)[:, None, :])
    cmap = jnp.sum(jnp.where(inside, seg_first[:, None, :] + c - local_first[:, None, :], 0), axis=-1)
    n_used = jnp.sum(e_region) // per_tile
    t = jnp.arange(n_gmm_tiles, dtype=jnp.int32)
    t_used = jnp.minimum(t, n_used - 1)
    tile_e = jnp.sum(t_used[:, None] * per_tile >= (e_first + e_region)[None, :], axis=1)
    tile_first = jnp.concatenate([jnp.ones((1,), jnp.int32),
                                  (tile_e[1:] != tile_e[:-1]).astype(jnp.int32)])
    i32 = lambda a: a.astype(jnp.int32)
    return dict(nch=i32(nch), cmap=i32(cmap.reshape(-1)), tail_first=i32(e_first + e_chunks),
                n_tail=i32(e_region - e_chunks), tile_e=i32(tile_e), tile_first=tile_first,
                n_used=i32(n_used.reshape(1)))


def _moe(x, h2, meta, metat, cnt, ada, g_post, n_groups, w1, w3, w2):
    b, s, d = x.shape
    n_exp, _, ff = w1.shape
    tm = h2.shape[1]
    assert s % tm == 0 and tm % LANES == 0 and GMM_ROW_TILE % CHUNK_ROWS == 0
    tiles_per_batch = s // tm
    nt = b * tiles_per_batch
    max_chunks = (TOP_K_IN_GROUP * tm + n_exp * (CHUNK_ROWS - 1)) // CHUNK_ROWS
    lrows = -(-max_chunks * CHUNK_ROWS // LANES) * LANES
    per_tile = GMM_ROW_TILE // CHUNK_ROWS
    n_gmm_tiles = -(-(nt * max_chunks + n_exp * (per_tile - 1)) // per_tile)
    n_rows = n_gmm_tiles * GMM_ROW_TILE
    dx = d + LANES

    row = lambda a: a.reshape(1, -1).astype(_F32)
    n_ada = ada.shape[1]
    xt = x.reshape(nt, tm, d)
    g = MOE_TILES_PER_STEP if tiles_per_batch % MOE_TILES_PER_STEP == 0 else 1
    steps_per_batch = tiles_per_batch // g

    plan = _moe_plan(cnt[:, 0, :n_exp], max_chunks, n_gmm_tiles)

    xs = pl.pallas_call(
        functools.partial(_dispatch_kernel, max_chunks=max_chunks),
        grid_spec=pltpu.PrefetchScalarGridSpec(
            num_scalar_prefetch=5,
            grid=(nt // g,),
            in_specs=[
                pl.BlockSpec((g, 8, tm), lambda i, *_: (i, 0, 0)),
                pl.BlockSpec((g, tm, d), lambda i, *_: (i, 0, 0)),
            ],
            out_specs=pl.BlockSpec(memory_space=pl.ANY),
            scratch_shapes=[
                pltpu.VMEM((2 * g, lrows, dx), _BF16),
                pltpu.VMEM((GMM_ROW_TILE, dx), _BF16),
                pltpu.SemaphoreType.DMA((2 * g,)),
                pltpu.SemaphoreType.DMA((2,)),
            ],
        ),
        out_shape=jax.ShapeDtypeStruct((n_rows, dx), _BF16),
        compiler_params=pltpu.CompilerParams(dimension_semantics=("arbitrary",)),
        name="dispatch",
    )(plan["nch"], plan["cmap"], plan["tail_first"], plan["n_tail"], plan["n_used"], metat, h2)

    ys = pl.pallas_call(
        _experts_kernel,
        grid_spec=pltpu.PrefetchScalarGridSpec(
            num_scalar_prefetch=3,
            grid=(n_gmm_tiles,),
            in_specs=[
                pl.BlockSpec((GMM_ROW_TILE, dx), lambda i, te, tf, nu: (jnp.minimum(i, nu[0] - 1), 0)),
                pl.BlockSpec((1, d, ff), lambda i, te, tf, nu: (te[i], 0, 0)),
                pl.BlockSpec((1, d, ff), lambda i, te, tf, nu: (te[i], 0, 0)),
                pl.BlockSpec((1, ff, d), lambda i, te, tf, nu: (te[i], 0, 0)),
            ],
            out_specs=pl.BlockSpec((GMM_ROW_TILE, d), lambda i, te, tf, nu: (i, 0)),
            scratch_shapes=[
                pltpu.VMEM((d, ff), _BF16),
                pltpu.VMEM((d, ff), _BF16),
                pltpu.VMEM((ff, d), _BF16),
            ],
        ),
        out_shape=jax.ShapeDtypeStruct((n_rows, d), _BF16),
        compiler_params=pltpu.CompilerParams(dimension_semantics=("arbitrary",),
                                             vmem_limit_bytes=VMEM_LIMIT_BYTES),
        name="experts",
    )(plan["tile_e"], plan["tile_first"], plan["n_used"], xs, w1, w3, w2)

    out = pl.pallas_call(
        functools.partial(_combine_kernel, max_chunks=max_chunks),
        grid_spec=pltpu.PrefetchScalarGridSpec(
            num_scalar_prefetch=2,
            grid=(nt // g,),
            in_specs=[
                pl.BlockSpec((g, tm, d), lambda i, *_: (i, 0, 0)),
                pl.BlockSpec((1, n_ada, d), lambda i, *_: (i // steps_per_batch, 0, 0)),
                pl.BlockSpec((1, d), lambda i, *_: (0, 0)),
                pl.BlockSpec((g, tm, LANES), lambda i, *_: (i, 0, 0)),
                pl.BlockSpec(memory_space=pl.ANY),
            ],
            out_specs=pl.BlockSpec((g, tm, d), lambda i, *_: (i, 0, 0)),
            scratch_shapes=[
                pltpu.VMEM((2 * g, lrows, d), _BF16),
                pltpu.SemaphoreType.DMA((2 * g,)),
            ],
        ),
        out_shape=jax.ShapeDtypeStruct((nt, tm, d), _F32),
        compiler_params=pltpu.CompilerParams(dimension_semantics=("arbitrary",)),
        name="combine",
    )(plan["nch"], plan["cmap"], xt, ada, row(g_post), meta, ys)
    return out.reshape(b, s, d)


def kernel(x, c, w_ada, b_ada, g_pre_mix, g_post_mix, w_in, ln_v_g, ln_v_b, w_spatial, b_spatial,
           lb_logits, g_hgrn_norm, w_out, g_pre_ffn, g_post_ffn, w_router_group, b_router_group,
           w_router_expert, b_router_expert, w1, w3, w2):
    depth = w_in.shape[0]
    b, s, d = x.shape
    lb_all = _lower_bounds(lb_logits)
    for layer in range(depth):
        ada = _ada(c, w_ada[layer], b_ada[layer]).reshape(b, 6, d)
        mix = functools.partial(
            _mixer, x, ada, g_pre_mix[layer], g_post_mix[layer], w_in[layer], ln_v_g[layer],
            ln_v_b[layer], w_spatial[layer], b_spatial[layer], lb_all[layer], g_hgrn_norm[layer],
            w_out[layer], g_pre_ffn[layer], w_router_group[layer], b_router_group[layer],
            w_router_expert[layer], b_router_expert[layer])
        *mixed, span = mix(stepwise=False)
        x, h2, meta, metat, cnt = lax.cond(
            jnp.max(span) > HGRN_MAX_SPAN,
            lambda: tuple(mix(stepwise=True)[:5]),
            lambda: tuple(mixed))
        x = _moe(x, h2, meta, metat, cnt, ada, g_post_ffn[layer], w_router_group.shape[-1],
                 w1[layer], w3[layer], w2[layer])
    return x
```

```python
import functools

import jax
import jax.numpy as jnp
from jax import lax
from jax.experimental import pallas as pl
from jax.experimental.pallas import tpu as pltpu

RMS_EPS = 1e-6
LN_EPS = 1e-5
HGRN_HEAD_DIM = 128
HGRN_CHUNK = 128
HGRN_MAX_SPAN = 60.0
N_IN_SLICES = 8
TOP_K_IN_GROUP = 2
LANES = 128
MIXER_SEQ_TILE = 256
PROJ_COLS = 256
CHUNK_ROWS = 16
GMM_ROW_TILE = 1024
MOE_TILES_PER_STEP = 4
VMEM_LIMIT_BYTES = 56 * 1024 * 1024

_F32 = jnp.float32
_BF16 = jnp.bfloat16


def _sigmoid(x):
    return 0.5 * (jnp.tanh(0.5 * x) + 1.0)


def _silu(x):
    return x * _sigmoid(x)


def _gelu_tanh(x):
    c = 0.7978845608028654
    return 0.5 * x * (1.0 + jnp.tanh(c * (x + 0.044715 * (x * x * x))))


def _rms(x):
    return x * lax.rsqrt(jnp.mean(x * x, axis=-1, keepdims=True) + RMS_EPS)


def _dot(a, b):
    return jnp.dot(a, b, preferred_element_type=_F32)


def _dot_nt(a, b):
    return lax.dot_general(a, b, (((1,), (1,)), ((), ())), preferred_element_type=_F32)


def _dot_tn(a, b):
    return lax.dot_general(a, b, (((0,), (0,)), ((), ())), preferred_element_type=_F32)


def _split3(x):
    hi = x.astype(_BF16)
    r = x - hi.astype(_F32)
    mid = r.astype(_BF16)
    lo = (r - mid.astype(_F32)).astype(_BF16)
    return hi, mid, lo


def _ada_kernel(c_ref, w_ref, b_ref, o_ref):
    s = _silu(c_ref[...])
    w = w_ref[...]
    acc = jnp.zeros(o_ref.shape, _F32)
    for sp in _split3(s):
        for wp in _split3(w)[:2]:
            acc = acc + _dot(sp, wp)
    o_ref[...] = acc + b_ref[...]


def _ada(c, w_ada, b_ada):
    b, d = c.shape
    n = w_ada.shape[1]
    tn = d
    return pl.pallas_call(
        _ada_kernel,
        grid=(n // tn,),
        in_specs=[
            pl.BlockSpec((b, d), lambda j: (0, 0)),
            pl.BlockSpec((d, tn), lambda j: (0, j)),
            pl.BlockSpec((1, tn), lambda j: (0, j)),
        ],
        out_specs=pl.BlockSpec((b, tn), lambda j: (0, j)),
        out_shape=jax.ShapeDtypeStruct((b, n), _F32),
        name="ada",
    )(c, w_ada, b_ada.reshape(1, n))


def _lb_kernel(l_ref, o_ref):
    l = l_ref[...]
    e = jnp.exp(l - jnp.max(l, axis=0, keepdims=True))
    p = e / jnp.sum(e, axis=0, keepdims=True)
    rows = [p[0:1]]
    for i in range(1, p.shape[0]):
        rows.append(rows[-1] + p[i:i + 1])
    for i, r in enumerate(rows):
        o_ref[i:i + 1, :] = r


def _lower_bounds(lb_logits):
    return pl.pallas_call(
        _lb_kernel,
        out_shape=jax.ShapeDtypeStruct(lb_logits.shape, _F32),
        name="lower_bounds",
    )(lb_logits.astype(_F32))


def _mixer_kernel(x_ref, ada_ref, adap_ref, gpre_ref, gpost_ref, win_ref, lng_ref, lnb_ref, wsp_ref,
                  bspt_ref, lb_ref, ghg_ref, wout_ref, gpre2_ref, wrt_ref, brt_ref,
                  o_ref, h2_ref, meta_ref, metat_ref, cnt_ref, span_ref, st_ref, ya_ref, ob_ref,
                  x1p_ref, act_ref, iv_ref, vn_ref,
                  *, n_route_groups, per_group, tiles_per_batch, stepwise):
    ts, d = x_ref.shape[1], x_ref.shape[2]
    w = wout_ref.shape[0]
    n_groups, gchunk = wsp_ref.shape[0], wsp_ref.shape[1]
    gdim = w // n_groups
    t = pl.program_id(0)
    is_tile = t < pl.num_programs(0) - 1

    @pl.when(t % tiles_per_batch == 0)
    def _():
        st_ref[...] = jnp.zeros(st_ref.shape, _F32)

    @pl.when(t == 0)
    def _():
        x1p_ref[...] = jnp.zeros(x1p_ref.shape, _F32)

    x1_prev = x1p_ref[...]
    adap = adap_ref[0]
    _route_tile(x1_prev, adap[3:4], adap[4:5], gpre2_ref[...], wrt_ref[...], brt_ref[...],
                h2_ref, meta_ref, metat_ref, cnt_ref, n_route_groups, per_group)

    x = x_ref[0]
    ada = ada_ref[0]
    sh1, sc1, gt1 = ada[0:1], ada[1:2], ada[2:3]
    h = (_rms(x) * gpre_ref[...]) * (1.0 + sc1) + sh1
    h_bf = h.astype(_BF16)

    lb = lb_ref[...]
    (s_v, s_u, s_q, s_f, s_i, s_og, s_ga, s_gb) = (1, 0, 2, 3, 4, 5, 6, 7)
    (a_u, a_v, a_q, a_lf, a_k, a_og, a_ga, a_gb) = range(8)

    def activate(j, cols, r):
        if j == s_u:
            act_ref[a_u, :, cols] = _gelu_tanh(r)
        elif j == s_v:
            act_ref[a_v, :, cols] = _gelu_tanh(r)
        elif j == s_q:
            act_ref[a_q, :, cols] = _silu(r)
        elif j == s_f:
            f = lb[:, cols] + (1.0 - lb[:, cols]) * _sigmoid(r)
            act_ref[a_lf, :, cols] = jnp.log(f)
            act_ref[a_k, :, cols] = 1.0 - f
        elif j == s_i:
            iv_ref[:, cols] = r.astype(_BF16)
        elif j == s_og:
            act_ref[a_og, :, cols] = _silu(r)
        elif j == s_ga:
            act_ref[a_ga, :, cols] = _sigmoid(r)
        else:
            act_ref[a_gb, :, cols] = _sigmoid(r)

    def layer_norm_v():
        v = act_ref[a_v]
        mu = jnp.mean(v, axis=-1, keepdims=True)
        vc = v - mu
        var = jnp.mean(vc * vc, axis=-1, keepdims=True)
        vn_ref[...] = ((vc * lax.rsqrt(var + LN_EPS)) * lng_ref[...] + lnb_ref[...]).astype(_BF16)

    order = (s_v, s_u, s_q, s_f, s_i, s_og, s_ga, s_gb)
    pending = None
    for pos, j in enumerate(order):
        for cb in range(w // PROJ_COLS):
            cols = slice(cb * PROJ_COLS, (cb + 1) * PROJ_COLS)
            r = _dot(h_bf, win_ref[:, j * w + cb * PROJ_COLS:j * w + (cb + 1) * PROJ_COLS])
            if pending is not None:
                pending()
            pending = functools.partial(activate, j, cols, r)
            if pos == 1 and cb == 0:
                layer_norm_v()
    pending()

    u = act_ref[a_u]
    vn = vn_ref[...]
    tri_g = (lax.broadcasted_iota(jnp.int32, (gchunk, gchunk), 0)
             >= lax.broadcasted_iota(jnp.int32, (gchunk, gchunk), 1))
    bspt = bspt_ref[...]
    n_gchunks = ts // gchunk
    for g in range(n_groups):
        wc = jnp.where(tri_g, wsp_ref[g], 0.0).astype(_BF16)
        cols = slice(g * gdim, (g + 1) * gdim)
        vg = jnp.concatenate([vn[n * gchunk:(n + 1) * gchunk, cols] for n in range(n_gchunks)], axis=1)
        zv = _dot(wc, vg)
        for n in range(n_gchunks):
            rows = slice(n * gchunk, (n + 1) * gchunk)
            ya_ref[rows, cols] = u[rows, cols] * (zv[:, n * gdim:(n + 1) * gdim] + bspt[:, g:g + 1])

    pair = 2 * HGRN_HEAD_DIM
    pr = lax.broadcasted_iota(jnp.int32, (pair, pair), 0)
    pc_i = lax.broadcasted_iota(jnp.int32, (pair, pair), 1)
    same_head = (pr < HGRN_HEAD_DIM) == (pc_i < HGRN_HEAD_DIM)
    if stepwise:
        _hgrn_stepwise(act_ref.at[a_q], act_ref.at[a_k], iv_ref, st_ref, ob_ref, same_head)
        span_ref[0] = jnp.zeros(span_ref.shape[1:], _F32)
    else:
        _hgrn_chunkwise(act_ref.at[a_q], act_ref.at[a_lf], act_ref.at[a_k], iv_ref, st_ref, ob_ref,
                        span_ref, same_head)

    yb = (ob_ref[...] * ghg_ref[...]) * act_ref[a_og]
    y = act_ref[a_ga] * ya_ref[...] + act_ref[a_gb] * yb
    out = _dot(y.astype(_BF16), wout_ref[...])
    x1 = jnp.where(is_tile, x + gt1 * (_rms(out) * gpost_ref[...]), x1_prev)
    o_ref[0] = x1
    x1p_ref[...] = x1


def _hgrn_stepwise(q_ref, k_ref, iv_ref, st_ref, ob_ref, same_head):
    ts, w = q_ref.shape
    pair = 2 * HGRN_HEAD_DIM
    sub = 16

    def group(gi, carry):
        r0 = pl.multiple_of(gi * sub, sub)
        q_g = q_ref[pl.ds(r0, sub), :]
        k_g = k_ref[pl.ds(r0, sub), :]
        v_g = iv_ref[pl.ds(r0, sub), :].astype(_F32)
        o_rows = []
        for i in range(sub):
            q_t, k_t, v_t = q_g[i:i + 1], k_g[i:i + 1], v_g[i:i + 1]
            o_t = []
            for p in range(w // pair):
                pcols = slice(p * pair, (p + 1) * pair)
                upd = _dot_tn(v_t[:, pcols].astype(_BF16), k_t[:, pcols].astype(_BF16))
                st = st_ref[p] * (1.0 - k_t[:, pcols]) + jnp.where(same_head, upd, 0.0)
                st_ref[p] = st
                o_p = _dot_nt(q_t[:, pcols].astype(_BF16), st.astype(_BF16))
                o_t += [_rms(o_p[:, :HGRN_HEAD_DIM]), _rms(o_p[:, HGRN_HEAD_DIM:])]
            o_rows.append(jnp.concatenate(o_t, axis=1))
        ob_ref[pl.ds(r0, sub), :] = jnp.concatenate(o_rows, axis=0)
        return carry

    lax.fori_loop(0, ts // sub, group, 0)


def _hgrn_chunkwise(q_ref, lf_ref, k_ref, iv_ref, st_ref, ob_ref, span_ref, same_head):
    ts, w = q_ref.shape
    n_heads = w // HGRN_HEAD_DIM
    c = HGRN_CHUNK
    q = q_ref[...]
    lf = lf_ref[...]
    k = k_ref[...]
    iv = iv_ref[...]
    r_i = lax.broadcasted_iota(jnp.int32, (ts, ts), 0)
    c_i = lax.broadcasted_iota(jnp.int32, (ts, ts), 1)
    ltri = jnp.where((r_i // c == c_i // c) & (c_i <= r_i), 1.0, 0.0).astype(_BF16)
    bcum = jnp.zeros((ts, w), _F32)
    for part in _split3(lf)[:2]:
        bcum = bcum + _dot(ltri, part)
    hb = c // 2
    row_h = lax.broadcasted_iota(jnp.int32, (hb, c), 0)
    lane_h = lax.broadcasted_iota(jnp.int32, (hb, c), 1)
    top_mask = lane_h <= row_h
    left = lane_h < hb
    bot_mask = lane_h - hb <= row_h
    pair = 2 * HGRN_HEAD_DIM
    up_rows = lax.broadcasted_iota(jnp.int32, (c, pair), 1) < HGRN_HEAD_DIM
    span = jnp.zeros((1, 1), _F32)
    for n in range(ts // c):
        rows = slice(n * c, (n + 1) * c)
        bc = bcum[rows]
        b_a, b_m, b_b, b_l = (bc[hb // 2 - 1:hb // 2], bc[hb - 1:hb],
                              bc[hb + hb // 2 - 1:hb + hb // 2], bc[c - 1:c])
        stack = lambda ra, rb: jnp.concatenate(
            [jnp.broadcast_to(ra, (hb, w)), jnp.broadcast_to(rb, (hb, w))], axis=0)
        ref = stack(b_a, b_b)
        ends = jnp.maximum(jnp.maximum(bc[0:1] - b_a, b_a - b_m),
                           jnp.maximum(bc[hb:hb + 1] - b_b, b_b - b_l))
        span = jnp.maximum(span, jnp.max(ends, axis=1, keepdims=True))
        qx = q[rows] * jnp.exp(bc - ref)
        kx = k[rows] * jnp.exp(ref - bc)
        qd = (qx * stack(jnp.exp(b_a), jnp.exp(b_b))).astype(_BF16)
        kd = (kx * stack(jnp.exp(b_l - b_a), jnp.exp(b_l - b_b))).astype(_BF16)
        q_off = qx[hb:] * jnp.exp(b_b - b_a)
        q3 = jnp.concatenate([qx, q_off], axis=0).astype(_BF16)
        k3 = kx.astype(_BF16)
        dec = jnp.exp(b_l)
        iv_c = iv[rows]
        for p in range(n_heads // 2):
            pcols = slice(p * pair, (p + 1) * pair)
            a_heads = []
            for hd in (2 * p, 2 * p + 1):
                cols = slice(hd * HGRN_HEAD_DIM, (hd + 1) * HGRN_HEAD_DIM)
                sc = _dot_nt(q3[:, cols], k3[:, cols])
                top = jnp.where(top_mask, sc[0:hb], 0.0)
                bot = jnp.where(left, sc[2 * hb:3 * hb], jnp.where(bot_mask, sc[hb:2 * hb], 0.0))
                a_heads.append(jnp.concatenate([top, bot], axis=0))
            a_pair = jnp.concatenate(a_heads, axis=1).astype(_BF16)
            iv_p = iv_c[:, pcols]
            iv_blk = jnp.concatenate([jnp.where(up_rows, iv_p, 0.0).astype(_BF16),
                                      jnp.where(up_rows, 0.0, iv_p).astype(_BF16)], axis=0)
            st = st_ref[p]
            o = _dot(a_pair, iv_blk) + _dot_nt(qd[:, pcols], st.astype(_BF16))
            upd = _dot_tn(iv_p, kd[:, pcols])
            st_ref[p] = st * dec[:, pcols] + jnp.where(same_head, upd, 0.0)
            for j in range(2):
                hcols = slice(j * HGRN_HEAD_DIM, (j + 1) * HGRN_HEAD_DIM)
                ob_ref[rows, p * pair + j * HGRN_HEAD_DIM:p * pair + (j + 1) * HGRN_HEAD_DIM] = _rms(o[:, hcols])
    span_ref[0] = jnp.broadcast_to(span, span_ref.shape[1:])


def _mixer(x, ada, g_pre, g_post, w_in, ln_g, ln_b, w_spatial, b_spatial, lb, g_hgrn, w_out,
           g_pre_ffn, w_rg, b_rg, w_re, b_re, stepwise):
    b, s, d = x.shape
    w = w_out.shape[0]
    n_groups, gchunk, _ = w_spatial.shape
    ts = min(MIXER_SEQ_TILE, s)
    assert s % ts == 0 and ts % gchunk == 0 and ts % HGRN_CHUNK == 0 and ts % LANES == 0
    n_rg, n_exp = w_rg.shape[1], w_re.shape[1]
    assert n_rg + n_exp <= LANES
    pad = LANES - n_rg - n_exp
    wrt = jnp.concatenate([w_rg, w_re, jnp.zeros((d, pad), _F32)], axis=1).T.astype(_BF16)
    brt = jnp.concatenate([b_rg, b_re, jnp.zeros((pad,), _F32)]).reshape(LANES, 1).astype(_F32)
    tiles = s // ts
    nt = b * tiles
    cur = lambda t: jnp.minimum(t, nt - 1)
    prev = lambda t: jnp.maximum(t - 1, 0)
    cur3 = lambda t: (cur(t), 0, 0)
    prev3 = lambda t: (prev(t), 0, 0)
    assert w % HGRN_HEAD_DIM == 0 and w // n_groups == LANES and w_in.shape == (d, N_IN_SLICES * w)
    assert w % PROJ_COLS == 0
    n_heads = w // HGRN_HEAD_DIM
    assert n_heads % 2 == 0 and HGRN_HEAD_DIM == LANES
    row = lambda a: a.reshape(1, -1).astype(_F32)
    const2 = lambda t: (0, 0)
    const3 = lambda t: (0, 0, 0)
    single = dict(pipeline_mode=pl.Buffered(1))
    x1, h2, meta, metat, cnt, span = pl.pallas_call(
        functools.partial(_mixer_kernel, n_route_groups=n_rg, per_group=n_exp // n_rg,
                          tiles_per_batch=tiles, stepwise=stepwise),
        grid=(nt + 1,),
        in_specs=[
            pl.BlockSpec((1, ts, d), cur3),
            pl.BlockSpec((1, ada.shape[1], d), lambda t: (cur(t) // tiles, 0, 0)),
            pl.BlockSpec((1, ada.shape[1], d), lambda t: (prev(t) // tiles, 0, 0)),
            pl.BlockSpec((1, d), const2),
            pl.BlockSpec((1, d), const2),
            pl.BlockSpec((d, N_IN_SLICES * w), const2, **single),
            pl.BlockSpec((1, w), const2),
            pl.BlockSpec((1, w), const2),
            pl.BlockSpec((n_groups, gchunk, gchunk), const3, **single),
            pl.BlockSpec((gchunk, n_groups), const2),
            pl.BlockSpec((1, w), const2),
            pl.BlockSpec((1, w), const2),
            pl.BlockSpec((w, d), const2, **single),
            pl.BlockSpec((1, d), const2),
            pl.BlockSpec((LANES, d), const2),
            pl.BlockSpec((LANES, 1), const2),
        ],
        out_specs=[
            pl.BlockSpec((1, ts, d), cur3),
            pl.BlockSpec((1, ts, d), prev3),
            pl.BlockSpec((1, ts, LANES), prev3),
            pl.BlockSpec((1, 8, ts), prev3),
            pl.BlockSpec((1, 8, LANES), prev3),
            pl.BlockSpec((1, 8, LANES), cur3),
        ],
        out_shape=[
            jax.ShapeDtypeStruct((nt, ts, d), _F32),
            jax.ShapeDtypeStruct((nt, ts, d), _BF16),
            jax.ShapeDtypeStruct((nt, ts, LANES), _F32),
            jax.ShapeDtypeStruct((nt, 8, ts), _F32),
            jax.ShapeDtypeStruct((nt, 8, LANES), jnp.int32),
            jax.ShapeDtypeStruct((nt, 8, LANES), _F32),
        ],
        scratch_shapes=[
            pltpu.VMEM((n_heads // 2, 2 * HGRN_HEAD_DIM, 2 * HGRN_HEAD_DIM), _F32),
            pltpu.VMEM((ts, w), _F32),
            pltpu.VMEM((ts, w), _F32),
            pltpu.VMEM((ts, d), _F32),
            pltpu.VMEM((N_IN_SLICES, ts, w), _F32),
            pltpu.VMEM((ts, w), _BF16),
            pltpu.VMEM((ts, w), _BF16),
        ],
        compiler_params=pltpu.CompilerParams(
            dimension_semantics=("arbitrary",),
            vmem_limit_bytes=VMEM_LIMIT_BYTES),
        name="mixer",
    )(x.reshape(nt, ts, d), ada, ada, row(g_pre), row(g_post), w_in.astype(_BF16), row(ln_g),
      row(ln_b), w_spatial.astype(_F32), b_spatial.T.astype(_F32), row(lb), row(g_hgrn),
      w_out.astype(_BF16), row(g_pre_ffn), wrt, brt)
    return x1.reshape(b, s, d), h2, meta, metat, cnt, span


def _first_max(vals):
    m = functools.reduce(jnp.maximum, vals)
    idx = jnp.full(m.shape, len(vals) - 1, jnp.int32)
    for j in range(len(vals) - 2, -1, -1):
        idx = jnp.where(vals[j] == m, j, idx)
    return m, idx


def _route(lt, n_groups, per_group):
    lg = [lt[g:g + 1] for g in range(n_groups)]
    mg, g_idx = _first_max(lg)
    p_top = 1.0 / functools.reduce(lambda a, b: a + b, [jnp.exp(l - mg) for l in lg])
    le = []
    for j in range(per_group):
        v = lt[n_groups + j:n_groups + j + 1]
        for g in range(1, n_groups):
            r0 = n_groups + g * per_group + j
            v = jnp.where(g_idx == g, lt[r0:r0 + 1], v)
        le.append(v)
    m1, j1 = _first_max(le)
    m2, j2 = _first_max([jnp.where(j1 == j, -jnp.inf, v) for j, v in enumerate(le)])
    r = jnp.exp(m2 - m1)
    w1 = p_top / (1.0 + r)
    w2 = w1 * r
    return g_idx * per_group + j1, g_idx * per_group + j2, w1, w2


def _route_tile(x, sh2, sc2, gpre, wrt, brt, h_ref, meta_ref, metat_ref, cnt_ref, n_groups, per_group):
    tm = x.shape[0]
    h = (_rms(x) * gpre) * (1.0 + sc2) + sh2
    hb = h.astype(_BF16)
    h_ref[0] = hb
    lt = _dot_nt(wrt, hb) + brt
    e1, e2, w1, w2 = _route(lt, n_groups, per_group)

    sub = lax.broadcasted_iota(jnp.int32, (LANES, tm), 0)
    ind = jnp.where((sub == e1) | (sub == e2), 1.0, 0.0)
    earlier = (lax.broadcasted_iota(jnp.int32, (tm, tm), 0)
               < lax.broadcasted_iota(jnp.int32, (tm, tm), 1))
    rank = _dot(ind.astype(_BF16), jnp.where(earlier, 1.0, 0.0).astype(_BF16))
    cnt = jnp.sum(ind, axis=1, keepdims=True)
    chunks = jnp.floor((cnt + (CHUNK_ROWS - 1)) * (1.0 / CHUNK_ROWS))
    below = (lax.broadcasted_iota(jnp.int32, (LANES, LANES), 1)
             < lax.broadcasted_iota(jnp.int32, (LANES, LANES), 0))
    first_chunk = _dot(jnp.where(below, 1.0, 0.0).astype(_BF16),
                       jnp.broadcast_to(chunks, (LANES, LANES)).astype(_BF16))
    dest = first_chunk[:, 0:1] * CHUNK_ROWS + rank
    n_rows = -(-n_groups * per_group // 8) * 8
    sub_e = lax.broadcasted_iota(jnp.int32, (n_rows, tm), 0)
    d1 = jnp.sum(jnp.where(sub_e == e1, dest[:n_rows], 0.0), axis=0, keepdims=True)
    d2 = jnp.sum(jnp.where(sub_e == e2, dest[:n_rows], 0.0), axis=0, keepdims=True)
    row8 = lax.broadcasted_iota(jnp.int32, (8, tm), 0)
    metat = jnp.where(row8 == 0, d1, jnp.where(row8 == 1, d2,
                      jnp.where(row8 == 2, w1, jnp.where(row8 == 3, w2, 0.0))))
    metat_ref[0] = metat
    meta_ref[0] = jnp.concatenate([metat, jnp.zeros((LANES - 8, tm), _F32)], axis=0).T
    cnt_ref[0] = jnp.broadcast_to(cnt, (LANES, LANES)).T[0:8].astype(jnp.int32)


def _chunk_loop(n, fn):
    def body(c, carry):
        fn(c)
        return carry
    lax.fori_loop(0, n, body, 0)


def _wait_chunks(n, max_chunks, copy_of_chunks):
    bit = 1
    while bit * 2 <= max_chunks:
        bit *= 2
    while bit >= 1:
        @pl.when((n & bit) != 0)
        def _(bit=bit):
            copy_of_chunks(bit).wait()
        bit //= 2


def _dispatch_kernel(nch_ref, cmap_ref, tbase_ref, ntail_ref, nu_ref, metat_ref, h_ref, xs_ref,
                     loc_ref, zero_ref, sem, zsem, *, max_chunks):
    i = pl.program_id(0)
    n_steps = pl.num_programs(0)
    g = h_ref.shape[0]
    lrows, tm = loc_ref.shape[1], h_ref.shape[1]
    buf = lambda step, j: (step % 2) * g + j

    def chunk_copy(slot_, c, dst_chunk):
        return pltpu.make_async_copy(
            loc_ref.at[slot_, pl.ds(pl.multiple_of(c * CHUNK_ROWS, CHUNK_ROWS), CHUNK_ROWS)],
            xs_ref.at[pl.ds(pl.multiple_of(dst_chunk * CHUNK_ROWS, CHUNK_ROWS), CHUNK_ROWS)],
            sem.at[slot_])

    def zero_copy(dst_chunk):
        return pltpu.make_async_copy(
            zero_ref.at[pl.ds(0, CHUNK_ROWS)],
            xs_ref.at[pl.ds(pl.multiple_of(dst_chunk * CHUNK_ROWS, CHUNK_ROWS), CHUNK_ROWS)],
            zsem.at[0])

    def zero_tile_copy(tile):
        return pltpu.make_async_copy(
            zero_ref,
            xs_ref.at[pl.ds(pl.multiple_of(tile * GMM_ROW_TILE, GMM_ROW_TILE), GMM_ROW_TILE)],
            zsem.at[1])

    @pl.when(i == 0)
    def _():
        zero_ref[...] = jnp.zeros(zero_ref.shape, zero_ref.dtype)
        n_unused = xs_ref.shape[0] // GMM_ROW_TILE - nu_ref[0]
        for e in range(tbase_ref.shape[0]):
            _chunk_loop(ntail_ref[e], lambda j, e=e: zero_copy(tbase_ref[e] + j).start())
        _chunk_loop(n_unused, lambda j: zero_tile_copy(nu_ref[0] + j).start())
        for e in range(tbase_ref.shape[0]):
            _chunk_loop(ntail_ref[e], lambda j: zero_copy(0).wait())
        _chunk_loop(n_unused, lambda j: zero_tile_copy(0).wait())

    def wait_slot(slot_, n_):
        _wait_chunks(n_, max_chunks, lambda k: pltpu.make_async_copy(
            loc_ref.at[slot_, pl.ds(0, k * CHUNK_ROWS)], xs_ref.at[pl.ds(0, k * CHUNK_ROWS)],
            sem.at[slot_]))

    @pl.when(i >= 2)
    def _():
        for j in range(g):
            wait_slot(buf(i, j), nch_ref[(i - 2) * g + j])

    r = lax.broadcasted_iota(jnp.int32, (lrows, tm), 0)
    lane = lax.broadcasted_iota(jnp.int32, (lrows, LANES), 1)
    d = h_ref.shape[2]
    for j in range(g):
        mt = metat_ref[j]
        is1 = r == mt[0:1].astype(jnp.int32)
        is2 = r == mt[1:2].astype(jnp.int32)
        sel = jnp.where(is1 | is2, 1.0, 0.0).astype(_BF16)
        loc_ref[buf(i, j), :, 0:d] = _dot(sel, h_ref[j]).astype(_BF16)
        ws = jnp.sum(jnp.where(is1, mt[2:3], 0.0) + jnp.where(is2, mt[3:4], 0.0),
                     axis=1, keepdims=True)
        hi, mid, lo = (p.astype(_F32) for p in _split3(ws))
        loc_ref[buf(i, j), :, d:d + LANES] = jnp.where(
            lane == 0, hi, jnp.where(lane == 1, mid, jnp.where(lane == 2, lo, 0.0))).astype(_BF16)
    for j in range(g):
        tile = i * g + j
        _chunk_loop(nch_ref[tile], lambda c, j=j, tile=tile: chunk_copy(
            buf(i, j), c, cmap_ref[tile * max_chunks + c]).start())

    @pl.when(i == n_steps - 1)
    def _():
        for j in range(g):
            wait_slot(buf(i, j), nch_ref[i * g + j])

        @pl.when(i >= 1)
        def _():
            for j in range(g):
                wait_slot(buf(i - 1, j), nch_ref[(i - 1) * g + j])


def _experts_kernel(te_ref, tf_ref, nu_ref, xs_ref, w1_ref, w3_ref, w2_ref, ys_ref,
                    wb1_ref, wb3_ref, wb2_ref):
    i = pl.program_id(0)

    @pl.when(i < nu_ref[0])
    def _():
        @pl.when(tf_ref[i] == 1)
        def _():
            wb1_ref[...] = w1_ref[0].astype(_BF16)
            wb3_ref[...] = w3_ref[0].astype(_BF16)
            wb2_ref[...] = w2_ref[0].astype(_BF16)

        d = wb1_ref.shape[0]
        xs = xs_ref[:, 0:d]
        half = wb1_ref.shape[1] // 2
        acc = None
        for hcols in (slice(0, half), slice(half, 2 * half)):
            a = _silu(_dot(xs, wb1_ref[:, hcols])) * _dot(xs, wb3_ref[:, hcols])
            part = _dot(a.astype(_BF16), wb2_ref[hcols, :])
            acc = part if acc is None else acc + part
        wp = xs_ref[:, d:d + LANES].astype(_F32)
        ys_ref[...] = (acc * (wp[:, 0:1] + wp[:, 1:2] + wp[:, 2:3])).astype(_BF16)

    @pl.when(i >= nu_ref[0])
    def _():
        ys_ref[...] = jnp.zeros(ys_ref.shape, ys_ref.dtype)


def _combine_kernel(nch_ref, cmap_ref, x_ref, ada_ref, gpost_ref, meta_ref, ys_ref, o_ref,
                    loc_ref, sem, *, max_chunks):
    i = pl.program_id(0)
    n_steps = pl.num_programs(0)
    g = x_ref.shape[0]
    lrows, tm = loc_ref.shape[1], x_ref.shape[1]
    buf = lambda step, j: (step % 2) * g + j

    def chunk_copy(tile, slot_, c):
        src_chunk = cmap_ref[tile * max_chunks + c]
        return pltpu.make_async_copy(
            ys_ref.at[pl.ds(pl.multiple_of(src_chunk * CHUNK_ROWS, CHUNK_ROWS), CHUNK_ROWS)],
            loc_ref.at[slot_, pl.ds(pl.multiple_of(c * CHUNK_ROWS, CHUNK_ROWS), CHUNK_ROWS)],
            sem.at[slot_])

    def fetch(step):
        for j in range(g):
            tile = step * g + j
            _chunk_loop(nch_ref[tile], lambda c, j=j, tile=tile: chunk_copy(tile, buf(step, j), c).start())

    @pl.when(i == 0)
    def _():
        loc_ref[...] = jnp.zeros(loc_ref.shape, loc_ref.dtype)
        fetch(0)

    @pl.when(i + 1 < n_steps)
    def _():
        fetch(i + 1)

    for j in range(g):
        _wait_chunks(nch_ref[i * g + j], max_chunks, lambda k, j=j: pltpu.make_async_copy(
            ys_ref.at[pl.ds(0, k * CHUNK_ROWS)], loc_ref.at[buf(i, j), pl.ds(0, k * CHUNK_ROWS)],
            sem.at[buf(i, j)]))

    r = lax.broadcasted_iota(jnp.int32, (tm, lrows), 1)
    gt2 = ada_ref[0][5:6]
    for j in range(g):
        meta = meta_ref[j]
        d1 = meta[:, 0:1].astype(jnp.int32)
        d2 = meta[:, 1:2].astype(jnp.int32)
        sel = jnp.where((r == d1) | (r == d2), 1.0, 0.0).astype(_BF16)
        y = _dot(sel, loc_ref[buf(i, j)])
        o_ref[j] = x_ref[j] + gt2 * (_rms(y) * gpost_ref[...])


def _moe_plan(cnt, max_chunks, n_gmm_tiles):
    per_tile = GMM_ROW_TILE // CHUNK_ROWS
    pc = (cnt + (CHUNK_ROWS - 1)) // CHUNK_ROWS
    local_first = jnp.cumsum(pc, axis=1) - pc
    nch = jnp.sum(pc, axis=1)
    e_chunks = jnp.sum(pc, axis=0)
    e_region = ((e_chunks + per_tile - 1) // per_tile) * per_tile
    e_first = jnp.cumsum(e_region) - e_region
    seg_first = e_first[None, :] + jnp.cumsum(pc, axis=0) - pc
    c = jnp.arange(max_chunks, dtype=jnp.int32)[None, :, None]
    inside = (c >= local_first[:, None, :]) & (c < (local_first + pc)[:, None, :])
    cmap = jnp.sum(jnp.where(inside, seg_first[:, None, :] + c - local_first[:, None, :], 0), axis=-1)
    n_used = jnp.sum(e_region) // per_tile
    t = jnp.arange(n_gmm_tiles, dtype=jnp.int32)
    t_used = jnp.minimum(t, n_used - 1)
    tile_e = jnp.sum(t_used[:, None] * per_tile >= (e_first + e_region)[None, :], axis=1)
    tile_first = jnp.concatenate([jnp.ones((1,), jnp.int32),
                                  (tile_e[1:] != tile_e[:-1]).astype(jnp.int32)])
    i32 = lambda a: a.astype(jnp.int32)
    return dict(nch=i32(nch), cmap=i32(cmap.reshape(-1)), tail_first=i32(e_first + e_chunks),
                n_tail=i32(e_region - e_chunks), tile_e=i32(tile_e), tile_first=tile_first,
                n_used=i32(n_used.reshape(1)))


def _moe(x, h2, meta, metat, cnt, ada, g_post, n_groups, w1, w3, w2):
    b, s, d = x.shape
    n_exp, _, ff = w1.shape
    tm = h2.shape[1]
    assert s % tm == 0 and tm % LANES == 0 and GMM_ROW_TILE % CHUNK_ROWS == 0
    tiles_per_batch = s // tm
    nt = b * tiles_per_batch
    max_chunks = (TOP_K_IN_GROUP * tm + n_exp * (CHUNK_ROWS - 1)) // CHUNK_ROWS
    lrows = -(-max_chunks * CHUNK_ROWS // LANES) * LANES
    per_tile = GMM_ROW_TILE // CHUNK_ROWS
    n_gmm_tiles = -(-(nt * max_chunks + n_exp * (per_tile - 1)) // per_tile)
    n_rows = n_gmm_tiles * GMM_ROW_TILE
    dx = d + LANES

    row = lambda a: a.reshape(1, -1).astype(_F32)
    n_ada = ada.shape[1]
    xt = x.reshape(nt, tm, d)
    g = MOE_TILES_PER_STEP if tiles_per_batch % MOE_TILES_PER_STEP == 0 else 1
    steps_per_batch = tiles_per_batch // g

    plan = _moe_plan(cnt[:, 0, :n_exp], max_chunks, n_gmm_tiles)

    xs = pl.pallas_call(
        functools.partial(_dispatch_kernel, max_chunks=max_chunks),
        grid_spec=pltpu.PrefetchScalarGridSpec(
            num_scalar_prefetch=5,
            grid=(nt // g,),
            in_specs=[
                pl.BlockSpec((g, 8, tm), lambda i, *_: (i, 0, 0)),
                pl.BlockSpec((g, tm, d), lambda i, *_: (i, 0, 0)),
            ],
            out_specs=pl.BlockSpec(memory_space=pl.ANY),
            scratch_shapes=[
                pltpu.VMEM((2 * g, lrows, dx), _BF16),
                pltpu.VMEM((GMM_ROW_TILE, dx), _BF16),
                pltpu.SemaphoreType.DMA((2 * g,)),
                pltpu.SemaphoreType.DMA((2,)),
            ],
        ),
        out_shape=jax.ShapeDtypeStruct((n_rows, dx), _BF16),
        compiler_params=pltpu.CompilerParams(dimension_semantics=("arbitrary",)),
        name="dispatch",
    )(plan["nch"], plan["cmap"], plan["tail_first"], plan["n_tail"], plan["n_used"], metat, h2)

    ys = pl.pallas_call(
        _experts_kernel,
        grid_spec=pltpu.PrefetchScalarGridSpec(
            num_scalar_prefetch=3,
            grid=(n_gmm_tiles,),
            in_specs=[
                pl.BlockSpec((GMM_ROW_TILE, dx), lambda i, te, tf, nu: (jnp.minimum(i, nu[0] - 1), 0)),
                pl.BlockSpec((1, d, ff), lambda i, te, tf, nu: (te[i], 0, 0)),
                pl.BlockSpec((1, d, ff), lambda i, te, tf, nu: (te[i], 0, 0)),
                pl.BlockSpec((1, ff, d), lambda i, te, tf, nu: (te[i], 0, 0)),
            ],
            out_specs=pl.BlockSpec((GMM_ROW_TILE, d), lambda i, te, tf, nu: (i, 0)),
            scratch_shapes=[
                pltpu.VMEM((d, ff), _BF16),
                pltpu.VMEM((d, ff), _BF16),
                pltpu.VMEM((ff, d), _BF16),
            ],
        ),
        out_shape=jax.ShapeDtypeStruct((n_rows, d), _BF16),
        compiler_params=pltpu.CompilerParams(dimension_semantics=("arbitrary",),
                                             vmem_limit_bytes=VMEM_LIMIT_BYTES),
        name="experts",
    )(plan["tile_e"], plan["tile_first"], plan["n_used"], xs, w1, w3, w2)

    out = pl.pallas_call(
        functools.partial(_combine_kernel, max_chunks=max_chunks),
        grid_spec=pltpu.PrefetchScalarGridSpec(
            num_scalar_prefetch=2,
            grid=(nt // g,),
            in_specs=[
                pl.BlockSpec((g, tm, d), lambda i, *_: (i, 0, 0)),
                pl.BlockSpec((1, n_ada, d), lambda i, *_: (i // steps_per_batch, 0, 0)),
                pl.BlockSpec((1, d), lambda i, *_: (0, 0)),
                pl.BlockSpec((g, tm, LANES), lambda i, *_: (i, 0, 0)),
                pl.BlockSpec(memory_space=pl.ANY),
            ],
            out_specs=pl.BlockSpec((g, tm, d), lambda i, *_: (i, 0, 0)),
            scratch_shapes=[
                pltpu.VMEM((2 * g, lrows, d), _BF16),
                pltpu.SemaphoreType.DMA((2 * g,)),
            ],
        ),
        out_shape=jax.ShapeDtypeStruct((nt, tm, d), _F32),
        compiler_params=pltpu.CompilerParams(dimension_semantics=("arbitrary",)),
        name="combine",
    )(plan["nch"], plan["cmap"], xt, ada, row(g_post), meta, ys)
    return out.reshape(b, s, d)


def kernel(x, c, w_ada, b_ada, g_pre_mix, g_post_mix, w_in, ln_v_g, ln_v_b, w_spatial, b_spatial,
           lb_logits, g_hgrn_norm, w_out, g_pre_ffn, g_post_ffn, w_router_group, b_router_group,
           w_router_expert, b_router_expert, w1, w3, w2):
    depth = w_in.shape[0]
    b, s, d = x.shape
    lb_all = _lower_bounds(lb_logits)
    for layer in range(depth):
        ada = _ada(c, w_ada[layer], b_ada[layer]).reshape(b, 6, d)
        mix = functools.partial(
            _mixer, x, ada, g_pre_mix[layer], g_post_mix[layer], w_in[layer], ln_v_g[layer],
            ln_v_b[layer], w_spatial[layer], b_spatial[layer], lb_all[layer], g_hgrn_norm[layer],
            w_out[layer], g_pre_ffn[layer], w_router_group[layer], b_router_group[layer],
            w_router_expert[layer], b_router_expert[layer])
        *mixed, span = mix(stepwise=False)
        x, h2, meta, metat, cnt = lax.cond(
            jnp.max(span) > HGRN_MAX_SPAN,
            lambda: tuple(mix(stepwise=True)[:5]),
            lambda: tuple(mixed))
        x = _moe(x, h2, meta, metat, cnt, ada, g_post_ffn[layer], w_router_group.shape[-1],
                 w1[layer], w3[layer], w2[layer])
    return x
```

```python
import functools

import jax
import jax.numpy as jnp
from jax import lax
from jax.experimental import pallas as pl
from jax.experimental.pallas import tpu as pltpu

RMS_EPS = 1e-6
LN_EPS = 1e-5
HGRN_HEAD_DIM = 128
HGRN_CHUNK = 128
HGRN_MAX_SPAN = 60.0
N_IN_SLICES = 8
TOP_K_IN_GROUP = 2
LANES = 128
MIXER_SEQ_TILE = 256
PROJ_COLS = 256
CHUNK_ROWS = 16
GMM_ROW_TILE = 1024
MOE_TILES_PER_STEP = 4
VMEM_LIMIT_BYTES = 56 * 1024 * 1024

_F32 = jnp.float32
_BF16 = jnp.bfloat16


def _sigmoid(x):
    return 0.5 * (jnp.tanh(0.5 * x) + 1.0)


def _silu(x):
    return x * _sigmoid(x)


def _gelu_tanh(x):
    c = 0.7978845608028654
    return 0.5 * x * (1.0 + jnp.tanh(c * (x + 0.044715 * (x * x * x))))


def _rms(x):
    return x * lax.rsqrt(jnp.mean(x * x, axis=-1, keepdims=True) + RMS_EPS)


def _dot(a, b):
    return jnp.dot(a, b, preferred_element_type=_F32)


def _dot_nt(a, b):
    return lax.dot_general(a, b, (((1,), (1,)), ((), ())), preferred_element_type=_F32)


def _dot_tn(a, b):
    return lax.dot_general(a, b, (((0,), (0,)), ((), ())), preferred_element_type=_F32)


def _split3(x):
    hi = x.astype(_BF16)
    r = x - hi.astype(_F32)
    mid = r.astype(_BF16)
    lo = (r - mid.astype(_F32)).astype(_BF16)
    return hi, mid, lo


def _ada_kernel(c_ref, w_ref, b_ref, o_ref):
    s = _silu(c_ref[...])
    w = w_ref[...]
    acc = jnp.zeros(o_ref.shape, _F32)
    for sp in _split3(s):
        for wp in _split3(w)[:2]:
            acc = acc + _dot(sp, wp)
    o_ref[...] = acc + b_ref[...]


def _ada(c, w_ada, b_ada):
    b, d = c.shape
    n = w_ada.shape[1]
    tn = d
    return pl.pallas_call(
        _ada_kernel,
        grid=(n // tn,),
        in_specs=[
            pl.BlockSpec((b, d), lambda j: (0, 0)),
            pl.BlockSpec((d, tn), lambda j: (0, j)),
            pl.BlockSpec((1, tn), lambda j: (0, j)),
        ],
        out_specs=pl.BlockSpec((b, tn), lambda j: (0, j)),
        out_shape=jax.ShapeDtypeStruct((b, n), _F32),
        name="ada",
    )(c, w_ada, b_ada.reshape(1, n))


def _lb_kernel(l_ref, o_ref):
    l = l_ref[...]
    e = jnp.exp(l - jnp.max(l, axis=0, keepdims=True))
    p = e / jnp.sum(e, axis=0, keepdims=True)
    rows = [p[0:1]]
    for i in range(1, p.shape[0]):
        rows.append(rows[-1] + p[i:i + 1])
    for i, r in enumerate(rows):
        o_ref[i:i + 1, :] = r


def _lower_bounds(lb_logits):
    return pl.pallas_call(
        _lb_kernel,
        out_shape=jax.ShapeDtypeStruct(lb_logits.shape, _F32),
        name="lower_bounds",
    )(lb_logits.astype(_F32))


def _mixer_kernel(x_ref, ada_ref, adap_ref, gpre_ref, gpost_ref, win_ref, lng_ref, lnb_ref, wsp_ref,
                  bspt_ref, lb_ref, ghg_ref, wout_ref, gpre2_ref, wrt_ref, brt_ref,
                  o_ref, h2_ref, meta_ref, metat_ref, cnt_ref, span_ref, st_ref, ya_ref, ob_ref,
                  x1p_ref, act_ref, iv_ref, vn_ref,
                  *, n_route_groups, per_group, tiles_per_batch, stepwise):
    ts, d = x_ref.shape[1], x_ref.shape[2]
    w = wout_ref.shape[0]
    n_groups, gchunk = wsp_ref.shape[0], wsp_ref.shape[1]
    gdim = w // n_groups
    t = pl.program_id(0)
    is_tile = t < pl.num_programs(0) - 1

    @pl.when(t % tiles_per_batch == 0)
    def _():
        st_ref[...] = jnp.zeros(st_ref.shape, _F32)

    @pl.when(t == 0)
    def _():
        x1p_ref[...] = jnp.zeros(x1p_ref.shape, _F32)

    x1_prev = x1p_ref[...]
    adap = adap_ref[0]
    _route_tile(x1_prev, adap[3:4], adap[4:5], gpre2_ref[...], wrt_ref[...], brt_ref[...],
                h2_ref, meta_ref, metat_ref, cnt_ref, n_route_groups, per_group)

    x = x_ref[0]
    ada = ada_ref[0]
    sh1, sc1, gt1 = ada[0:1], ada[1:2], ada[2:3]
    h = (_rms(x) * gpre_ref[...]) * (1.0 + sc1) + sh1
    h_bf = h.astype(_BF16)

    lb = lb_ref[...]
    (s_v, s_u, s_q, s_f, s_i, s_og, s_ga, s_gb) = (1, 0, 2, 3, 4, 5, 6, 7)
    (a_u, a_v, a_q, a_lf, a_k, a_og, a_ga, a_gb) = range(8)

    def activate(j, cols, r):
        if j == s_u:
            act_ref[a_u, :, cols] = _gelu_tanh(r)
        elif j == s_v:
            act_ref[a_v, :, cols] = _gelu_tanh(r)
        elif j == s_q:
            act_ref[a_q, :, cols] = _silu(r)
        elif j == s_f:
            f = lb[:, cols] + (1.0 - lb[:, cols]) * _sigmoid(r)
            act_ref[a_lf, :, cols] = jnp.log(f)
            act_ref[a_k, :, cols] = 1.0 - f
        elif j == s_i:
            iv_ref[:, cols] = r.astype(_BF16)
        elif j == s_og:
            act_ref[a_og, :, cols] = _silu(r)
        elif j == s_ga:
            act_ref[a_ga, :, cols] = _sigmoid(r)
        else:
            act_ref[a_gb, :, cols] = _sigmoid(r)

    def layer_norm_v():
        v = act_ref[a_v]
        mu = jnp.mean(v, axis=-1, keepdims=True)
        vc = v - mu
        var = jnp.mean(vc * vc, axis=-1, keepdims=True)
        vn_ref[...] = ((vc * lax.rsqrt(var + LN_EPS)) * lng_ref[...] + lnb_ref[...]).astype(_BF16)

    order = (s_v, s_u, s_q, s_f, s_i, s_og, s_ga, s_gb)
    pending = None
    for pos, j in enumerate(order):
        for cb in range(w // PROJ_COLS):
            cols = slice(cb * PROJ_COLS, (cb + 1) * PROJ_COLS)
            r = _dot(h_bf, win_ref[:, j * w + cb * PROJ_COLS:j * w + (cb + 1) * PROJ_COLS])
            if pending is not None:
                pending()
            pending = functools.partial(activate, j, cols, r)
            if pos == 1 and cb == 0:
                layer_norm_v()
    pending()

    u = act_ref[a_u]
    vn = vn_ref[...]
    tri_g = (lax.broadcasted_iota(jnp.int32, (gchunk, gchunk), 0)
             >= lax.broadcasted_iota(jnp.int32, (gchunk, gchunk), 1))
    bspt = bspt_ref[...]
    n_gchunks = ts // gchunk
    for g in range(n_groups):
        wc = jnp.where(tri_g, wsp_ref[g], 0.0).astype(_BF16)
        cols = slice(g * gdim, (g + 1) * gdim)
        vg = jnp.concatenate([vn[n * gchunk:(n + 1) * gchunk, cols] for n in range(n_gchunks)], axis=1)
        zv = _dot(wc, vg)
        for n in range(n_gchunks):
            rows = slice(n * gchunk, (n + 1) * gchunk)
            ya_ref[rows, cols] = u[rows, cols] * (zv[:, n * gdim:(n + 1) * gdim] + bspt[:, g:g + 1])

    pair = 2 * HGRN_HEAD_DIM
    pr = lax.broadcasted_iota(jnp.int32, (pair, pair), 0)
    pc_i = lax.broadcasted_iota(jnp.int32, (pair, pair), 1)
    same_head = (pr < HGRN_HEAD_DIM) == (pc_i < HGRN_HEAD_DIM)
    if stepwise:
        _hgrn_stepwise(act_ref.at[a_q], act_ref.at[a_k], iv_ref, st_ref, ob_ref, same_head)
        span_ref[0] = jnp.zeros(span_ref.shape[1:], _F32)
    else:
        _hgrn_chunkwise(act_ref.at[a_q], act_ref.at[a_lf], act_ref.at[a_k], iv_ref, st_ref, ob_ref,
                        span_ref, same_head)

    yb = (ob_ref[...] * ghg_ref[...]) * act_ref[a_og]
    y = act_ref[a_ga] * ya_ref[...] + act_ref[a_gb] * yb
    out = _dot(y.astype(_BF16), wout_ref[...])
    x1 = jnp.where(is_tile, x + gt1 * (_rms(out) * gpost_ref[...]), x1_prev)
    o_ref[0] = x1
    x1p_ref[...] = x1


def _hgrn_stepwise(q_ref, k_ref, iv_ref, st_ref, ob_ref, same_head):
    ts, w = q_ref.shape
    pair = 2 * HGRN_HEAD_DIM
    sub = 16

    def group(gi, carry):
        r0 = pl.multiple_of(gi * sub, sub)
        q_g = q_ref[pl.ds(r0, sub), :]
        k_g = k_ref[pl.ds(r0, sub), :]
        v_g = iv_ref[pl.ds(r0, sub), :].astype(_F32)
        o_rows = []
        for i in range(sub):
            q_t, k_t, v_t = q_g[i:i + 1], k_g[i:i + 1], v_g[i:i + 1]
            o_t = []
            for p in range(w // pair):
                pcols = slice(p * pair, (p + 1) * pair)
                upd = _dot_tn(v_t[:, pcols].astype(_BF16), k_t[:, pcols].astype(_BF16))
                st = st_ref[p] * (1.0 - k_t[:, pcols]) + jnp.where(same_head, upd, 0.0)
                st_ref[p] = st
                o_p = _dot_nt(q_t[:, pcols].astype(_BF16), st.astype(_BF16))
                o_t += [_rms(o_p[:, :HGRN_HEAD_DIM]), _rms(o_p[:, HGRN_HEAD_DIM:])]
            o_rows.append(jnp.concatenate(o_t, axis=1))
        ob_ref[pl.ds(r0, sub), :] = jnp.concatenate(o_rows, axis=0)
        return carry

    lax.fori_loop(0, ts // sub, group, 0)


def _hgrn_chunkwise(q_ref, lf_ref, k_ref, iv_ref, st_ref, ob_ref, span_ref, same_head):
    ts, w = q_ref.shape
    n_heads = w // HGRN_HEAD_DIM
    c = HGRN_CHUNK
    q = q_ref[...]
    lf = lf_ref[...]
    k = k_ref[...]
    iv = iv_ref[...]
    r_i = lax.broadcasted_iota(jnp.int32, (ts, ts), 0)
    c_i = lax.broadcasted_iota(jnp.int32, (ts, ts), 1)
    ltri = jnp.where((r_i // c == c_i // c) & (c_i <= r_i), 1.0, 0.0).astype(_BF16)
    bcum = jnp.zeros((ts, w), _F32)
    for part in _split3(lf)[:2]:
        bcum = bcum + _dot(ltri, part)
    hb = c // 2
    row_h = lax.broadcasted_iota(jnp.int32, (hb, c), 0)
    lane_h = lax.broadcasted_iota(jnp.int32, (hb, c), 1)
    top_mask = lane_h <= row_h
    left = lane_h < hb
    bot_mask = lane_h - hb <= row_h
    pair = 2 * HGRN_HEAD_DIM
    up_rows = lax.broadcasted_iota(jnp.int32, (c, pair), 1) < HGRN_HEAD_DIM
    span = jnp.zeros((1, 1), _F32)
    for n in range(ts // c):
        rows = slice(n * c, (n + 1) * c)
        bc = bcum[rows]
        b_a, b_m, b_b, b_l = (bc[hb // 2 - 1:hb // 2], bc[hb - 1:hb],
                              bc[hb + hb // 2 - 1:hb + hb // 2], bc[c - 1:c])
        stack = lambda ra, rb: jnp.concatenate(
            [jnp.broadcast_to(ra, (hb, w)), jnp.broadcast_to(rb, (hb, w))], axis=0)
        ref = stack(b_a, b_b)
        ends = jnp.maximum(jnp.maximum(bc[0:1] - b_a, b_a - b_m),
                           jnp.maximum(bc[hb:hb + 1] - b_b, b_b - b_l))
        span = jnp.maximum(span, jnp.max(ends, axis=1, keepdims=True))
        qx = q[rows] * jnp.exp(bc - ref)
        kx = k[rows] * jnp.exp(ref - bc)
        qd = (qx * stack(jnp.exp(b_a), jnp.exp(b_b))).astype(_BF16)
        kd = (kx * stack(jnp.exp(b_l - b_a), jnp.exp(b_l - b_b))).astype(_BF16)
        q_off = qx[hb:] * jnp.exp(b_b - b_a)
        q3 = jnp.concatenate([qx, q_off], axis=0).astype(_BF16)
        k3 = kx.astype(_BF16)
        dec = jnp.exp(b_l)
        iv_c = iv[rows]
        for p in range(n_heads // 2):
            pcols = slice(p * pair, (p + 1) * pair)
            a_heads = []
            for hd in (2 * p, 2 * p + 1):
                cols = slice(hd * HGRN_HEAD_DIM, (hd + 1) * HGRN_HEAD_DIM)
                sc = _dot_nt(q3[:, cols], k3[:, cols])
                top = jnp.where(top_mask, sc[0:hb], 0.0)
                bot = jnp.where(left, sc[2 * hb:3 * hb], jnp.where(bot_mask, sc[hb:2 * hb], 0.0))
                a_heads.append(jnp.concatenate([top, bot], axis=0))
            a_pair = jnp.concatenate(a_heads, axis=1).astype(_BF16)
            iv_p = iv_c[:, pcols]
            iv_blk = jnp.concatenate([jnp.where(up_rows, iv_p, 0.0).astype(_BF16),
                                      jnp.where(up_rows, 0.0, iv_p).astype(_BF16)], axis=0)
            st = st_ref[p]
            o = _dot(a_pair, iv_blk) + _dot_nt(qd[:, pcols], st.astype(_BF16))
            upd = _dot_tn(iv_p, kd[:, pcols])
            st_ref[p] = st * dec[:, pcols] + jnp.where(same_head, upd, 0.0)
            for j in range(2):
                hcols = slice(j * HGRN_HEAD_DIM, (j + 1) * HGRN_HEAD_DIM)
                ob_ref[rows, p * pair + j * HGRN_HEAD_DIM:p * pair + (j + 1) * HGRN_HEAD_DIM] = _rms(o[:, hcols])
    span_ref[0] = jnp.broadcast_to(span, span_ref.shape[1:])


def _mixer(x, ada, g_pre, g_post, w_in, ln_g, ln_b, w_spatial, b_spatial, lb, g_hgrn, w_out,
           g_pre_ffn, w_rg, b_rg, w_re, b_re, stepwise):
    b, s, d = x.shape
    w = w_out.shape[0]
    n_groups, gchunk, _ = w_spatial.shape
    ts = min(MIXER_SEQ_TILE, s)
    assert s % ts == 0 and ts % gchunk == 0 and ts % HGRN_CHUNK == 0 and ts % LANES == 0
    n_rg, n_exp = w_rg.shape[1], w_re.shape[1]
    assert n_rg + n_exp <= LANES
    pad = LANES - n_rg - n_exp
    wrt = jnp.concatenate([w_rg, w_re, jnp.zeros((d, pad), _F32)], axis=1).T.astype(_BF16)
    brt = jnp.concatenate([b_rg, b_re, jnp.zeros((pad,), _F32)]).reshape(LANES, 1).astype(_F32)
    tiles = s // ts
    nt = b * tiles
    cur = lambda t: jnp.minimum(t, nt - 1)
    prev = lambda t: jnp.maximum(t - 1, 0)
    cur3 = lambda t: (cur(t), 0, 0)
    prev3 = lambda t: (prev(t), 0, 0)
    assert w % HGRN_HEAD_DIM == 0 and w // n_groups == LANES and w_in.shape == (d, N_IN_SLICES * w)
    assert w % PROJ_COLS == 0
    n_heads = w // HGRN_HEAD_DIM
    assert n_heads % 2 == 0 and HGRN_HEAD_DIM == LANES
    row = lambda a: a.reshape(1, -1).astype(_F32)
    const2 = lambda t: (0, 0)
    const3 = lambda t: (0, 0, 0)
    single = dict(pipeline_mode=pl.Buffered(1))
    x1, h2, meta, metat, cnt, span = pl.pallas_call(
        functools.partial(_mixer_kernel, n_route_groups=n_rg, per_group=n_exp // n_rg,
                          tiles_per_batch=tiles, stepwise=stepwise),
        grid=(nt + 1,),
        in_specs=[
            pl.BlockSpec((1, ts, d), cur3),
            pl.BlockSpec((1, ada.shape[1], d), lambda t: (cur(t) // tiles, 0, 0)),
            pl.BlockSpec((1, ada.shape[1], d), lambda t: (prev(t) // tiles, 0, 0)),
            pl.BlockSpec((1, d), const2),
            pl.BlockSpec((1, d), const2),
            pl.BlockSpec((d, N_IN_SLICES * w), const2, **single),
            pl.BlockSpec((1, w), const2),
            pl.BlockSpec((1, w), const2),
            pl.BlockSpec((n_groups, gchunk, gchunk), const3, **single),
            pl.BlockSpec((gchunk, n_groups), const2),
            pl.BlockSpec((1, w), const2),
            pl.BlockSpec((1, w), const2),
            pl.BlockSpec((w, d), const2, **single),
            pl.BlockSpec((1, d), const2),
            pl.BlockSpec((LANES, d), const2),
            pl.BlockSpec((LANES, 1), const2),
        ],
        out_specs=[
            pl.BlockSpec((1, ts, d), cur3),
            pl.BlockSpec((1, ts, d), prev3),
            pl.BlockSpec((1, ts, LANES), prev3),
            pl.BlockSpec((1, 8, ts), prev3),
            pl.BlockSpec((1, 8, LANES), prev3),
            pl.BlockSpec((1, 8, LANES), cur3),
        ],
        out_shape=[
            jax.ShapeDtypeStruct((nt, ts, d), _F32),
            jax.ShapeDtypeStruct((nt, ts, d), _BF16),
            jax.ShapeDtypeStruct((nt, ts, LANES), _F32),
            jax.ShapeDtypeStruct((nt, 8, ts), _F32),
            jax.ShapeDtypeStruct((nt, 8, LANES), jnp.int32),
            jax.ShapeDtypeStruct((nt, 8, LANES), _F32),
        ],
        scratch_shapes=[
            pltpu.VMEM((n_heads // 2, 2 * HGRN_HEAD_DIM, 2 * HGRN_HEAD_DIM), _F32),
            pltpu.VMEM((ts, w), _F32),
            pltpu.VMEM((ts, w), _F32),
            pltpu.VMEM((ts, d), _F32),
            pltpu.VMEM((N_IN_SLICES, ts, w), _F32),
            pltpu.VMEM((ts, w), _BF16),
            pltpu.VMEM((ts, w), _BF16),
        ],
        compiler_params=pltpu.CompilerParams(
            dimension_semantics=("arbitrary",),
            vmem_limit_bytes=VMEM_LIMIT_BYTES),
        name="mixer",
    )(x.reshape(nt, ts, d), ada, ada, row(g_pre), row(g_post), w_in.astype(_BF16), row(ln_g),
      row(ln_b), w_spatial.astype(_F32), b_spatial.T.astype(_F32), row(lb), row(g_hgrn),
      w_out.astype(_BF16), row(g_pre_ffn), wrt, brt)
    return x1.reshape(b, s, d), h2, meta, metat, cnt, span


def _first_max(vals):
    m = functools.reduce(jnp.maximum, vals)
    idx = jnp.full(m.shape, len(vals) - 1, jnp.int32)
    for j in range(len(vals) - 2, -1, -1):
        idx = jnp.where(vals[j] == m, j, idx)
    return m, idx


def _route(lt, n_groups, per_group):
    lg = [lt[g:g + 1] for g in range(n_groups)]
    mg, g_idx = _first_max(lg)
    p_top = 1.0 / functools.reduce(lambda a, b: a + b, [jnp.exp(l - mg) for l in lg])
    le = []
    for j in range(per_group):
        v = lt[n_groups + j:n_groups + j + 1]
        for g in range(1, n_groups):
            r0 = n_groups + g * per_group + j
            v = jnp.where(g_idx == g, lt[r0:r0 + 1], v)
        le.append(v)
    m1, j1 = _first_max(le)
    m2, j2 = _first_max([jnp.where(j1 == j, -jnp.inf, v) for j, v in enumerate(le)])
    r = jnp.exp(m2 - m1)
    w1 = p_top / (1.0 + r)
    w2 = w1 * r
    return g_idx * per_group + j1, g_idx * per_group + j2, w1, w2


def _route_tile(x, sh2, sc2, gpre, wrt, brt, h_ref, meta_ref, metat_ref, cnt_ref, n_groups, per_group):
    tm = x.shape[0]
    h = (_rms(x) * gpre) * (1.0 + sc2) + sh2
    hb = h.astype(_BF16)
    h_ref[0] = hb
    lt = _dot_nt(wrt, hb) + brt
    e1, e2, w1, w2 = _route(lt, n_groups, per_group)

    sub = lax.broadcasted_iota(jnp.int32, (LANES, tm), 0)
    ind = jnp.where((sub == e1) | (sub == e2), 1.0, 0.0)
    earlier = (lax.broadcasted_iota(jnp.int32, (tm, tm), 0)
               < lax.broadcasted_iota(jnp.int32, (tm, tm), 1))
    rank = _dot(ind.astype(_BF16), jnp.where(earlier, 1.0, 0.0).astype(_BF16))
    cnt = jnp.sum(ind, axis=1, keepdims=True)
    chunks = jnp.floor((cnt + (CHUNK_ROWS - 1)) * (1.0 / CHUNK_ROWS))
    below = (lax.broadcasted_iota(jnp.int32, (LANES, LANES), 1)
             < lax.broadcasted_iota(jnp.int32, (LANES, LANES), 0))
    first_chunk = _dot(jnp.where(below, 1.0, 0.0).astype(_BF16),
                       jnp.broadcast_to(chunks, (LANES, LANES)).astype(_BF16))
    dest = first_chunk[:, 0:1] * CHUNK_ROWS + rank
    n_rows = -(-n_groups * per_group // 8) * 8
    sub_e = lax.broadcasted_iota(jnp.int32, (n_rows, tm), 0)
    d1 = jnp.sum(jnp.where(sub_e == e1, dest[:n_rows], 0.0), axis=0, keepdims=True)
    d2 = jnp.sum(jnp.where(sub_e == e2, dest[:n_rows], 0.0), axis=0, keepdims=True)
    row8 = lax.broadcasted_iota(jnp.int32, (8, tm), 0)
    metat = jnp.where(row8 == 0, d1, jnp.where(row8 == 1, d2,
                      jnp.where(row8 == 2, w1, jnp.where(row8 == 3, w2, 0.0))))
    metat_ref[0] = metat
    meta_ref[0] = jnp.concatenate([metat, jnp.zeros((LANES - 8, tm), _F32)], axis=0).T
    cnt_ref[0] = jnp.broadcast_to(cnt, (LANES, LANES)).T[0:8].astype(jnp.int32)


def _chunk_loop(n, fn):
    def body(c, carry):
        fn(c)
        return carry
    lax.fori_loop(0, n, body, 0)


def _start_chunks(n, copy_of_chunk):
    _chunk_loop((n + 1) // 2, lambda h: copy_of_chunk(2 * h).start(priority=0))
    _chunk_loop(n // 2, lambda h: copy_of_chunk(2 * h + 1).start(priority=1))


def _wait_chunks(n, max_chunks, copy_of_chunks):
    bit = 1
    while bit * 2 <= max_chunks:
        bit *= 2
    while bit >= 1:
        @pl.when((n & bit) != 0)
        def _(bit=bit):
            copy_of_chunks(bit).wait()
        bit //= 2


def _dispatch_kernel(nch_ref, cmap_ref, tbase_ref, ntail_ref, nu_ref, metat_ref, h_ref, xs_ref,
                     loc_ref, zero_ref, sem, zsem, *, max_chunks):
    i = pl.program_id(0)
    n_steps = pl.num_programs(0)
    g = h_ref.shape[0]
    lrows, tm = loc_ref.shape[1], h_ref.shape[1]
    buf = lambda step, j: (step % 2) * g + j

    def chunk_copy(slot_, c, dst_chunk):
        return pltpu.make_async_copy(
            loc_ref.at[slot_, pl.ds(pl.multiple_of(c * CHUNK_ROWS, CHUNK_ROWS), CHUNK_ROWS)],
            xs_ref.at[pl.ds(pl.multiple_of(dst_chunk * CHUNK_ROWS, CHUNK_ROWS), CHUNK_ROWS)],
            sem.at[slot_])

    def zero_copy(dst_chunk):
        return pltpu.make_async_copy(
            zero_ref.at[pl.ds(0, CHUNK_ROWS)],
            xs_ref.at[pl.ds(pl.multiple_of(dst_chunk * CHUNK_ROWS, CHUNK_ROWS), CHUNK_ROWS)],
            zsem.at[0])

    def zero_tile_copy(tile):
        return pltpu.make_async_copy(
            zero_ref,
            xs_ref.at[pl.ds(pl.multiple_of(tile * GMM_ROW_TILE, GMM_ROW_TILE), GMM_ROW_TILE)],
            zsem.at[1])

    @pl.when(i == 0)
    def _():
        zero_ref[...] = jnp.zeros(zero_ref.shape, zero_ref.dtype)
        n_unused = xs_ref.shape[0] // GMM_ROW_TILE - nu_ref[0]
        for e in range(tbase_ref.shape[0]):
            _chunk_loop(ntail_ref[e], lambda j, e=e: zero_copy(tbase_ref[e] + j).start())
        _chunk_loop(n_unused, lambda j: zero_tile_copy(nu_ref[0] + j).start())
        for e in range(tbase_ref.shape[0]):
            _chunk_loop(ntail_ref[e], lambda j: zero_copy(0).wait())
        _chunk_loop(n_unused, lambda j: zero_tile_copy(0).wait())

    def wait_slot(slot_, n_):
        _wait_chunks(n_, max_chunks, lambda k: pltpu.make_async_copy(
            loc_ref.at[slot_, pl.ds(0, k * CHUNK_ROWS)], xs_ref.at[pl.ds(0, k * CHUNK_ROWS)],
            sem.at[slot_]))

    @pl.when(i >= 2)
    def _():
        for j in range(g):
            wait_slot(buf(i, j), nch_ref[(i - 2) * g + j])

    r = lax.broadcasted_iota(jnp.int32, (lrows, tm), 0)
    lane = lax.broadcasted_iota(jnp.int32, (lrows, LANES), 1)
    d = h_ref.shape[2]
    for j in range(g):
        mt = metat_ref[j]
        is1 = r == mt[0:1].astype(jnp.int32)
        is2 = r == mt[1:2].astype(jnp.int32)
        sel = jnp.where(is1 | is2, 1.0, 0.0).astype(_BF16)
        loc_ref[buf(i, j), :, 0:d] = _dot(sel, h_ref[j]).astype(_BF16)
        ws = jnp.sum(jnp.where(is1, mt[2:3], 0.0) + jnp.where(is2, mt[3:4], 0.0),
                     axis=1, keepdims=True)
        hi, mid, lo = (p.astype(_F32) for p in _split3(ws))
        loc_ref[buf(i, j), :, d:d + LANES] = jnp.where(
            lane == 0, hi, jnp.where(lane == 1, mid, jnp.where(lane == 2, lo, 0.0))).astype(_BF16)
    for j in range(g):
        tile = i * g + j
        _start_chunks(nch_ref[tile], lambda c, j=j, tile=tile: chunk_copy(
            buf(i, j), c, cmap_ref[tile * max_chunks + c]))

    @pl.when(i == n_steps - 1)
    def _():
        for j in range(g):
            wait_slot(buf(i, j), nch_ref[i * g + j])

        @pl.when(i >= 1)
        def _():
            for j in range(g):
                wait_slot(buf(i - 1, j), nch_ref[(i - 1) * g + j])


def _experts_kernel(te_ref, tf_ref, nu_ref, xs_ref, w1_ref, w3_ref, w2_ref, ys_ref,
                    wb1_ref, wb3_ref, wb2_ref):
    i = pl.program_id(0)

    @pl.when(i < nu_ref[0])
    def _():
        @pl.when(tf_ref[i] == 1)
        def _():
            wb1_ref[...] = w1_ref[0].astype(_BF16)
            wb3_ref[...] = w3_ref[0].astype(_BF16)
            wb2_ref[...] = w2_ref[0].astype(_BF16)

        d = wb1_ref.shape[0]
        xs = xs_ref[:, 0:d]
        half = wb1_ref.shape[1] // 2
        acc = None
        for hcols in (slice(0, half), slice(half, 2 * half)):
            a = _silu(_dot(xs, wb1_ref[:, hcols])) * _dot(xs, wb3_ref[:, hcols])
            part = _dot(a.astype(_BF16), wb2_ref[hcols, :])
            acc = part if acc is None else acc + part
        wp = xs_ref[:, d:d + LANES].astype(_F32)
        ys_ref[...] = (acc * (wp[:, 0:1] + wp[:, 1:2] + wp[:, 2:3])).astype(_BF16)

    @pl.when(i >= nu_ref[0])
    def _():
        ys_ref[...] = jnp.zeros(ys_ref.shape, ys_ref.dtype)


def _combine_kernel(nch_ref, cmap_ref, x_ref, ada_ref, gpost_ref, meta_ref, ys_ref, o_ref,
                    loc_ref, sem, *, max_chunks):
    i = pl.program_id(0)
    n_steps = pl.num_programs(0)
    g = x_ref.shape[0]
    lrows, tm = loc_ref.shape[1], x_ref.shape[1]
    buf = lambda step, j: (step % 2) * g + j

    def chunk_copy(tile, slot_, c):
        src_chunk = cmap_ref[tile * max_chunks + c]
        return pltpu.make_async_copy(
            ys_ref.at[pl.ds(pl.multiple_of(src_chunk * CHUNK_ROWS, CHUNK_ROWS), CHUNK_ROWS)],
            loc_ref.at[slot_, pl.ds(pl.multiple_of(c * CHUNK_ROWS, CHUNK_ROWS), CHUNK_ROWS)],
            sem.at[slot_])

    def fetch(step):
        for j in range(g):
            tile = step * g + j
            _start_chunks(nch_ref[tile], lambda c, j=j, tile=tile: chunk_copy(tile, buf(step, j), c))

    @pl.when(i == 0)
    def _():
        loc_ref[...] = jnp.zeros(loc_ref.shape, loc_ref.dtype)
        fetch(0)

    @pl.when(i + 1 < n_steps)
    def _():
        fetch(i + 1)

    for j in range(g):
        _wait_chunks(nch_ref[i * g + j], max_chunks, lambda k, j=j: pltpu.make_async_copy(
            ys_ref.at[pl.ds(0, k * CHUNK_ROWS)], loc_ref.at[buf(i, j), pl.ds(0, k * CHUNK_ROWS)],
            sem.at[buf(i, j)]))

    r = lax.broadcasted_iota(jnp.int32, (tm, lrows), 1)
    gt2 = ada_ref[0][5:6]
    for j in range(g):
        meta = meta_ref[j]
        d1 = meta[:, 0:1].astype(jnp.int32)
        d2 = meta[:, 1:2].astype(jnp.int32)
        sel = jnp.where((r == d1) | (r == d2), 1.0, 0.0).astype(_BF16)
        y = _dot(sel, loc_ref[buf(i, j)])
        o_ref[j] = x_ref[j] + gt2 * (_rms(y) * gpost_ref[...])


def _moe_plan(cnt, max_chunks, n_gmm_tiles):
    per_tile = GMM_ROW_TILE // CHUNK_ROWS
    pc = (cnt + (CHUNK_ROWS - 1)) // CHUNK_ROWS
    local_first = jnp.cumsum(pc, axis=1) - pc
    nch = jnp.sum(pc, axis=1)
    e_chunks = jnp.sum(pc, axis=0)
    e_region = ((e_chunks + per_tile - 1) // per_tile) * per_tile
    e_first = jnp.cumsum(e_region) - e_region
    seg_first = e_first[None, :] + jnp.cumsum(pc, axis=0) - pc
    c = jnp.arange(max_chunks, dtype=jnp.int32)[None, :, None]
    inside = (c >= local_first[:, None, :]) & (c < (local_first + pc)[:, None, :])
    cmap = jnp.sum(jnp.where(inside, seg_first[:, None, :] + c - local_first[:, None, :], 0), axis=-1)
    n_used = jnp.sum(e_region) // per_tile
    t = jnp.arange(n_gmm_tiles, dtype=jnp.int32)
    t_used = jnp.minimum(t, n_used - 1)
    tile_e = jnp.sum(t_used[:, None] * per_tile >= (e_first + e_region)[None, :], axis=1)
    tile_first = jnp.concatenate([jnp.ones((1,), jnp.int32),
                                  (tile_e[1:] != tile_e[:-1]).astype(jnp.int32)])
    i32 = lambda a: a.astype(jnp.int32)
    return dict(nch=i32(nch), cmap=i32(cmap.reshape(-1)), tail_first=i32(e_first + e_chunks),
                n_tail=i32(e_region - e_chunks), tile_e=i32(tile_e), tile_first=tile_first,
                n_used=i32(n_used.reshape(1)))


def _moe(x, h2, meta, metat, cnt, ada, g_post, n_groups, w1, w3, w2):
    b, s, d = x.shape
    n_exp, _, ff = w1.shape
    tm = h2.shape[1]
    assert s % tm == 0 and tm % LANES == 0 and GMM_ROW_TILE % CHUNK_ROWS == 0
    tiles_per_batch = s // tm
    nt = b * tiles_per_batch
    max_chunks = (TOP_K_IN_GROUP * tm + n_exp * (CHUNK_ROWS - 1)) // CHUNK_ROWS
    lrows = -(-max_chunks * CHUNK_ROWS // LANES) * LANES
    per_tile = GMM_ROW_TILE // CHUNK_ROWS
    n_gmm_tiles = -(-(nt * max_chunks + n_exp * (per_tile - 1)) // per_tile)
    n_rows = n_gmm_tiles * GMM_ROW_TILE
    dx = d + LANES

    row = lambda a: a.reshape(1, -1).astype(_F32)
    n_ada = ada.shape[1]
    xt = x.reshape(nt, tm, d)
    g = MOE_TILES_PER_STEP if tiles_per_batch % MOE_TILES_PER_STEP == 0 else 1
    steps_per_batch = tiles_per_batch // g

    plan = _moe_plan(cnt[:, 0, :n_exp], max_chunks, n_gmm_tiles)

    xs = pl.pallas_call(
        functools.partial(_dispatch_kernel, max_chunks=max_chunks),
        grid_spec=pltpu.PrefetchScalarGridSpec(
            num_scalar_prefetch=5,
            grid=(nt // g,),
            in_specs=[
                pl.BlockSpec((g, 8, tm), lambda i, *_: (i, 0, 0)),
                pl.BlockSpec((g, tm, d), lambda i, *_: (i, 0, 0)),
            ],
            out_specs=pl.BlockSpec(memory_space=pl.ANY),
            scratch_shapes=[
                pltpu.VMEM((2 * g, lrows, dx), _BF16),
                pltpu.VMEM((GMM_ROW_TILE, dx), _BF16),
                pltpu.SemaphoreType.DMA((2 * g,)),
                pltpu.SemaphoreType.DMA((2,)),
            ],
        ),
        out_shape=jax.ShapeDtypeStruct((n_rows, dx), _BF16),
        compiler_params=pltpu.CompilerParams(dimension_semantics=("arbitrary",)),
        name="dispatch",
    )(plan["nch"], plan["cmap"], plan["tail_first"], plan["n_tail"], plan["n_used"], metat, h2)

    ys = pl.pallas_call(
        _experts_kernel,
        grid_spec=pltpu.PrefetchScalarGridSpec(
            num_scalar_prefetch=3,
            grid=(n_gmm_tiles,),
            in_specs=[
                pl.BlockSpec((GMM_ROW_TILE, dx), lambda i, te, tf, nu: (jnp.minimum(i, nu[0] - 1), 0)),
                pl.BlockSpec((1, d, ff), lambda i, te, tf, nu: (te[i], 0, 0)),
                pl.BlockSpec((1, d, ff), lambda i, te, tf, nu: (te[i], 0, 0)),
                pl.BlockSpec((1, ff, d), lambda i, te, tf, nu: (te[i], 0, 0)),
            ],
            out_specs=pl.BlockSpec((GMM_ROW_TILE, d), lambda i, te, tf, nu: (i, 0)),
            scratch_shapes=[
                pltpu.VMEM((d, ff), _BF16),
                pltpu.VMEM((d, ff), _BF16),
                pltpu.VMEM((ff, d), _BF16),
            ],
        ),
        out_shape=jax.ShapeDtypeStruct((n_rows, d), _BF16),
        compiler_params=pltpu.CompilerParams(dimension_semantics=("arbitrary",),
                                             vmem_limit_bytes=VMEM_LIMIT_BYTES),
        name="experts",
    )(plan["tile_e"], plan["tile_first"], plan["n_used"], xs, w1, w3, w2)

    out = pl.pallas_call(
        functools.partial(_combine_kernel, max_chunks=max_chunks),
        grid_spec=pltpu.PrefetchScalarGridSpec(
            num_scalar_prefetch=2,
            grid=(nt // g,),
            in_specs=[
                pl.BlockSpec((g, tm, d), lambda i, *_: (i, 0, 0)),
                pl.BlockSpec((1, n_ada, d), lambda i, *_: (i // steps_per_batch, 0, 0)),
                pl.BlockSpec((1, d), lambda i, *_: (0, 0)),
                pl.BlockSpec((g, tm, LANES), lambda i, *_: (i, 0, 0)),
                pl.BlockSpec(memory_space=pl.ANY),
            ],
            out_specs=pl.BlockSpec((g, tm, d), lambda i, *_: (i, 0, 0)),
            scratch_shapes=[
                pltpu.VMEM((2 * g, lrows, d), _BF16),
                pltpu.SemaphoreType.DMA((2 * g,)),
            ],
        ),
        out_shape=jax.ShapeDtypeStruct((nt, tm, d), _F32),
        compiler_params=pltpu.CompilerParams(dimension_semantics=("arbitrary",)),
        name="combine",
    )(plan["nch"], plan["cmap"], xt, ada, row(g_post), meta, ys)
    return out.reshape(b, s, d)


def kernel(x, c, w_ada, b_ada, g_pre_mix, g_post_mix, w_in, ln_v_g, ln_v_b, w_spatial, b_spatial,
           lb_logits, g_hgrn_norm, w_out, g_pre_ffn, g_post_ffn, w_router_group, b_router_group,
           w_router_expert, b_router_expert, w1, w3, w2):
    depth = w_in.shape[0]
    b, s, d = x.shape
    lb_all = _lower_bounds(lb_logits)
    for layer in range(depth):
        ada = _ada(c, w_ada[layer], b_ada[layer]).reshape(b, 6, d)
        mix = functools.partial(
            _mixer, x, ada, g_pre_mix[layer], g_post_mix[layer], w_in[layer], ln_v_g[layer],
            ln_v_b[layer], w_spatial[layer], b_spatial[layer], lb_all[layer], g_hgrn_norm[layer],
            w_out[layer], g_pre_ffn[layer], w_router_group[layer], b_router_group[layer],
            w_router_expert[layer], b_router_expert[layer])
        *mixed, span = mix(stepwise=False)
        x, h2, meta, metat, cnt = lax.cond(
            jnp.max(span) > HGRN_MAX_SPAN,
            lambda: tuple(mix(stepwise=True)[:5]),
            lambda: tuple(mixed))
        x = _moe(x, h2, meta, metat, cnt, ada, g_post_ffn[layer], w_router_group.shape[-1],
                 w1[layer], w3[layer], w2[layer])
    return x
```
